```python
import jax, jax.numpy as jnp
from jax import lax
import numpy as np

D_MODEL = 2048
BATCH = 2
SEQ = 4096
DEPTH = 4

HEAD_DIM = 64
EPS = 1e-6
Q_BLOCK = 128
FOX_HEADS = 8
FOX_WIDTH = FOX_HEADS * HEAD_DIM
CONV_WIDTH = 512
CONV_TAPS = 3
SGU_GROUPS = 4
SGU_GROUP_DIM = 128
SGU_WIDTH = SGU_GROUPS * SGU_GROUP_DIM
SGU_CHUNK = 128
DIL_PATTERNS = ((128, 1), (512, 4), (2048, 16))
DIL_HEADS_PER_GROUP = 4
DIL_HEADS = DIL_HEADS_PER_GROUP * len(DIL_PATTERNS)
DIL_WIDTH = DIL_HEADS * HEAD_DIM
DIL_OUT = DIL_HEADS_PER_GROUP * HEAD_DIM
ROPE_THETA = 500000.0
ROPE_DIM = HEAD_DIM // 4
N_BRANCH = 4
IN_SIZES = (3 * FOX_WIDTH, FOX_HEADS, 3 * CONV_WIDTH, 2 * SGU_WIDTH, 3 * DIL_WIDTH, N_BRANCH * D_MODEL)
D_IN = 3 * FOX_WIDTH + FOX_HEADS + 3 * CONV_WIDTH + 2 * SGU_WIDTH + 3 * DIL_WIDTH + N_BRANCH * D_MODEL
D_FF = 5632
FFN_TAPS = 3
PLE_DIM = 256

kernel_name = "hybrid_parallel_gated_fox_conv_sgu_dilated"


def rms_norm(x, g):
    xf = x.astype(jnp.float32)
    var = jnp.mean(xf * xf, axis=-1, keepdims=True)
    return (xf * lax.rsqrt(var + EPS)).astype(x.dtype) * g


def causal_dwconv(z, w):
    K = w.shape[0]
    S = z.shape[1]
    zp = jnp.pad(z, ((0, 0), (K - 1, 0), (0, 0)))
    return sum(w[k] * zp[:, k:k + S] for k in range(K))


def partial_rope(x, positions):
    half = ROPE_DIM // 2
    inv = ROPE_THETA ** (-jnp.arange(half, dtype=jnp.float32) * (2.0 / ROPE_DIM))
    ang = positions.astype(jnp.float32)[..., None] * inv
    cos = jnp.cos(ang)[:, :, None, :]
    sin = jnp.sin(ang)[:, :, None, :]
    x1 = x[..., :half].astype(jnp.float32)
    x2 = x[..., half:ROPE_DIM].astype(jnp.float32)
    rot = jnp.concatenate([(x1 * cos - x2 * sin).astype(x.dtype),
                           (x1 * sin + x2 * cos).astype(x.dtype),
                           x[..., ROPE_DIM:]], axis=-1)
    return rot


def forgetting_attention(q, k, v, log_f):
    B, S, H, Dh = q.shape
    nb = S // Q_BLOCK
    F = jnp.cumsum(log_f, axis=1).transpose(0, 2, 1)
    qb = q.reshape(B, nb, Q_BLOCK, H, Dh).transpose(1, 0, 2, 3, 4)
    Fb = F.reshape(B, H, nb, Q_BLOCK).transpose(2, 0, 1, 3)
    kpos = jnp.arange(S)
    scale = Dh ** -0.5

    def one_block(args):
        blk, q_blk, f_blk = args
        s = jnp.einsum('bqhd,bkhd->bhqk', q_blk, k, preferred_element_type=jnp.float32) * scale
        s = s + (f_blk[..., :, None] - F[..., None, :])
        qpos = blk * Q_BLOCK + jnp.arange(Q_BLOCK)
        s = jnp.where(kpos[None, :] <= qpos[:, None], s, -jnp.inf)
        p = jax.nn.softmax(s, axis=-1).astype(v.dtype)
        return jnp.einsum('bhqk,bkhd->bqhd', p, v)

    out = lax.map(one_block, (jnp.arange(nb), qb, Fb))
    return out.transpose(1, 0, 2, 3, 4).reshape(B, S, H * Dh)


def dilated_window_attention(q, k, v, window, dilation):
    B, S, H, Dh = q.shape
    L = S // dilation
    span = window // dilation
    nb = -(-L // span)
    Lp = nb * span

    def strided_blocks(t):
        t = t.reshape(B, L, dilation, H, Dh).transpose(0, 2, 1, 3, 4)
        t = jnp.pad(t, ((0, 0), (0, 0), (0, Lp - L), (0, 0), (0, 0)))
        return t.reshape(B, dilation, nb, span, H, Dh)

    def with_prev(t):
        prev = jnp.pad(t[:, :, :-1], ((0, 0), (0, 0), (1, 0), (0, 0), (0, 0), (0, 0)))
        return jnp.concatenate([prev, t], axis=3)

    qs = strided_blocks(q)
    kc = with_prev(strided_blocks(k))
    vc = with_prev(strided_blocks(v))
    s = jnp.einsum('brnqhd,brnkhd->brnhqk', qs, kc, preferred_element_type=jnp.float32) * (Dh ** -0.5)
    blk = jnp.arange(nb)[:, None, None]
    qi = jnp.arange(span)[None, :, None]
    ki = jnp.arange(2 * span)[None, None, :]
    dist = span + qi - ki
    valid = (dist >= 0) & (dist <= span) & ((blk > 0) | (ki >= span))
    s = jnp.where(valid[:, None], s, -jnp.inf)
    lse = jax.nn.logsumexp(s, axis=-1, keepdims=True)
    p = jnp.exp(s - lse).astype(v.dtype)
    o = jnp.einsum('brnhqk,brnkhd->brnqhd', p, vc)
    o = o.reshape(B, dilation, Lp, H, Dh)[:, :, :L].transpose(0, 2, 1, 3, 4).reshape(B, S, H, Dh)
    lse = lse[..., 0].transpose(0, 1, 2, 4, 3).reshape(B, dilation, Lp, H)[:, :, :L]
    lse = lse.transpose(0, 2, 1, 3).reshape(B, S, H)
    return o, lse


def chunked_spatial_gating(z, norm_g, w_s, b_s):
    u, v = jnp.split(z, 2, axis=-1)
    v = rms_norm(v, norm_g)
    B, S, _ = v.shape
    nc = S // SGU_CHUNK
    vc = v.reshape(B, nc, SGU_CHUNK, SGU_GROUPS, SGU_GROUP_DIM)
    mask = jnp.tril(jnp.ones((SGU_CHUNK, SGU_CHUNK), dtype=bool))
    ws = jnp.where(mask[None], w_s, jnp.zeros_like(w_s))
    mixed = jnp.einsum('gts,bcsgd->bctgd', ws, vc) + b_s.T[None, None, :, :, None]
    return u * mixed.reshape(B, S, SGU_WIDTH)


def setup_inputs(seed: int = 0) -> dict:
    key = jax.random.key(seed)
    ks = jax.random.split(key, 24)

    def nrm(k, shape, scale):
        return jax.random.normal(k, shape, jnp.float32) * scale

    def gain(k, shape):
        return 1.0 + 0.1 * jax.random.normal(k, shape, jnp.float32)

    offset = jax.random.randint(ks[2], (BATCH, 1), 0, 1024, dtype=jnp.int32)
    positions = offset + jnp.arange(SEQ, dtype=jnp.int32)[None, :]
    return {
        "x": nrm(ks[0], (BATCH, SEQ, D_MODEL), 1.0),
        "p": nrm(ks[1], (DEPTH, BATCH, SEQ, PLE_DIM), 1.0),
        "positions": positions,
        "norm_mix_g": gain(ks[3], (DEPTH, D_MODEL)),
        "w_in": nrm(ks[4], (DEPTH, D_MODEL, D_IN), D_MODEL ** -0.5),
        "fox_forget_b": 2.0 + 3.0 * jax.random.uniform(ks[5], (DEPTH, FOX_HEADS), jnp.float32),
        "shortconv_w": nrm(ks[6], (DEPTH, CONV_TAPS, CONV_WIDTH), CONV_TAPS ** -0.5),
        "sgu_norm_g": gain(ks[7], (DEPTH, SGU_WIDTH)),
        "sgu_w": nrm(ks[8], (DEPTH, SGU_GROUPS, SGU_CHUNK, SGU_CHUNK), SGU_CHUNK ** -0.5),
        "sgu_b": gain(ks[9], (DEPTH, SGU_GROUPS, SGU_CHUNK)),
        "w_br_fox": nrm(ks[10], (DEPTH, FOX_WIDTH, D_MODEL), FOX_WIDTH ** -0.5),
        "w_br_conv": nrm(ks[11], (DEPTH, CONV_WIDTH, D_MODEL), CONV_WIDTH ** -0.5),
        "w_br_sgu": nrm(ks[12], (DEPTH, SGU_WIDTH, D_MODEL), SGU_WIDTH ** -0.5),
        "w_br_dil": nrm(ks[13], (DEPTH, DIL_OUT, D_MODEL), DIL_OUT ** -0.5),
        "w_out": nrm(ks[14], (DEPTH, D_MODEL, D_MODEL), D_MODEL ** -0.5),
        "norm_ffn_g": gain(ks[15], (DEPTH, D_MODEL)),
        "w_up": nrm(ks[16], (DEPTH, D_MODEL, 2 * D_FF), D_MODEL ** -0.5),
        "ffn_conv_w": nrm(ks[17], (DEPTH, FFN_TAPS, 2 * D_FF), FFN_TAPS ** -0.5),
        "w_down": nrm(ks[18], (DEPTH, D_FF, D_MODEL), D_FF ** -0.5),
        "norm_ple_g": gain(ks[19], (DEPTH, D_MODEL)),
        "w_ple_gate": nrm(ks[20], (DEPTH, D_MODEL, D_MODEL), D_MODEL ** -0.5),
        "w_ple_proj": nrm(ks[21], (DEPTH, PLE_DIM, D_MODEL), PLE_DIM ** -0.5),
        "final_norm_g": gain(ks[22], (D_MODEL,)),
    }


def reference(x, p, positions, norm_mix_g, w_in, fox_forget_b, shortconv_w, sgu_norm_g, sgu_w, sgu_b,
              w_br_fox, w_br_conv, w_br_sgu, w_br_dil, w_out, norm_ffn_g, w_up, ffn_conv_w, w_down,
              norm_ple_g, w_ple_gate, w_ple_proj, final_norm_g):
    B, S, _ = x.shape
    split_points = [int(c) for c in np.cumsum(IN_SIZES)[:-1]]
    for i in range(DEPTH):
        h = rms_norm(x, norm_mix_g[i])
        proj = h @ w_in[i]
        a_qkv, a_f, b_in, c_in, d_qkv, gate_logits = jnp.split(proj, split_points, axis=-1)

        a_qkv = a_qkv.reshape(B, S, 3, FOX_HEADS, HEAD_DIM)
        log_f = jax.nn.log_sigmoid(a_f.astype(jnp.float32) + fox_forget_b[i].astype(jnp.float32))
        o_a = forgetting_attention(a_qkv[:, :, 0], a_qkv[:, :, 1], a_qkv[:, :, 2], log_f)

        xb, gate_b, gate_c = jnp.split(b_in, 3, axis=-1)
        o_b = gate_b * causal_dwconv(gate_c * xb, shortconv_w[i])

        o_c = chunked_spatial_gating(jax.nn.gelu(c_in), sgu_norm_g[i], sgu_w[i], sgu_b[i])

        d_qkv = d_qkv.reshape(B, S, 3, DIL_HEADS, HEAD_DIM)
        qd = partial_rope(d_qkv[:, :, 0], positions)
        kd = partial_rope(d_qkv[:, :, 1], positions)
        vd = d_qkv[:, :, 2]
        outs, lses = [], []
        for g, (window, dil) in enumerate(DIL_PATTERNS):
            hs = slice(g * DIL_HEADS_PER_GROUP, (g + 1) * DIL_HEADS_PER_GROUP)
            o_g, l_g = dilated_window_attention(qd[:, :, hs], kd[:, :, hs], vd[:, :, hs], window, dil)
            outs.append(o_g)
            lses.append(l_g)
        wts = jax.nn.softmax(jnp.stack(lses, axis=0), axis=0)
        o_d = jnp.sum(wts[..., None] * jnp.stack(outs, axis=0), axis=0).astype(x.dtype).reshape(B, S, DIL_OUT)

        gates = jax.nn.sigmoid(gate_logits).reshape(B, S, N_BRANCH, D_MODEL)
        merged = (gates[:, :, 0] * (o_a @ w_br_fox[i]) + gates[:, :, 1] * (o_b @ w_br_conv[i])
                  + gates[:, :, 2] * (o_c @ w_br_sgu[i]) + gates[:, :, 3] * (o_d @ w_br_dil[i]))
        x = x + merged @ w_out[i]

        h = rms_norm(x, norm_ffn_g[i])
        up = causal_dwconv(h @ w_up[i], ffn_conv_w[i])
        up_gate, up_val = jnp.split(up, 2, axis=-1)
        x = x + (jax.nn.silu(up_gate) * up_val) @ w_down[i]

        ple_gate = jax.nn.sigmoid(rms_norm(x, norm_ple_g[i]) @ w_ple_gate[i])
        x = x + ple_gate * (p[i] @ w_ple_proj[i])
    return rms_norm(x, final_norm_g)
```

```python
import functools

import jax
import jax.numpy as jnp
from jax import lax
from jax.experimental import pallas as pl
from jax.experimental.pallas import tpu as pltpu

F32 = jnp.float32
BF16 = jnp.bfloat16

HEAD_DIM = 64
EPS = 1e-6
FOX_HEADS = 8
FOX_WIDTH = FOX_HEADS * HEAD_DIM
CONV_WIDTH = 512
CONV_TAPS = 3
SGU_GROUPS = 4
SGU_GROUP_DIM = 128
SGU_WIDTH = SGU_GROUPS * SGU_GROUP_DIM
SGU_CHUNK = 128
DIL_PATTERNS = ((128, 1), (512, 4), (2048, 16))
DIL_HEADS_PER_GROUP = 4
DIL_HEADS = DIL_HEADS_PER_GROUP * len(DIL_PATTERNS)
DIL_WIDTH = DIL_HEADS * HEAD_DIM
DIL_OUT = DIL_HEADS_PER_GROUP * HEAD_DIM
DIL_SPAN = 128
ROPE_THETA = 500000.0
ROPE_DIM = HEAD_DIM // 4
N_BRANCH = 4

LANES = 128
SUBLANES = 8
HEADS_PER_SLAB = 4
SLAB = HEADS_PER_SLAB * HEAD_DIM
VMEM_LIMIT = 56 * 1024 * 1024
NEG = -1e30
QK_SCALE = HEAD_DIM ** -0.5


def _params(n_axes):
    return pltpu.CompilerParams(dimension_semantics=("arbitrary",) * n_axes, vmem_limit_bytes=VMEM_LIMIT)


def _resident(shape):
    return pl.BlockSpec(shape, lambda *_: (0,) * len(shape), pipeline_mode=pl.Buffered(1))


def _tile(n, pref):
    t = min(n, pref)
    assert n % t == 0, (n, t)
    return t


def _rms(x, g):
    var = jnp.mean(x * x, axis=-1, keepdims=True)
    return x * lax.rsqrt(var + EPS) * g


def _dot(a, b):
    return jnp.dot(a, b, preferred_element_type=F32)


def _dot_nt(a, b):
    return lax.dot_general(a, b, (((1,), (1,)), ((), ())), preferred_element_type=F32)


def _shift_rows(z, prev_ref, row):
    p1 = prev_ref[SUBLANES - 1:SUBLANES, :]
    p2 = prev_ref[SUBLANES - 2:SUBLANES - 1, :]
    z1 = jnp.where(row == 0, p1, pltpu.roll(z, 1, 0))
    z2 = jnp.where(row == 0, p2, jnp.where(row == 1, p1, pltpu.roll(z, 2, 0)))
    return z1, z2


def _causal_conv3(z, w_ref, prev_ref, first_of_seq):
    tm = z.shape[0]

    @pl.when(first_of_seq)
    def _():
        prev_ref[...] = jnp.zeros_like(prev_ref)

    row = lax.broadcasted_iota(jnp.int32, z.shape, 0)
    z1, z2 = _shift_rows(z, prev_ref, row)
    y = w_ref[0:1, :] * z2 + w_ref[1:2, :] * z1 + w_ref[2:3, :] * z
    prev_ref[...] = z[tm - SUBLANES:, :]
    return y


def _rmsnorm_kernel(x_ref, g_ref, h_ref):
    h_ref[...] = _rms(x_ref[...], g_ref[...]).astype(h_ref.dtype)


def _rmsnorm(x, g, tm):
    T, D = x.shape
    return pl.pallas_call(
        _rmsnorm_kernel,
        out_shape=jax.ShapeDtypeStruct((T, D), BF16),
        grid=(T // tm,),
        in_specs=[pl.BlockSpec((tm, D), lambda i: (i, 0)), _resident((1, D))],
        out_specs=pl.BlockSpec((tm, D), lambda i: (i, 0)),
        compiler_params=_params(1),
        name="rmsnorm",
    )(x, g)


def _rope_table_kernel(pos_ref, inv_ref, c_ref, s1_ref, s2_ref):
    ang = pos_ref[...] * inv_ref[...]
    cos = jnp.cos(ang)
    sin = jnp.sin(ang)
    lane = lax.broadcasted_iota(jnp.int32, ang.shape, 1) % HEAD_DIM
    half = ROPE_DIM // 2
    c_ref[...] = jnp.where(lane < ROPE_DIM, cos, 1.0)
    s1_ref[...] = jnp.where((lane >= half) & (lane < ROPE_DIM), sin, 0.0)
    s2_ref[...] = jnp.where(lane < half, -sin, 0.0)


def _rope_tables(positions, tm):
    T = positions.size
    half = ROPE_DIM // 2
    inv = ROPE_THETA ** (-jnp.arange(half, dtype=F32) * (2.0 / ROPE_DIM))
    lane = jnp.arange(LANES) % HEAD_DIM
    inv_lane = jnp.where(lane < ROPE_DIM, inv[lane % half], 0.0).astype(F32)[None, :]
    posf = jnp.broadcast_to(positions.reshape(T, 1).astype(F32), (T, LANES))
    spec = pl.BlockSpec((tm, LANES), lambda i: (i, 0))
    return pl.pallas_call(
        _rope_table_kernel,
        out_shape=[jax.ShapeDtypeStruct((T, LANES), F32)] * 3,
        grid=(T // tm,),
        in_specs=[spec, _resident((1, LANES))],
        out_specs=[spec] * 3,
        compiler_params=_params(1),
        name="rope_tables",
    )(posf, inv_lane)


def _fox_proj_kernel(h_ref, w_ref, wf_ref, bf_ref, q_ref, k_ref, v_ref, f_ref, carry_ref, *, tiles_per_seq):
    i = pl.program_id(0)
    h = h_ref[...]
    W = FOX_WIDTH
    q_ref[...] = (_dot(h, w_ref[:, 0:W]) * QK_SCALE).astype(BF16)
    k_ref[...] = _dot(h, w_ref[:, W:2 * W]).astype(BF16)
    v_ref[...] = _dot(h, w_ref[:, 2 * W:3 * W]).astype(BF16)

    x = jax.nn.log_sigmoid(_dot(h, wf_ref[...]) + bf_ref[...])
    tm = x.shape[0]
    row = lax.broadcasted_iota(jnp.int32, x.shape, 0)
    s = 1
    while s < tm:
        x = x + jnp.where(row >= s, pltpu.roll(x, s, 0), 0.0)
        s *= 2

    @pl.when(i % tiles_per_seq == 0)
    def _():
        carry_ref[...] = jnp.zeros_like(carry_ref)

    x = x + carry_ref[0:1, :]
    f_ref[...] = x
    carry_ref[...] = jnp.broadcast_to(x[tm - 1:tm, :], carry_ref.shape)


def _fox_proj(h, w_qkv, w_f, b_f, tm, tiles_per_seq):
    T, D = h.shape
    W = FOX_WIDTH
    row = lambda n: pl.BlockSpec((tm, n), lambda i: (i, 0))
    return pl.pallas_call(
        functools.partial(_fox_proj_kernel, tiles_per_seq=tiles_per_seq),
        out_shape=[jax.ShapeDtypeStruct((T, W), BF16)] * 3 + [jax.ShapeDtypeStruct((T, LANES), F32)],
        grid=(T // tm,),
        in_specs=[row(D), _resident((D, 3 * W)), _resident((D, LANES)), _resident((1, LANES))],
        out_specs=[row(W), row(W), row(W), row(LANES)],
        scratch_shapes=[pltpu.VMEM((SUBLANES, LANES), F32)],
        compiler_params=_params(1),
        name="fox_proj",
    )(h, w_qkv, w_f, b_f)


def _conv_branch_kernel(h_ref, w_ref, cw_ref, o_ref, prev_ref, *, tiles_per_seq):
    i = pl.program_id(0)
    h = h_ref[...]
    W = CONV_WIDTH
    xb = _dot(h, w_ref[:, 0:W])
    gate_b = _dot(h, w_ref[:, W:2 * W])
    gate_c = _dot(h, w_ref[:, 2 * W:3 * W])
    y = _causal_conv3(gate_c * xb, cw_ref, prev_ref, i % tiles_per_seq == 0)
    o_ref[...] = (gate_b * y).astype(BF16)


def _conv_branch(h, w_b, conv_w, tm, tiles_per_seq):
    T, D = h.shape
    W = CONV_WIDTH
    return pl.pallas_call(
        functools.partial(_conv_branch_kernel, tiles_per_seq=tiles_per_seq),
        out_shape=jax.ShapeDtypeStruct((T, W), BF16),
        grid=(T // tm,),
        in_specs=[pl.BlockSpec((tm, D), lambda i: (i, 0)), _resident((D, 3 * W)), _resident((CONV_TAPS, W))],
        out_specs=pl.BlockSpec((tm, W), lambda i: (i, 0)),
        scratch_shapes=[pltpu.VMEM((SUBLANES, W), F32)],
        compiler_params=_params(1),
        name="conv_branch",
    )(h, w_b, conv_w)


def _sgu_branch_kernel(h_ref, w_ref, ng_ref, ws_ref, bias_ref, o_ref):
    h = h_ref[...]
    W = SGU_WIDTH
    C = SGU_CHUNK
    G = SGU_GROUP_DIM
    u = jax.nn.gelu(_dot(h, w_ref[:, 0:W]))
    v = _rms(jax.nn.gelu(_dot(h, w_ref[:, W:2 * W])), ng_ref[...]).astype(BF16)
    row = lax.broadcasted_iota(jnp.int32, (C, C), 0)
    col = lax.broadcasted_iota(jnp.int32, (C, C), 1)
    for g in range(SGU_GROUPS):
        w_tril = jnp.where(col <= row, ws_ref[g], 0.0).astype(BF16)
        cols = slice(g * G, (g + 1) * G)
        for c in range(h.shape[0] // C):
            rows = slice(c * C, (c + 1) * C)
            mixed = _dot(w_tril, v[rows, cols]) + bias_ref[:, cols]
            o_ref[rows, cols] = (u[rows, cols] * mixed).astype(BF16)


def _sgu_branch(h, w_c, norm_g, w_s, bias_full, tm):
    T, D = h.shape
    W = SGU_WIDTH
    return pl.pallas_call(
        _sgu_branch_kernel,
        out_shape=jax.ShapeDtypeStruct((T, W), BF16),
        grid=(T // tm,),
        in_specs=[pl.BlockSpec((tm, D), lambda i: (i, 0)), _resident((D, 2 * W)), _resident((1, W)),
                  _resident((SGU_GROUPS, SGU_CHUNK, SGU_CHUNK)), _resident((SGU_CHUNK, W))],
        out_specs=pl.BlockSpec((tm, W), lambda i: (i, 0)),
        compiler_params=_params(1),
        name="sgu_branch",
    )(h, w_c, norm_g, w_s, bias_full)


def _dil_proj_kernel(h_ref, w_ref, c_ref, s1_ref, s2_ref, q_ref, k_ref, v_ref):
    h = h_ref[...]
    W = DIL_WIDTH
    cos = c_ref[...]
    s1 = s1_ref[...]
    s2 = s2_ref[...]
    shift = ROPE_DIM // 2
    for out_ref, base, scale in ((q_ref, 0, QK_SCALE), (k_ref, W, None)):
        for j in range(W // LANES):
            x = _dot(h, w_ref[:, base + j * LANES: base + (j + 1) * LANES])
            r = x * cos + pltpu.roll(x, shift, 1) * s1 + pltpu.roll(x, LANES - shift, 1) * s2
            if scale is not None:
                r = r * scale
            out_ref[:, j * LANES:(j + 1) * LANES] = r.astype(BF16)
    v_ref[...] = _dot(h, w_ref[:, 2 * W:3 * W]).astype(BF16)


def _dil_proj(h, w_d, tables, tm):
    T, D = h.shape
    W = DIL_WIDTH
    row = lambda n: pl.BlockSpec((tm, n), lambda i: (i, 0))
    return pl.pallas_call(
        _dil_proj_kernel,
        out_shape=[jax.ShapeDtypeStruct((T, W), BF16)] * 3,
        grid=(T // tm,),
        in_specs=[row(D), _resident((D, 3 * W)), row(LANES), row(LANES), row(LANES)],
        out_specs=[row(W)] * 3,
        compiler_params=_params(1),
        name="dil_proj",
    )(h, w_d, *tables)


def _slab_head_masks(n_rows):
    lane = lax.broadcasted_iota(jnp.int32, (n_rows, SLAB), 1)
    return [(lane >= h * HEAD_DIM) & (lane < (h + 1) * HEAD_DIM) for h in range(HEADS_PER_SLAB)]


def _fox_attn_kernel(q_ref, k_ref, v_ref, f_ref, o_ref, *, tq):
    i = pl.program_id(2)
    q0 = pl.multiple_of(i * tq, tq)
    q = q_ref[0]
    masks = _slab_head_masks(tq)
    zero = jnp.zeros_like(q)
    qh = [jnp.where(masks[h], q, zero) for h in range(HEADS_PER_SLAB)]
    f0 = [f_ref[0, h:h + 1, pl.ds(q0, LANES)][:, 0:1] for h in range(HEADS_PER_SLAB)]
    rowi = lax.broadcasted_iota(jnp.int32, (tq, tq), 0)
    coli = lax.broadcasted_iota(jnp.int32, (tq, tq), 1)

    def block(k0, carry, diagonal):
        ms, ls, acc = carry
        k = k_ref[0, pl.ds(k0, tq), :]
        v = v_ref[0, pl.ds(k0, tq), :]
        new_ms, new_ls = [], []
        for h in range(HEADS_PER_SLAB):
            s = _dot_nt(qh[h], k) + (f0[h] - f_ref[0, h:h + 1, pl.ds(k0, tq)])
            if diagonal:
                s = jnp.where(coli <= rowi, s, NEG)
            m_new = jnp.maximum(ms[h], jnp.max(s, axis=-1, keepdims=True))
            alpha = jnp.exp(ms[h] - m_new)
            p = jnp.exp(s - m_new)
            new_ls.append(alpha * ls[h] + jnp.sum(p, axis=-1, keepdims=True))
            new_ms.append(m_new)
            acc = jnp.where(masks[h], acc * alpha + _dot(p.astype(BF16), v), acc)
        return tuple(new_ms), tuple(new_ls), acc

    init = (tuple(jnp.full((tq, 1), NEG, F32) for _ in range(HEADS_PER_SLAB)),
            tuple(jnp.zeros((tq, 1), F32) for _ in range(HEADS_PER_SLAB)),
            jnp.zeros((tq, SLAB), F32))
    carry = lax.fori_loop(0, i, lambda kb, c: block(pl.multiple_of(kb * tq, tq), c, False), init)
    ms, ls, acc = block(q0, carry, True)
    denom = jnp.zeros((tq, SLAB), F32)
    for h in range(HEADS_PER_SLAB):
        denom = jnp.where(masks[h], ls[h], denom)
    o_ref[0] = (acc / denom).astype(BF16)


def _fox_attention(q, k, v, f_slabs, B, S, tq):
    n_slabs = FOX_WIDTH // SLAB
    kv_spec = pl.BlockSpec((1, S, SLAB), lambda b, g, i: (b, 0, g))
    return pl.pallas_call(
        functools.partial(_fox_attn_kernel, tq=tq),
        out_shape=jax.ShapeDtypeStruct((B, S, FOX_WIDTH), BF16),
        grid=(B, n_slabs, S // tq),
        in_specs=[pl.BlockSpec((1, tq, SLAB), lambda b, g, i: (b, i, g)), kv_spec, kv_spec,
                  pl.BlockSpec((1, HEADS_PER_SLAB, S), lambda b, g, i: (b * n_slabs + g, 0, 0))],
        out_specs=pl.BlockSpec((1, tq, SLAB), lambda b, g, i: (b, i, g)),
        compiler_params=_params(3),
        name="fox_attention",
    )(q, k, v, f_slabs)


def _dil_attn_kernel(q_ref, k_ref, v_ref, o_ref, lse_ref, *, n_blocks):
    P = DIL_SPAN
    masks = _slab_head_masks(P)

    def block(q_rows, k_rows, n_keys, valid):
        q = q_ref[0, q_rows, :]
        k = k_ref[0, k_rows, :]
        v = v_ref[0, k_rows, :]
        zero = jnp.zeros_like(q)
        o = jnp.zeros((P, SLAB), F32)
        lse = jnp.zeros((P, SLAB), F32)
        for h in range(HEADS_PER_SLAB):
            s = jnp.where(valid, _dot_nt(jnp.where(masks[h], q, zero), k), NEG)
            m = jnp.max(s, axis=-1, keepdims=True)
            p = jnp.exp(s - m)
            l = jnp.sum(p, axis=-1, keepdims=True)
            o = jnp.where(masks[h], _dot(p.astype(BF16), v) / l, o)
            lse = jnp.where(masks[h], m + jnp.log(l), lse)
        o_ref[0, q_rows, :] = o
        lse_ref[0, q_rows, :] = lse

    qi = lax.broadcasted_iota(jnp.int32, (P, P), 0)
    ki = lax.broadcasted_iota(jnp.int32, (P, P), 1)
    block(pl.ds(0, P), pl.ds(0, P), P, ki <= qi)

    qi2 = lax.broadcasted_iota(jnp.int32, (P, 2 * P), 0)
    ki2 = lax.broadcasted_iota(jnp.int32, (P, 2 * P), 1)
    valid2 = (ki2 >= qi2) & (ki2 <= qi2 + P)

    def body(n, _):
        start = pl.multiple_of(n * P, P)
        block(pl.ds(start, P), pl.ds(start - P, 2 * P), 2 * P, valid2)
        return 0

    lax.fori_loop(1, n_blocks, body, 0)


def _dil_attention(q, k, v, B, S, group, dilation):
    L = S // dilation
    n_groups = len(DIL_PATTERNS)
    view = lambda t: t.reshape(B, L, dilation * DIL_WIDTH)
    in_spec = pl.BlockSpec((1, L, SLAB), lambda b, r: (b, 0, r * n_groups + group))
    out_spec = pl.BlockSpec((1, L, SLAB), lambda b, r: (b, 0, r))
    o, lse = pl.pallas_call(
        functools.partial(_dil_attn_kernel, n_blocks=L // DIL_SPAN),
        out_shape=[jax.ShapeDtypeStruct((B, L, dilation * SLAB), F32)] * 2,
        grid=(B, dilation),
        in_specs=[in_spec] * 3,
        out_specs=[out_spec] * 2,
        compiler_params=_params(2),
        name=f"dil_attention_d{dilation}",
    )(view(q), view(k), view(v))
    return o.reshape(B * S, SLAB), lse.reshape(B * S, SLAB)


def _dil_merge_kernel(o0, o1, o2, l0, l1, l2, out_ref):
    ls = [l0[...], l1[...], l2[...]]
    m = jnp.maximum(jnp.maximum(ls[0], ls[1]), ls[2])
    es = [jnp.exp(l - m) for l in ls]
    num = es[0] * o0[...] + es[1] * o1[...] + es[2] * o2[...]
    out_ref[...] = (num / (es[0] + es[1] + es[2])).astype(BF16)


def _dil_merge(outs, lses, tm):
    T = outs[0].shape[0]
    spec = pl.BlockSpec((tm, SLAB), lambda i: (i, 0))
    return pl.pallas_call(
        _dil_merge_kernel,
        out_shape=jax.ShapeDtypeStruct((T, SLAB), BF16),
        grid=(T // tm,),
        in_specs=[spec] * 6,
        out_specs=spec,
        compiler_params=_params(1),
        name="dil_merge",
    )(*outs, *lses)


def _gated_merge_kernel(h_ref, oa_ref, ob_ref, oc_ref, od_ref, g0, g1, g2, g3, wa, wb, wc, wd, out_ref):
    h = h_ref[...]
    acc = None
    for o_ref, g_ref, w_ref in ((oa_ref, g0, wa), (ob_ref, g1, wb), (oc_ref, g2, wc), (od_ref, g3, wd)):
        term = jax.nn.sigmoid(_dot(h, g_ref[...])) * _dot(o_ref[...], w_ref[...])
        acc = term if acc is None else acc + term
    out_ref[...] = acc.astype(BF16)


def _gated_merge(h, branches, w_gate, w_branches, tm, tn):
    T, D = h.shape
    n_col = D // tn
    row = lambda n: pl.BlockSpec((tm, n), lambda j, i: (i, 0))
    gate_specs = [pl.BlockSpec((D, tn), functools.partial(lambda j, i, br: (0, br * n_col + j), br=br))
                  for br in range(N_BRANCH)]
    w_specs = [pl.BlockSpec((w.shape[0], tn), lambda j, i: (0, j)) for w in w_branches]
    return pl.pallas_call(
        _gated_merge_kernel,
        out_shape=jax.ShapeDtypeStruct((T, D), BF16),
        grid=(n_col, T // tm),
        in_specs=[row(D)] + [row(o.shape[1]) for o in branches] + gate_specs + w_specs,
        out_specs=pl.BlockSpec((tm, tn), lambda j, i: (i, j)),
        compiler_params=_params(2),
        name="gated_merge",
    )(h, *branches, w_gate, w_gate, w_gate, w_gate, *w_branches)


def _residual_proj_kernel(a_ref, w_ref, x_ref, g_ref, xo_ref, h_ref):
    x = x_ref[...] + _dot(a_ref[...], w_ref[...])
    xo_ref[...] = x
    h_ref[...] = _rms(x, g_ref[...]).astype(h_ref.dtype)


def _residual_proj(a, w, x, g_next, tm):
    T, D = x.shape
    K = a.shape[1]
    row = lambda n: pl.BlockSpec((tm, n), lambda i: (i, 0))
    return pl.pallas_call(
        _residual_proj_kernel,
        out_shape=[jax.ShapeDtypeStruct((T, D), F32), jax.ShapeDtypeStruct((T, D), BF16)],
        grid=(T // tm,),
        in_specs=[row(K), _resident((K, D)), row(D), _resident((1, D))],
        out_specs=[row(D), row(D)],
        compiler_params=_params(1),
        name=f"residual_proj_k{K}",
    )(a, w, x, g_next)


def _ffn_up_kernel(h_ref, wg_ref, wv_ref, cg_ref, cv_ref, act_ref, prev_g, prev_v, *, tiles_per_seq):
    i = pl.program_id(1)
    first = i % tiles_per_seq == 0
    h = h_ref[...]
    up_gate = _causal_conv3(_dot(h, wg_ref[...]), cg_ref, prev_g, first)
    up_val = _causal_conv3(_dot(h, wv_ref[...]), cv_ref, prev_v, first)
    act_ref[...] = (jax.nn.silu(up_gate) * up_val).astype(BF16)


def _ffn_up(h, w_up, conv_w, tm, tn, tiles_per_seq):
    T, D = h.shape
    d_ff = w_up.shape[1] // 2
    n_col = d_ff // tn
    return pl.pallas_call(
        functools.partial(_ffn_up_kernel, tiles_per_seq=tiles_per_seq),
        out_shape=jax.ShapeDtypeStruct((T, d_ff), BF16),
        grid=(n_col, T // tm),
        in_specs=[pl.BlockSpec((tm, D), lambda j, i: (i, 0)),
                  pl.BlockSpec((D, tn), lambda j, i: (0, j)),
                  pl.BlockSpec((D, tn), lambda j, i: (0, n_col + j)),
                  pl.BlockSpec((CONV_TAPS, tn), lambda j, i: (0, j)),
                  pl.BlockSpec((CONV_TAPS, tn), lambda j, i: (0, n_col + j))],
        out_specs=pl.BlockSpec((tm, tn), lambda j, i: (i, j)),
        scratch_shapes=[pltpu.VMEM((SUBLANES, tn), F32)] * 2,
        compiler_params=_params(2),
        name="ffn_up",
    )(h, w_up, w_up, conv_w, conv_w)


def _ple_kernel(h_ref, wg_ref, p_ref, wp_ref, x_ref, g_ref, *out_refs, final):
    gate = jax.nn.sigmoid(_dot(h_ref[...], wg_ref[...]))
    x = x_ref[...] + gate * _dot(p_ref[...].astype(BF16), wp_ref[...])
    if not final:
        out_refs[0][...] = x
    out_refs[-1][...] = _rms(x, g_ref[...]).astype(out_refs[-1].dtype)


def _ple(h, w_gate, p, w_proj, x, g_next, tm, final):
    T, D = x.shape
    P = p.shape[1]
    row = lambda n: pl.BlockSpec((tm, n), lambda i: (i, 0))
    if final:
        out_shape = [jax.ShapeDtypeStruct((T, D), F32)]
    else:
        out_shape = [jax.ShapeDtypeStruct((T, D), F32), jax.ShapeDtypeStruct((T, D), BF16)]
    return pl.pallas_call(
        functools.partial(_ple_kernel, final=final),
        out_shape=out_shape,
        grid=(T // tm,),
        in_specs=[row(D), _resident((D, D)), row(P), _resident((P, D)), row(D), _resident((1, D))],
        out_specs=[row(D)] * len(out_shape),
        compiler_params=_params(1),
        name="ple_final" if final else "ple",
    )(h, w_gate, p, w_proj, x, g_next)


def kernel(x, p, positions, norm_mix_g, w_in, fox_forget_b, shortconv_w, sgu_norm_g, sgu_w, sgu_b, w_br_fox, w_br_conv, w_br_sgu, w_br_dil, w_out, norm_ffn_g, w_up, ffn_conv_w, w_down, norm_ple_g, w_ple_gate, w_ple_proj, final_norm_g):
    B, S, D = x.shape
    depth = w_in.shape[0]
    T = B * S
    d_ff = w_down.shape[1]

    tm = _tile(S, 512)
    tm_wide = _tile(S, 256)
    tq = _tile(S, 512)
    tn_merge = _tile(D, 512)
    tn_ffn = _tile(d_ff, 512)
    tiles_per_seq = S // tm

    sizes = (3 * FOX_WIDTH, FOX_HEADS, 3 * CONV_WIDTH, 2 * SGU_WIDTH, 3 * DIL_WIDTH, N_BRANCH * D)
    offs = [0]
    for n in sizes:
        offs.append(offs[-1] + n)

    bf = lambda w: w.astype(BF16)
    w_fox = bf(w_in[:, :, offs[0]:offs[1]])
    w_f = jnp.pad(bf(w_in[:, :, offs[1]:offs[2]]), ((0, 0), (0, 0), (0, LANES - FOX_HEADS)))
    b_f = jnp.pad(fox_forget_b.astype(F32), ((0, 0), (0, LANES - FOX_HEADS)))[:, None, :]
    w_conv = bf(w_in[:, :, offs[2]:offs[3]])
    w_sgu = bf(w_in[:, :, offs[3]:offs[4]])
    w_dil = bf(w_in[:, :, offs[4]:offs[5]])
    w_gate = bf(w_in[:, :, offs[5]:offs[6]])
    w_brs = [bf(w_br_fox), bf(w_br_conv), bf(w_br_sgu), bf(w_br_dil)]
    w_out_b, w_up_b, w_down_b = bf(w_out), bf(w_up), bf(w_down)
    w_pg_b, w_pp_b = bf(w_ple_gate), bf(w_ple_proj)
    sgu_bias = jnp.repeat(jnp.swapaxes(sgu_b, 1, 2), SGU_GROUP_DIM, axis=2)
    row_vec = lambda g: g[None, :]

    tables = _rope_tables(positions, tm)
    xf = x.reshape(T, D)
    h = _rmsnorm(xf, row_vec(norm_mix_g[0]), tm)
    n_slabs = FOX_WIDTH // SLAB
    out = None
    for i in range(depth):
        qa, ka, va, f_pad = _fox_proj(h, w_fox[i], w_f[i], b_f[i], tm, tiles_per_seq)
        f_slabs = f_pad[:, :FOX_HEADS].reshape(B, S, FOX_HEADS).transpose(0, 2, 1)
        f_slabs = f_slabs.reshape(B * n_slabs, HEADS_PER_SLAB, S)
        o_a = _fox_attention(qa.reshape(B, S, FOX_WIDTH), ka.reshape(B, S, FOX_WIDTH),
                             va.reshape(B, S, FOX_WIDTH), f_slabs, B, S, tq).reshape(T, FOX_WIDTH)
        o_b = _conv_branch(h, w_conv[i], shortconv_w[i], tm, tiles_per_seq)
        o_c = _sgu_branch(h, w_sgu[i], row_vec(sgu_norm_g[i]), sgu_w[i], sgu_bias[i], tm)
        qd, kd, vd = _dil_proj(h, w_dil[i], tables, tm)
        outs, lses = [], []
        for g, (window, dil) in enumerate(DIL_PATTERNS):
            assert window // dil == DIL_SPAN
            o_g, l_g = _dil_attention(qd, kd, vd, B, S, g, dil)
            outs.append(o_g)
            lses.append(l_g)
        o_d = _dil_merge(outs, lses, tm)

        merged = _gated_merge(h, (o_a, o_b, o_c, o_d), w_gate[i], [w[i] for w in w_brs], tm, tn_merge)
        xf, h = _residual_proj(merged, w_out_b[i], xf, row_vec(norm_ffn_g[i]), tm_wide)

        act = _ffn_up(h, w_up_b[i], ffn_conv_w[i], tm, tn_ffn, tiles_per_seq)
        xf, h = _residual_proj(act, w_down_b[i], xf, row_vec(norm_ple_g[i]), tm_wide)

        final = i == depth - 1
        g_next = final_norm_g if final else norm_mix_g[i + 1]
        res = _ple(h, w_pg_b[i], p[i].reshape(T, p.shape[-1]), w_pp_b[i], xf, row_vec(g_next), tm_wide, final)
        if final:
            out = res[0]
        else:
            xf, h = res
    return out.reshape(B, S, D)
```

```python
import functools

import jax
import jax.numpy as jnp
from jax import lax
from jax.experimental import pallas as pl
from jax.experimental.pallas import tpu as pltpu

F32 = jnp.float32
BF16 = jnp.bfloat16

HEAD_DIM = 64
EPS = 1e-6
FOX_HEADS = 8
FOX_WIDTH = FOX_HEADS * HEAD_DIM
CONV_WIDTH = 512
CONV_TAPS = 3
SGU_GROUPS = 4
SGU_GROUP_DIM = 128
SGU_WIDTH = SGU_GROUPS * SGU_GROUP_DIM
SGU_CHUNK = 128
DIL_PATTERNS = ((128, 1), (512, 4), (2048, 16))
DIL_HEADS_PER_GROUP = 4
DIL_HEADS = DIL_HEADS_PER_GROUP * len(DIL_PATTERNS)
DIL_WIDTH = DIL_HEADS * HEAD_DIM
DIL_OUT = DIL_HEADS_PER_GROUP * HEAD_DIM
DIL_SPAN = 128
ROPE_THETA = 500000.0
ROPE_DIM = HEAD_DIM // 4
N_BRANCH = 4

LANES = 128
SUBLANES = 8
BF16_ROWS = 16
HEADS_PER_SLAB = 4
SLAB = HEADS_PER_SLAB * HEAD_DIM
VMEM_LIMIT = 56 * 1024 * 1024
NEG = -1e30
QK_SCALE = HEAD_DIM ** -0.5


def _params(n_axes):
    return pltpu.CompilerParams(dimension_semantics=("arbitrary",) * n_axes, vmem_limit_bytes=VMEM_LIMIT)


def _resident(shape, index=None):
    index = (0,) * len(shape) if index is None else index
    return pl.BlockSpec(shape, lambda *_: index, pipeline_mode=pl.Buffered(1))


def _layer_weight(layer, rows, cols):
    return _resident((None, rows, cols), (layer, 0, 0))


def _tile(n, pref):
    t = min(n, pref)
    assert n % t == 0, (n, t)
    return t


def _rms(x, g):
    var = jnp.mean(x * x, axis=-1, keepdims=True)
    return x * lax.rsqrt(var + EPS) * g


def _dot(a, b):
    return jnp.dot(a, b, preferred_element_type=F32)


def _dot_nt(a, b):
    return lax.dot_general(a, b, (((1,), (1,)), ((), ())), preferred_element_type=F32)


def _causal_conv3(z, w, prev):
    y = w[0:1] * pltpu.roll(z, 2, 0) + w[1:2] * pltpu.roll(z, 1, 0) + w[2:3] * z
    zh = z[0:BF16_ROWS]
    row = lax.broadcasted_iota(jnp.int32, zh.shape, 0)
    p1 = prev[SUBLANES - 1:SUBLANES]
    p2 = prev[SUBLANES - 2:SUBLANES - 1]
    z1 = jnp.where(row == 0, p1, pltpu.roll(zh, 1, 0))
    z2 = jnp.where(row == 0, p2, jnp.where(row == 1, p1, pltpu.roll(zh, 2, 0)))
    yh = w[0:1] * z2 + w[1:2] * z1 + w[2:3] * zh
    return jnp.concatenate([yh, y[BF16_ROWS:]], axis=0)


def _rmsnorm_kernel(x_ref, g_ref, h_ref):
    h_ref[...] = _rms(x_ref[...], g_ref[...]).astype(h_ref.dtype)


def _rmsnorm(x, g, tm):
    T, D = x.shape
    return pl.pallas_call(
        _rmsnorm_kernel,
        out_shape=jax.ShapeDtypeStruct((T, D), BF16),
        grid=(T // tm,),
        in_specs=[pl.BlockSpec((tm, D), lambda i: (i, 0)), _resident((1, D))],
        out_specs=pl.BlockSpec((tm, D), lambda i: (i, 0)),
        compiler_params=_params(1),
        name="rmsnorm",
    )(x, g)


def _rope_table_kernel(pos_ref, inv_ref, c_ref, s1_ref, s2_ref):
    ang = pos_ref[...] * inv_ref[...]
    cos = jnp.cos(ang)
    sin = jnp.sin(ang)
    lane = lax.broadcasted_iota(jnp.int32, ang.shape, 1) % HEAD_DIM
    half = ROPE_DIM // 2
    c_ref[...] = jnp.where(lane < ROPE_DIM, cos, 1.0)
    s1_ref[...] = jnp.where((lane >= half) & (lane < ROPE_DIM), sin, 0.0)
    s2_ref[...] = jnp.where(lane < half, -sin, 0.0)


def _rope_tables(positions, tm):
    T = positions.size
    half = ROPE_DIM // 2
    inv = ROPE_THETA ** (-jnp.arange(half, dtype=F32) * (2.0 / ROPE_DIM))
    lane = jnp.arange(LANES) % HEAD_DIM
    inv_lane = jnp.where(lane < ROPE_DIM, inv[lane % half], 0.0).astype(F32)[None, :]
    posf = jnp.broadcast_to(positions.reshape(T, 1).astype(F32), (T, LANES))
    spec = pl.BlockSpec((tm, LANES), lambda i: (i, 0))
    return pl.pallas_call(
        _rope_table_kernel,
        out_shape=[jax.ShapeDtypeStruct((T, LANES), F32)] * 3,
        grid=(T // tm,),
        in_specs=[spec, _resident((1, LANES))],
        out_specs=[spec] * 3,
        compiler_params=_params(1),
        name="rope_tables",
    )(posf, inv_lane)


def _fox_proj_kernel(h_ref, w_ref, wf_ref, bf_ref, q_ref, k_ref, v_ref, f_ref, carry_ref, *, tiles_per_seq):
    @pl.when(pl.program_id(0) % tiles_per_seq == 0)
    def _():
        carry_ref[...] = jnp.zeros_like(carry_ref)

    h = h_ref[...]
    W = FOX_WIDTH
    q_ref[...] = (_dot(h, w_ref[:, 0:W]) * QK_SCALE).astype(BF16)
    k_ref[...] = _dot(h, w_ref[:, W:2 * W]).astype(BF16)
    v_ref[...] = _dot(h, w_ref[:, 2 * W:3 * W]).astype(BF16)

    x = jax.nn.log_sigmoid(_dot(h, wf_ref[...]) + bf_ref[...])
    tm = x.shape[0]
    row = lax.broadcasted_iota(jnp.int32, x.shape, 0)
    s = 1
    while s < tm:
        x = x + jnp.where(row >= s, pltpu.roll(x, s, 0), 0.0)
        s *= 2
    x = x + carry_ref[0:1, :]
    f_ref[...] = x
    carry_ref[...] = jnp.broadcast_to(x[tm - 1:tm, :], carry_ref.shape)


def _fox_proj(h, w_qkv, w_f, b_f, layer, tm, tiles_per_seq):
    T, D = h.shape
    W = FOX_WIDTH
    row = lambda n: pl.BlockSpec((tm, n), lambda i: (i, 0))
    return pl.pallas_call(
        functools.partial(_fox_proj_kernel, tiles_per_seq=tiles_per_seq),
        out_shape=[jax.ShapeDtypeStruct((T, W), BF16)] * 3 + [jax.ShapeDtypeStruct((T, LANES), F32)],
        grid=(T // tm,),
        in_specs=[row(D), _layer_weight(layer, D, 3 * W), _layer_weight(layer, D, LANES),
                  _layer_weight(layer, 1, LANES)],
        out_specs=[row(W), row(W), row(W), row(LANES)],
        scratch_shapes=[pltpu.VMEM((SUBLANES, LANES), F32)],
        compiler_params=_params(1),
        name="fox_proj",
    )(h, w_qkv, w_f, b_f)


def _conv_branch_kernel(h_ref, w_ref, cw_ref, o_ref, prev_ref, *, tiles_per_seq):
    @pl.when(pl.program_id(0) % tiles_per_seq == 0)
    def _():
        prev_ref[...] = jnp.zeros_like(prev_ref)

    h = h_ref[...]
    W = CONV_WIDTH
    tm = h.shape[0]
    xb = _dot(h, w_ref[:, 0:W])
    gate_b = _dot(h, w_ref[:, W:2 * W])
    gate_c = _dot(h, w_ref[:, 2 * W:3 * W])
    z = gate_c * xb
    y = _causal_conv3(z, cw_ref[...], prev_ref[...])
    prev_ref[...] = z[tm - SUBLANES:, :]
    o_ref[...] = (gate_b * y).astype(BF16)


def _conv_branch(h, w_b, conv_w, layer, tm, tiles_per_seq):
    T, D = h.shape
    W = CONV_WIDTH
    return pl.pallas_call(
        functools.partial(_conv_branch_kernel, tiles_per_seq=tiles_per_seq),
        out_shape=jax.ShapeDtypeStruct((T, W), BF16),
        grid=(T // tm,),
        in_specs=[pl.BlockSpec((tm, D), lambda i: (i, 0)), _layer_weight(layer, D, 3 * W),
                  _layer_weight(layer, CONV_TAPS, W)],
        out_specs=pl.BlockSpec((tm, W), lambda i: (i, 0)),
        scratch_shapes=[pltpu.VMEM((SUBLANES, W), F32)],
        compiler_params=_params(1),
        name="conv_branch",
    )(h, w_b, conv_w)


def _sgu_branch_kernel(h_ref, w_ref, ng_ref, ws_ref, bias_ref, o_ref):
    h = h_ref[...]
    W = SGU_WIDTH
    C = SGU_CHUNK
    G = SGU_GROUP_DIM
    u = jax.nn.gelu(_dot(h, w_ref[:, 0:W]))
    v = _rms(jax.nn.gelu(_dot(h, w_ref[:, W:2 * W])), ng_ref[...]).astype(BF16)
    row = lax.broadcasted_iota(jnp.int32, (C, C), 0)
    col = lax.broadcasted_iota(jnp.int32, (C, C), 1)
    for g in range(SGU_GROUPS):
        w_tril = jnp.where(col <= row, ws_ref[g], 0.0).astype(BF16)
        cols = slice(g * G, (g + 1) * G)
        for c in range(h.shape[0] // C):
            rows = slice(c * C, (c + 1) * C)
            mixed = _dot(w_tril, v[rows, cols]) + bias_ref[:, cols]
            o_ref[rows, cols] = (u[rows, cols] * mixed).astype(BF16)


def _sgu_branch(h, w_c, norm_g, w_s, bias_full, layer, tm):
    T, D = h.shape
    W = SGU_WIDTH
    return pl.pallas_call(
        _sgu_branch_kernel,
        out_shape=jax.ShapeDtypeStruct((T, W), BF16),
        grid=(T // tm,),
        in_specs=[pl.BlockSpec((tm, D), lambda i: (i, 0)), _layer_weight(layer, D, 2 * W),
                  _layer_weight(layer, 1, W),
                  _resident((None, SGU_GROUPS, SGU_CHUNK, SGU_CHUNK), (layer, 0, 0, 0)),
                  _layer_weight(layer, SGU_CHUNK, W)],
        out_specs=pl.BlockSpec((tm, W), lambda i: (i, 0)),
        compiler_params=_params(1),
        name="sgu_branch",
    )(h, w_c, norm_g, w_s, bias_full)


def _dil_proj_kernel(h_ref, w_ref, c_ref, s1_ref, s2_ref, *refs):
    n_groups = len(DIL_PATTERNS)
    out_refs = refs[:3 * n_groups]
    stage_ref = refs[3 * n_groups]
    h = h_ref[...]
    tm = h.shape[0]
    cos = c_ref[...]
    s1 = s1_ref[...]
    s2 = s2_ref[...]
    shift = ROPE_DIM // 2
    slot = 0
    for which, rope, scale in ((0, True, QK_SCALE), (1, True, None), (2, False, None)):
        for g, (_, d) in enumerate(DIL_PATTERNS):
            col = which * DIL_WIDTH + g * SLAB
            x = _dot(h, w_ref[:, col:col + SLAB])
            parts = []
            for j in range(SLAB // LANES):
                xj = x[:, j * LANES:(j + 1) * LANES]
                if rope:
                    xj = xj * cos + pltpu.roll(xj, shift, 1) * s1 + pltpu.roll(xj, LANES - shift, 1) * s2
                if scale is not None:
                    xj = xj * scale
                parts.append(xj)
            out = out_refs[g * 3 + which]
            for j, xj in enumerate(parts):
                lanes = slice(j * LANES, (j + 1) * LANES)
                if d == 1:
                    out[0, 0, :, lanes] = xj.astype(BF16)
                else:
                    stage_ref[slot] = xj
                    for r in range(d):
                        out[0, r, :, lanes] = stage_ref[slot, pl.ds(r, tm // d, stride=d), :].astype(BF16)
                    slot += 1


def _dil_proj(h, w_d, tables, layer, B, S, tm):
    T, D = h.shape
    tiles_per_seq = S // tm
    row = lambda n: pl.BlockSpec((tm, n), lambda i: (i, 0))
    out_shape, out_specs = [], []
    n_strided = 0
    for _, d in DIL_PATTERNS:
        assert (tm // d) % BF16_ROWS == 0
        n_strided += 3 * (SLAB // LANES) * (d > 1)
        for _ in range(3):
            out_shape.append(jax.ShapeDtypeStruct((B, d, S // d, SLAB), BF16))
            out_specs.append(pl.BlockSpec((1, d, tm // d, SLAB),
                                          lambda i: (i // tiles_per_seq, 0, i % tiles_per_seq, 0)))
    return pl.pallas_call(
        _dil_proj_kernel,
        out_shape=out_shape,
        grid=(T // tm,),
        in_specs=[row(D), _layer_weight(layer, D, 3 * DIL_WIDTH), row(LANES), row(LANES), row(LANES)],
        out_specs=out_specs,
        scratch_shapes=[pltpu.VMEM((n_strided, tm, LANES), F32)],
        compiler_params=_params(1),
        name="dil_proj",
    )(h, w_d, *tables)


def _slab_head_masks(n_rows):
    lane = lax.broadcasted_iota(jnp.int32, (n_rows, SLAB), 1)
    return [(lane >= h * HEAD_DIM) & (lane < (h + 1) * HEAD_DIM) for h in range(HEADS_PER_SLAB)]


def _fox_attn_kernel(q_ref, k_ref, v_ref, f_ref, o_ref, *, tq):
    i = pl.program_id(2)
    q0 = pl.multiple_of(i * tq, tq)
    q = q_ref[0]
    masks = _slab_head_masks(tq)
    zero = jnp.zeros_like(q)
    qh = [jnp.where(masks[h], q, zero) for h in range(HEADS_PER_SLAB)]
    f0 = [f_ref[0, h:h + 1, pl.ds(q0, LANES)][:, 0:1] for h in range(HEADS_PER_SLAB)]
    rowi = lax.broadcasted_iota(jnp.int32, (tq, tq), 0)
    coli = lax.broadcasted_iota(jnp.int32, (tq, tq), 1)

    def block(k0, carry, diagonal):
        ms, ls, acc = carry
        k = k_ref[0, pl.ds(k0, tq), :]
        v = v_ref[0, pl.ds(k0, tq), :]
        new_ms, new_ls = [], []
        for h in range(HEADS_PER_SLAB):
            s = _dot_nt(qh[h], k) + (f0[h] - f_ref[0, h:h + 1, pl.ds(k0, tq)])
            if diagonal:
                s = jnp.where(coli <= rowi, s, NEG)
            m_new = jnp.maximum(ms[h], jnp.max(s, axis=-1, keepdims=True))
            alpha = jnp.exp(ms[h] - m_new)
            p = jnp.exp(s - m_new)
            new_ls.append(alpha * ls[h] + jnp.sum(p, axis=-1, keepdims=True))
            new_ms.append(m_new)
            acc = jnp.where(masks[h], acc * alpha + _dot(p.astype(BF16), v), acc)
        return tuple(new_ms), tuple(new_ls), acc

    init = (tuple(jnp.full((tq, 1), NEG, F32) for _ in range(HEADS_PER_SLAB)),
            tuple(jnp.zeros((tq, 1), F32) for _ in range(HEADS_PER_SLAB)),
            jnp.zeros((tq, SLAB), F32))
    carry = lax.fori_loop(0, i, lambda kb, c: block(pl.multiple_of(kb * tq, tq), c, False), init)
    ms, ls, acc = block(q0, carry, True)
    denom = jnp.zeros((tq, SLAB), F32)
    for h in range(HEADS_PER_SLAB):
        denom = jnp.where(masks[h], ls[h], denom)
    o_ref[0] = (acc / denom).astype(BF16)


def _fox_attention(q, k, v, f_slabs, B, S, tq):
    n_slabs = FOX_WIDTH // SLAB
    kv_spec = pl.BlockSpec((1, S, SLAB), lambda b, g, i: (b, 0, g))
    return pl.pallas_call(
        functools.partial(_fox_attn_kernel, tq=tq),
        out_shape=jax.ShapeDtypeStruct((B, S, FOX_WIDTH), BF16),
        grid=(B, n_slabs, S // tq),
        in_specs=[pl.BlockSpec((1, tq, SLAB), lambda b, g, i: (b, i, g)), kv_spec, kv_spec,
                  pl.BlockSpec((1, HEADS_PER_SLAB, S), lambda b, g, i: (b * n_slabs + g, 0, 0))],
        out_specs=pl.BlockSpec((1, tq, SLAB), lambda b, g, i: (b, i, g)),
        compiler_params=_params(3),
        name="fox_attention",
    )(q, k, v, f_slabs)


def _dil_attn_kernel(q_ref, k_ref, v_ref, o_ref, lse_ref, *, n_blocks):
    P = DIL_SPAN
    masks = _slab_head_masks(P)

    def block(q_rows, k_rows, valid):
        q = q_ref[q_rows, :]
        k = k_ref[k_rows, :]
        v = v_ref[k_rows, :]
        zero = jnp.zeros_like(q)
        o = jnp.zeros((P, SLAB), F32)
        lse = jnp.zeros((P, SLAB), F32)
        for h in range(HEADS_PER_SLAB):
            s = jnp.where(valid, _dot_nt(jnp.where(masks[h], q, zero), k), NEG)
            m = jnp.max(s, axis=-1, keepdims=True)
            p = jnp.exp(s - m)
            l = jnp.sum(p, axis=-1, keepdims=True)
            o = jnp.where(masks[h], _dot(p.astype(BF16), v) / l, o)
            lse = jnp.where(masks[h], m + jnp.log(l), lse)
        o_ref[q_rows, :] = o
        lse_ref[q_rows, :] = lse

    qi = lax.broadcasted_iota(jnp.int32, (P, P), 0)
    ki = lax.broadcasted_iota(jnp.int32, (P, P), 1)
    block(pl.ds(0, P), pl.ds(0, P), ki <= qi)

    qi2 = lax.broadcasted_iota(jnp.int32, (P, 2 * P), 0)
    ki2 = lax.broadcasted_iota(jnp.int32, (P, 2 * P), 1)
    valid2 = (ki2 >= qi2) & (ki2 <= qi2 + P)

    def body(n, _):
        start = pl.multiple_of(n * P, P)
        block(pl.ds(start, P), pl.ds(pl.multiple_of(start - P, P), 2 * P), valid2)
        return 0

    lax.fori_loop(1, n_blocks, body, 0)


def _dil_attention(q, k, v, dilation):
    B, d, L, _ = q.shape
    assert d == dilation
    spec = pl.BlockSpec((None, None, L, SLAB), lambda b, r: (b, r, 0, 0))
    return pl.pallas_call(
        functools.partial(_dil_attn_kernel, n_blocks=L // DIL_SPAN),
        out_shape=[jax.ShapeDtypeStruct((B, d, L, SLAB), F32)] * 2,
        grid=(B, d),
        in_specs=[spec] * 3,
        out_specs=[spec] * 2,
        compiler_params=_params(2),
        name=f"dil_attention_d{dilation}",
    )(q, k, v)


def _dil_merge_kernel(*refs):
    n_groups = len(DIL_PATTERNS)
    o_refs, l_refs = refs[:n_groups], refs[n_groups:2 * n_groups]
    out_ref, stage_ref = refs[2 * n_groups], refs[2 * n_groups + 1]
    tm = out_ref.shape[0]
    slot = 0
    outs, lses = [], []
    for src, dst in ((o_refs, outs), (l_refs, lses)):
        for (_, d), ref in zip(DIL_PATTERNS, src):
            if d == 1:
                dst.append(ref[0, 0])
                continue
            halves = []
            for j in range(SLAB // LANES):
                for r in range(d):
                    stage_ref[slot, pl.ds(r, tm // d, stride=d), :] = ref[0, r, :, j * LANES:(j + 1) * LANES]
                halves.append(stage_ref[slot])
                slot += 1
            dst.append(jnp.concatenate(halves, axis=1))
    m = functools.reduce(jnp.maximum, lses)
    es = [jnp.exp(l - m) for l in lses]
    num = es[0] * outs[0] + es[1] * outs[1] + es[2] * outs[2]
    out_ref[...] = (num / (es[0] + es[1] + es[2])).astype(BF16)


def _dil_merge(outs, lses, B, S, tm):
    T = B * S
    tiles_per_seq = S // tm
    specs = [pl.BlockSpec((1, d, tm // d, SLAB), lambda i: (i // tiles_per_seq, 0, i % tiles_per_seq, 0))
             for _, d in DIL_PATTERNS]
    n_strided = 2 * (SLAB // LANES) * sum(d > 1 for _, d in DIL_PATTERNS)
    return pl.pallas_call(
        _dil_merge_kernel,
        out_shape=jax.ShapeDtypeStruct((T, SLAB), BF16),
        grid=(T // tm,),
        in_specs=specs * 2,
        out_specs=pl.BlockSpec((tm, SLAB), lambda i: (i, 0)),
        scratch_shapes=[pltpu.VMEM((n_strided, tm, LANES), F32)],
        compiler_params=_params(1),
        name="dil_merge",
    )(*outs, *lses)


def _gated_merge_kernel(h_ref, oa_ref, ob_ref, oc_ref, od_ref, g0, g1, g2, g3, wa, wb, wc, wd, out_ref):
    h = h_ref[...]
    acc = None
    for o_ref, g_ref, w_ref in ((oa_ref, g0, wa), (ob_ref, g1, wb), (oc_ref, g2, wc), (od_ref, g3, wd)):
        term = jax.nn.sigmoid(_dot(h, g_ref[...])) * _dot(o_ref[...], w_ref[...])
        acc = term if acc is None else acc + term
    out_ref[...] = acc.astype(BF16)


def _gated_merge(h, branches, w_gate, w_branches, layer, tm, tn):
    T, D = h.shape
    n_col = D // tn
    row = lambda n: pl.BlockSpec((tm, n), lambda j, i: (i, 0))
    gate_specs = [pl.BlockSpec((None, D, tn), functools.partial(lambda j, i, br: (layer, 0, br * n_col + j), br=br))
                  for br in range(N_BRANCH)]
    w_specs = [pl.BlockSpec((None, w.shape[1], tn), lambda j, i: (layer, 0, j)) for w in w_branches]
    return pl.pallas_call(
        _gated_merge_kernel,
        out_shape=jax.ShapeDtypeStruct((T, D), BF16),
        grid=(n_col, T // tm),
        in_specs=[row(D)] + [row(o.shape[1]) for o in branches] + gate_specs + w_specs,
        out_specs=pl.BlockSpec((tm, tn), lambda j, i: (i, j)),
        compiler_params=_params(2),
        name="gated_merge",
    )(h, *branches, w_gate, w_gate, w_gate, w_gate, *w_branches)


def _residual_proj_kernel(a_ref, w_ref, x_ref, g_ref, xo_ref, h_ref):
    x = x_ref[...] + _dot(a_ref[...], w_ref[...])
    xo_ref[...] = x
    h_ref[...] = _rms(x, g_ref[...]).astype(h_ref.dtype)


def _residual_proj(a, w, x, g_next, layer, tm):
    T, D = x.shape
    K = a.shape[1]
    row = lambda n: pl.BlockSpec((tm, n), lambda i: (i, 0))
    return pl.pallas_call(
        _residual_proj_kernel,
        out_shape=[jax.ShapeDtypeStruct((T, D), F32), jax.ShapeDtypeStruct((T, D), BF16)],
        grid=(T // tm,),
        in_specs=[row(K), _layer_weight(layer, K, D), row(D), _resident((1, D))],
        out_specs=[row(D), row(D)],
        compiler_params=_params(1),
        name=f"residual_proj_k{K}",
    )(a, w, x, g_next)


FFN_CHUNK = SLAB


def _ffn_up_kernel(h_ref, wg_ref, wv_ref, cg_ref, cv_ref, act_ref, wg_bf, wv_bf, prev_g, prev_v, *, tiles_per_seq):
    i = pl.program_id(1)

    @pl.when(i == 0)
    def _():
        wg_bf[...] = wg_ref[...].astype(BF16)
        wv_bf[...] = wv_ref[...].astype(BF16)

    @pl.when(i % tiles_per_seq == 0)
    def _():
        prev_g[...] = jnp.zeros_like(prev_g)
        prev_v[...] = jnp.zeros_like(prev_v)

    h = h_ref[...]
    tm = h.shape[0]
    for c in range(act_ref.shape[1] // FFN_CHUNK):
        cols = slice(c * FFN_CHUNK, (c + 1) * FFN_CHUNK)
        zg = _dot(h, wg_bf[:, cols])
        zv = _dot(h, wv_bf[:, cols])
        up_gate = _causal_conv3(zg, cg_ref[:, cols], prev_g[:, cols])
        up_val = _causal_conv3(zv, cv_ref[:, cols], prev_v[:, cols])
        prev_g[:, cols] = zg[tm - SUBLANES:, :]
        prev_v[:, cols] = zv[tm - SUBLANES:, :]
        act_ref[:, cols] = (jax.nn.silu(up_gate) * up_val).astype(BF16)


def _ffn_up(h, w_up, conv_w, layer, tm, tn, tiles_per_seq):
    T, D = h.shape
    d_ff = w_up.shape[2] // 2
    n_col = d_ff // tn
    return pl.pallas_call(
        functools.partial(_ffn_up_kernel, tiles_per_seq=tiles_per_seq),
        out_shape=jax.ShapeDtypeStruct((T, d_ff), BF16),
        grid=(n_col, T // tm),
        in_specs=[pl.BlockSpec((tm, D), lambda j, i: (i, 0)),
                  pl.BlockSpec((None, D, tn), lambda j, i: (layer, 0, j)),
                  pl.BlockSpec((None, D, tn), lambda j, i: (layer, 0, n_col + j)),
                  pl.BlockSpec((None, CONV_TAPS, tn), lambda j, i: (layer, 0, j)),
                  pl.BlockSpec((None, CONV_TAPS, tn), lambda j, i: (layer, 0, n_col + j))],
        out_specs=pl.BlockSpec((tm, tn), lambda j, i: (i, j)),
        scratch_shapes=[pltpu.VMEM((D, tn), BF16)] * 2 + [pltpu.VMEM((SUBLANES, tn), F32)] * 2,
        compiler_params=_params(2),
        name="ffn_up",
    )(h, w_up, w_up, conv_w, conv_w)


def _ple_kernel(h_ref, wg_ref, p_ref, wp_ref, x_ref, g_ref, *out_refs, final):
    gate = jax.nn.sigmoid(_dot(h_ref[...], wg_ref[...]))
    x = x_ref[...] + gate * _dot(p_ref[...].astype(BF16), wp_ref[...])
    if not final:
        out_refs[0][...] = x
    out_refs[-1][...] = _rms(x, g_ref[...]).astype(out_refs[-1].dtype)


def _ple(h, w_gate, p, w_proj, x, g_next, layer, tm, final):
    T, D = x.shape
    P = p.shape[2]
    row = lambda n: pl.BlockSpec((tm, n), lambda i: (i, 0))
    if final:
        out_shape = [jax.ShapeDtypeStruct((T, D), F32)]
    else:
        out_shape = [jax.ShapeDtypeStruct((T, D), F32), jax.ShapeDtypeStruct((T, D), BF16)]
    return pl.pallas_call(
        functools.partial(_ple_kernel, final=final),
        out_shape=out_shape,
        grid=(T // tm,),
        in_specs=[row(D), _layer_weight(layer, D, D), pl.BlockSpec((None, tm, P), lambda i: (layer, i, 0)),
                  _layer_weight(layer, P, D), row(D), _resident((1, D))],
        out_specs=[row(D)] * len(out_shape),
        compiler_params=_params(1),
        name="ple_final" if final else "ple",
    )(h, w_gate, p, w_proj, x, g_next)


def kernel(x, p, positions, norm_mix_g, w_in, fox_forget_b, shortconv_w, sgu_norm_g, sgu_w, sgu_b, w_br_fox, w_br_conv, w_br_sgu, w_br_dil, w_out, norm_ffn_g, w_up, ffn_conv_w, w_down, norm_ple_g, w_ple_gate, w_ple_proj, final_norm_g):
    B, S, D = x.shape
    depth = w_in.shape[0]
    T = B * S
    d_ff = w_down.shape[1]

    tm = _tile(S, 512)
    tm_wide = _tile(S, 256)
    tq = _tile(S, 512)
    tn_merge = _tile(D, 512)
    tn_ffn = _tile(d_ff, 512)
    tiles_per_seq = S // tm

    sizes = (3 * FOX_WIDTH, FOX_HEADS, 3 * CONV_WIDTH, 2 * SGU_WIDTH, 3 * DIL_WIDTH, N_BRANCH * D)
    offs = [0]
    for n in sizes:
        offs.append(offs[-1] + n)

    bf = lambda w: w.astype(BF16)
    w_fox = bf(w_in[:, :, offs[0]:offs[1]])
    w_f = jnp.pad(bf(w_in[:, :, offs[1]:offs[2]]), ((0, 0), (0, 0), (0, LANES - FOX_HEADS)))
    b_f = jnp.pad(fox_forget_b.astype(F32), ((0, 0), (0, LANES - FOX_HEADS)))[:, None, :]
    w_conv = bf(w_in[:, :, offs[2]:offs[3]])
    w_sgu = bf(w_in[:, :, offs[3]:offs[4]])
    w_dil = bf(w_in[:, :, offs[4]:offs[5]])
    w_gate = bf(w_in[:, :, offs[5]:offs[6]])
    w_brs = [bf(w_br_fox), bf(w_br_conv), bf(w_br_sgu), bf(w_br_dil)]
    w_out_b, w_down_b = bf(w_out), bf(w_down)
    w_pg_b, w_pp_b = bf(w_ple_gate), bf(w_ple_proj)
    sgu_bias = jnp.repeat(jnp.swapaxes(sgu_b, 1, 2), SGU_GROUP_DIM, axis=2)
    sgu_norm = sgu_norm_g[:, None, :]
    p_tok = p.reshape(depth, T, p.shape[-1])
    row_vec = lambda g: g[None, :]

    tables = _rope_tables(positions, tm)
    xf = x.reshape(T, D)
    h = _rmsnorm(xf, row_vec(norm_mix_g[0]), tm)
    n_slabs = FOX_WIDTH // SLAB
    out = None
    for i in range(depth):
        qa, ka, va, f_pad = _fox_proj(h, w_fox, w_f, b_f, i, tm, tiles_per_seq)
        f_slabs = f_pad[:, :FOX_HEADS].reshape(B, S, FOX_HEADS).transpose(0, 2, 1)
        f_slabs = f_slabs.reshape(B * n_slabs, HEADS_PER_SLAB, S)
        o_a = _fox_attention(qa.reshape(B, S, FOX_WIDTH), ka.reshape(B, S, FOX_WIDTH),
                             va.reshape(B, S, FOX_WIDTH), f_slabs, B, S, tq).reshape(T, FOX_WIDTH)
        o_b = _conv_branch(h, w_conv, shortconv_w, i, tm, tiles_per_seq)
        o_c = _sgu_branch(h, w_sgu, sgu_norm, sgu_w, sgu_bias, i, tm)
        qkv = _dil_proj(h, w_dil, tables, i, B, S, tm)
        outs, lses = [], []
        for g, (window, dil) in enumerate(DIL_PATTERNS):
            assert window // dil == DIL_SPAN
            o_g, l_g = _dil_attention(*qkv[3 * g:3 * g + 3], dil)
            outs.append(o_g)
            lses.append(l_g)
        o_d = _dil_merge(outs, lses, B, S, tm)

        merged = _gated_merge(h, (o_a, o_b, o_c, o_d), w_gate, w_brs, i, tm, tn_merge)
        xf, h = _residual_proj(merged, w_out_b, xf, row_vec(norm_ffn_g[i]), i, tm_wide)

        act = _ffn_up(h, w_up, ffn_conv_w, i, tm, tn_ffn, tiles_per_seq)
        xf, h = _residual_proj(act, w_down_b, xf, row_vec(norm_ple_g[i]), i, tm_wide)

        final = i == depth - 1
        g_next = final_norm_g if final else norm_mix_g[i + 1]
        res = _ple(h, w_pg_b, p_tok, w_pp_b, xf, row_vec(g_next), i, tm_wide, final)
        if final:
            out = res[0]
        else:
            xf, h = res
    return out.reshape(B, S, D)
```

```python
import functools

import jax
import jax.numpy as jnp
from jax import lax
from jax.experimental import pallas as pl
from jax.experimental.pallas import tpu as pltpu

F32 = jnp.float32
BF16 = jnp.bfloat16

HEAD_DIM = 64
EPS = 1e-6
FOX_HEADS = 8
FOX_WIDTH = FOX_HEADS * HEAD_DIM
CONV_WIDTH = 512
CONV_TAPS = 3
SGU_GROUPS = 4
SGU_GROUP_DIM = 128
SGU_WIDTH = SGU_GROUPS * SGU_GROUP_DIM
SGU_CHUNK = 128
DIL_PATTERNS = ((128, 1), (512, 4), (2048, 16))
DIL_HEADS_PER_GROUP = 4
DIL_HEADS = DIL_HEADS_PER_GROUP * len(DIL_PATTERNS)
DIL_WIDTH = DIL_HEADS * HEAD_DIM
DIL_OUT = DIL_HEADS_PER_GROUP * HEAD_DIM
DIL_SPAN = 128
ROPE_THETA = 500000.0
ROPE_DIM = HEAD_DIM // 4
N_BRANCH = 4

LANES = 128
SUBLANES = 8
BF16_ROWS = 16
HEADS_PER_SLAB = 4
SLAB = HEADS_PER_SLAB * HEAD_DIM
VMEM_LIMIT = 56 * 1024 * 1024
NEG = -1e30
QK_SCALE = HEAD_DIM ** -0.5


def _params(n_axes):
    return pltpu.CompilerParams(dimension_semantics=("arbitrary",) * n_axes, vmem_limit_bytes=VMEM_LIMIT)


def _resident(shape, index=None):
    index = (0,) * len(shape) if index is None else index
    return pl.BlockSpec(shape, lambda *_: index, pipeline_mode=pl.Buffered(1))


def _layer_weight(layer, rows, cols):
    return _resident((None, rows, cols), (layer, 0, 0))


def _tile(n, pref):
    t = min(n, pref)
    assert n % t == 0, (n, t)
    return t


def _rms(x, g):
    var = jnp.mean(x * x, axis=-1, keepdims=True)
    return x * lax.rsqrt(var + EPS) * g


def _dot(a, b):
    return jnp.dot(a, b, preferred_element_type=F32)


def _dot_nt(a, b):
    return lax.dot_general(a, b, (((1,), (1,)), ((), ())), preferred_element_type=F32)


def _causal_conv3(z, w, prev):
    y = w[0:1] * pltpu.roll(z, 2, 0) + w[1:2] * pltpu.roll(z, 1, 0) + w[2:3] * z
    zh = z[0:BF16_ROWS]
    row = lax.broadcasted_iota(jnp.int32, zh.shape, 0)
    p1 = prev[SUBLANES - 1:SUBLANES]
    p2 = prev[SUBLANES - 2:SUBLANES - 1]
    z1 = jnp.where(row == 0, p1, pltpu.roll(zh, 1, 0))
    z2 = jnp.where(row == 0, p2, jnp.where(row == 1, p1, pltpu.roll(zh, 2, 0)))
    yh = w[0:1] * z2 + w[1:2] * z1 + w[2:3] * zh
    return jnp.concatenate([yh, y[BF16_ROWS:]], axis=0)


def _rmsnorm_kernel(x_ref, g_ref, h_ref):
    h_ref[...] = _rms(x_ref[...], g_ref[...]).astype(h_ref.dtype)


def _rmsnorm(x, g, tm):
    T, D = x.shape
    return pl.pallas_call(
        _rmsnorm_kernel,
        out_shape=jax.ShapeDtypeStruct((T, D), BF16),
        grid=(T // tm,),
        in_specs=[pl.BlockSpec((tm, D), lambda i: (i, 0)), _resident((1, D))],
        out_specs=pl.BlockSpec((tm, D), lambda i: (i, 0)),
        compiler_params=_params(1),
        name="rmsnorm",
    )(x, g)


def _rope_table_kernel(pos_ref, inv_ref, c_ref, s1_ref, s2_ref):
    ang = pos_ref[...] * inv_ref[...]
    cos = jnp.cos(ang)
    sin = jnp.sin(ang)
    lane = lax.broadcasted_iota(jnp.int32, ang.shape, 1) % HEAD_DIM
    half = ROPE_DIM // 2
    c_ref[...] = jnp.where(lane < ROPE_DIM, cos, 1.0)
    s1_ref[...] = jnp.where((lane >= half) & (lane < ROPE_DIM), sin, 0.0)
    s2_ref[...] = jnp.where(lane < half, -sin, 0.0)


def _rope_tables(positions, tm):
    T = positions.size
    half = ROPE_DIM // 2
    inv = ROPE_THETA ** (-jnp.arange(half, dtype=F32) * (2.0 / ROPE_DIM))
    lane = jnp.arange(LANES) % HEAD_DIM
    inv_lane = jnp.where(lane < ROPE_DIM, inv[lane % half], 0.0).astype(F32)[None, :]
    posf = jnp.broadcast_to(positions.reshape(T, 1).astype(F32), (T, LANES))
    spec = pl.BlockSpec((tm, LANES), lambda i: (i, 0))
    return pl.pallas_call(
        _rope_table_kernel,
        out_shape=[jax.ShapeDtypeStruct((T, LANES), F32)] * 3,
        grid=(T // tm,),
        in_specs=[spec, _resident((1, LANES))],
        out_specs=[spec] * 3,
        compiler_params=_params(1),
        name="rope_tables",
    )(posf, inv_lane)


def _fox_proj_kernel(h_ref, w_ref, wf_ref, bf_ref, q_ref, k_ref, v_ref, carry_ref, *, tiles_per_seq):
    @pl.when(pl.program_id(0) % tiles_per_seq == 0)
    def _():
        carry_ref[...] = jnp.zeros_like(carry_ref)

    h = h_ref[...]
    W = FOX_WIDTH
    q_ref[...] = (_dot(h, w_ref[:, 0:W]) * QK_SCALE).astype(BF16)
    k = _dot(h, w_ref[:, W:2 * W]).astype(BF16)
    v = _dot(h, w_ref[:, 2 * W:3 * W]).astype(BF16)

    x = jax.nn.log_sigmoid(_dot(h, wf_ref[...]) + bf_ref[...])
    tm = x.shape[0]
    row = lax.broadcasted_iota(jnp.int32, x.shape, 0)
    lane = lax.broadcasted_iota(jnp.int32, x.shape, 1)
    s = 1
    while s < tm:
        x = x + jnp.where(row >= s, pltpu.roll(x, s, 0), 0.0)
        s *= 2
    x = x + carry_ref[0:1, :]
    carry_ref[...] = jnp.broadcast_to(x[tm - 1:tm, :], carry_ref.shape)

    x = jnp.where(lane < FOX_HEADS, x, 0.0)
    hi = x.astype(BF16).astype(F32)
    mid = (x - hi).astype(BF16).astype(F32)
    lo = (x - hi - mid).astype(BF16).astype(F32)
    f_lanes = (hi + pltpu.roll(mid, FOX_HEADS, 1) + pltpu.roll(lo, 2 * FOX_HEADS, 1)).astype(BF16)
    one_lane = jnp.where(lane == 0, 1.0, 0.0).astype(BF16)
    for j in range(FOX_WIDTH // LANES):
        k_ref[:, j * SLAB:j * SLAB + LANES] = k[:, j * LANES:(j + 1) * LANES]
        k_ref[:, j * SLAB + LANES:(j + 1) * SLAB] = f_lanes
        v_ref[:, j * SLAB:j * SLAB + LANES] = v[:, j * LANES:(j + 1) * LANES]
        v_ref[:, j * SLAB + LANES:(j + 1) * SLAB] = one_lane


def _fox_proj(h, w_fox, b_f, layer, tm, tiles_per_seq):
    T, D = h.shape
    W = FOX_WIDTH
    n_pairs = FOX_WIDTH // LANES
    row = lambda n: pl.BlockSpec((tm, n), lambda i: (i, 0))
    return pl.pallas_call(
        functools.partial(_fox_proj_kernel, tiles_per_seq=tiles_per_seq),
        out_shape=[jax.ShapeDtypeStruct((T, W), BF16)] + [jax.ShapeDtypeStruct((T, n_pairs * SLAB), BF16)] * 2,
        grid=(T // tm,),
        in_specs=[row(D), _resident((None, D, 3 * W), (layer, 0, 0)),
                  _resident((None, D, LANES), (layer, 0, 3 * W // LANES)), _layer_weight(layer, 1, LANES)],
        out_specs=[row(W), row(n_pairs * SLAB), row(n_pairs * SLAB)],
        scratch_shapes=[pltpu.VMEM((SUBLANES, LANES), F32)],
        compiler_params=_params(1),
        name="fox_proj",
    )(h, w_fox, w_fox, b_f)


def _conv_branch_kernel(h_ref, w_ref, cw_ref, o_ref, prev_ref, *, tiles_per_seq):
    @pl.when(pl.program_id(0) % tiles_per_seq == 0)
    def _():
        prev_ref[...] = jnp.zeros_like(prev_ref)

    h = h_ref[...]
    W = CONV_WIDTH
    tm = h.shape[0]
    xb = _dot(h, w_ref[:, 0:W])
    gate_b = _dot(h, w_ref[:, W:2 * W])
    gate_c = _dot(h, w_ref[:, 2 * W:3 * W])
    z = gate_c * xb
    y = _causal_conv3(z, cw_ref[...], prev_ref[...])
    prev_ref[...] = z[tm - SUBLANES:, :]
    o_ref[...] = (gate_b * y).astype(BF16)


def _conv_branch(h, w_mix, col_block, conv_w, layer, tm, tiles_per_seq):
    T, D = h.shape
    W = CONV_WIDTH
    return pl.pallas_call(
        functools.partial(_conv_branch_kernel, tiles_per_seq=tiles_per_seq),
        out_shape=jax.ShapeDtypeStruct((T, W), BF16),
        grid=(T // tm,),
        in_specs=[pl.BlockSpec((tm, D), lambda i: (i, 0)), _resident((None, D, 3 * W), (layer, 0, col_block)),
                  _layer_weight(layer, CONV_TAPS, W)],
        out_specs=pl.BlockSpec((tm, W), lambda i: (i, 0)),
        scratch_shapes=[pltpu.VMEM((SUBLANES, W), F32)],
        compiler_params=_params(1),
        name="conv_branch",
    )(h, w_mix, conv_w)


def _sgu_branch_kernel(h_ref, w_ref, ng_ref, ws_ref, bias_ref, o_ref):
    h = h_ref[...]
    W = SGU_WIDTH
    C = SGU_CHUNK
    G = SGU_GROUP_DIM
    u = jax.nn.gelu(_dot(h, w_ref[:, 0:W]))
    v = _rms(jax.nn.gelu(_dot(h, w_ref[:, W:2 * W])), ng_ref[...]).astype(BF16)
    row = lax.broadcasted_iota(jnp.int32, (C, C), 0)
    col = lax.broadcasted_iota(jnp.int32, (C, C), 1)
    for g in range(SGU_GROUPS):
        w_tril = jnp.where(col <= row, ws_ref[g], 0.0).astype(BF16)
        cols = slice(g * G, (g + 1) * G)
        for c in range(h.shape[0] // C):
            rows = slice(c * C, (c + 1) * C)
            mixed = _dot(w_tril, v[rows, cols]) + bias_ref[:, cols]
            o_ref[rows, cols] = (u[rows, cols] * mixed).astype(BF16)


def _sgu_branch(h, w_mix, col_block, norm_g, w_s, bias_full, layer, tm):
    T, D = h.shape
    W = SGU_WIDTH
    return pl.pallas_call(
        _sgu_branch_kernel,
        out_shape=jax.ShapeDtypeStruct((T, W), BF16),
        grid=(T // tm,),
        in_specs=[pl.BlockSpec((tm, D), lambda i: (i, 0)), _resident((None, D, 2 * W), (layer, 0, col_block)),
                  _layer_weight(layer, 1, W),
                  _resident((None, SGU_GROUPS, SGU_CHUNK, SGU_CHUNK), (layer, 0, 0, 0)),
                  _layer_weight(layer, SGU_CHUNK, W)],
        out_specs=pl.BlockSpec((tm, W), lambda i: (i, 0)),
        compiler_params=_params(1),
        name="sgu_branch",
    )(h, w_mix, norm_g, w_s, bias_full)


def _dil_proj_kernel(h_ref, w_ref, c_ref, s1_ref, s2_ref, *refs):
    n_groups = len(DIL_PATTERNS)
    out_refs = refs[:3 * n_groups]
    stage_ref = refs[3 * n_groups]
    h = h_ref[...]
    tm = h.shape[0]
    cos = c_ref[...]
    s1 = s1_ref[...]
    s2 = s2_ref[...]
    shift = ROPE_DIM // 2
    slot = 0
    for which, rope, scale in ((0, True, QK_SCALE), (1, True, None), (2, False, None)):
        for g, (_, d) in enumerate(DIL_PATTERNS):
            col = which * DIL_WIDTH + g * SLAB
            x = _dot(h, w_ref[:, col:col + SLAB])
            parts = []
            for j in range(SLAB // LANES):
                xj = x[:, j * LANES:(j + 1) * LANES]
                if rope:
                    xj = xj * cos + pltpu.roll(xj, shift, 1) * s1 + pltpu.roll(xj, LANES - shift, 1) * s2
                if scale is not None:
                    xj = xj * scale
                parts.append(xj)
            out = out_refs[g * 3 + which]
            for j, xj in enumerate(parts):
                lanes = slice(j * LANES, (j + 1) * LANES)
                if d == 1:
                    out[0, 0, :, lanes] = xj.astype(BF16)
                else:
                    stage_ref[slot] = xj
                    for r in range(d):
                        out[0, r, :, lanes] = stage_ref[slot, pl.ds(r, tm // d, stride=d), :].astype(BF16)
                    slot += 1


def _dil_proj(h, w_mix, col_block, tables, layer, B, S, tm):
    T, D = h.shape
    tiles_per_seq = S // tm
    row = lambda n: pl.BlockSpec((tm, n), lambda i: (i, 0))
    out_shape, out_specs = [], []
    n_strided = 0
    for _, d in DIL_PATTERNS:
        assert (tm // d) % BF16_ROWS == 0
        n_strided += 3 * (SLAB // LANES) * (d > 1)
        for _ in range(3):
            out_shape.append(jax.ShapeDtypeStruct((B, d, S // d, SLAB), BF16))
            out_specs.append(pl.BlockSpec((1, d, tm // d, SLAB),
                                          lambda i: (i // tiles_per_seq, 0, i % tiles_per_seq, 0)))
    return pl.pallas_call(
        _dil_proj_kernel,
        out_shape=out_shape,
        grid=(T // tm,),
        in_specs=[row(D), _resident((None, D, 3 * DIL_WIDTH), (layer, 0, col_block)), row(LANES), row(LANES),
                  row(LANES)],
        out_specs=out_specs,
        scratch_shapes=[pltpu.VMEM((n_strided, tm, LANES), F32)],
        compiler_params=_params(1),
        name="dil_proj",
    )(h, w_mix, *tables)


def _slab_head_masks(n_rows):
    lane = lax.broadcasted_iota(jnp.int32, (n_rows, SLAB), 1)
    return [(lane >= h * HEAD_DIM) & (lane < (h + 1) * HEAD_DIM) for h in range(HEADS_PER_SLAB)]


FOX_PAIRS_PER_STEP = 2


def _fox_attn_kernel(q_ref, k_ref, v_ref, o_ref, *, tq):
    group = pl.program_id(1)
    i = pl.program_id(2)
    q0 = pl.multiple_of(i * tq, tq)
    lane = lax.broadcasted_iota(jnp.int32, (tq, LANES), 1)
    rowi = lax.broadcasted_iota(jnp.int32, (2 * tq, tq), 0) % tq
    coli = lax.broadcasted_iota(jnp.int32, (2 * tq, tq), 1)

    def stacked_queries(c):
        q = q_ref[0, :, c * LANES:(c + 1) * LANES]
        zero = jnp.zeros_like(q)
        halves = []
        for e in range(2):
            head = 2 * (FOX_PAIRS_PER_STEP * group + c) + e
            f_pick = (lane == head) | (lane == head + FOX_HEADS) | (lane == head + 2 * FOX_HEADS)
            halves.append(jnp.concatenate(
                [jnp.where(lane // HEAD_DIM == e, q, zero), jnp.where(f_pick, -1.0, 0.0).astype(BF16)], axis=1))
        return jnp.concatenate(halves, axis=0)

    qs = [stacked_queries(c) for c in range(FOX_PAIRS_PER_STEP)]

    def block(k0, carry, diagonal):
        slabs = [slice(c * SLAB, (c + 1) * SLAB) for c in range(FOX_PAIRS_PER_STEP)]
        logits = [_dot_nt(qs[c], k_ref[0, pl.ds(k0, tq), slabs[c]]) for c in range(FOX_PAIRS_PER_STEP)]
        stats = []
        for (m, _), s in zip(carry, logits):
            if diagonal:
                s = jnp.where(coli <= rowi, s, NEG)
            m_new = jnp.maximum(m, jnp.max(s, axis=-1, keepdims=True))
            stats.append((m_new, jnp.exp(m - m_new), jnp.exp(s - m_new).astype(BF16)))
        return tuple((m_new, alpha * acc + _dot(p, v_ref[0, pl.ds(k0, tq), slabs[c]]))
                     for c, ((_, acc), (m_new, alpha, p)) in enumerate(zip(carry, stats)))

    init = tuple((jnp.full((2 * tq, 1), NEG, F32), jnp.zeros((2 * tq, SLAB), F32))
                 for _ in range(FOX_PAIRS_PER_STEP))
    carry = lax.fori_loop(0, i, lambda kb, c: block(pl.multiple_of(kb * tq, tq), c, False), init)
    for c, (_, acc) in enumerate(block(q0, carry, True)):
        out = acc[:, 0:LANES] / acc[:, LANES:LANES + 1]
        o_ref[0, :, c * LANES:(c + 1) * LANES] = jnp.where(lane // HEAD_DIM == 0, out[0:tq],
                                                           out[tq:2 * tq]).astype(BF16)


def _fox_attention(q, k_slabs, v_slabs, B, S, tq):
    n_groups = FOX_WIDTH // LANES // FOX_PAIRS_PER_STEP
    kv_spec = pl.BlockSpec((1, S, FOX_PAIRS_PER_STEP * SLAB), lambda b, g, i: (b, 0, g))
    q_spec = pl.BlockSpec((1, tq, FOX_PAIRS_PER_STEP * LANES), lambda b, g, i: (b, i, g))
    return pl.pallas_call(
        functools.partial(_fox_attn_kernel, tq=tq),
        out_shape=jax.ShapeDtypeStruct((B, S, FOX_WIDTH), BF16),
        grid=(B, n_groups, S // tq),
        in_specs=[q_spec, kv_spec, kv_spec],
        out_specs=q_spec,
        compiler_params=_params(3),
        name="fox_attention",
    )(q, k_slabs, v_slabs)


def _dil_attn_kernel(q_ref, k_ref, v_ref, o_ref, lse_ref, *, n_blocks):
    P = DIL_SPAN
    H = HEADS_PER_SLAB
    masks = _slab_head_masks(P)

    def attend(q_rows, k_rows, valid):
        q = q_ref[q_rows, :]
        zero = jnp.zeros_like(q)
        qs = jnp.concatenate([jnp.where(masks[h], q, zero) for h in range(H)], axis=0)
        s = jnp.where(valid, _dot_nt(qs, k_ref[k_rows, :]), NEG)
        m = jnp.max(s, axis=-1, keepdims=True)
        p = jnp.exp(s - m)
        l = jnp.sum(p, axis=-1, keepdims=True)
        pv = _dot(p.astype(BF16), v_ref[k_rows, :]) / l
        lse = jnp.broadcast_to(m + jnp.log(l), (H * P, SLAB))
        o, ls = pv[0:P], lse[0:P]
        for h in range(1, H):
            o = jnp.where(masks[h], pv[h * P:(h + 1) * P], o)
            ls = jnp.where(masks[h], lse[h * P:(h + 1) * P], ls)
        o_ref[q_rows, :] = o
        lse_ref[q_rows, :] = ls

    qi = lax.broadcasted_iota(jnp.int32, (H * P, P), 0) % P
    ki = lax.broadcasted_iota(jnp.int32, (H * P, P), 1)
    attend(pl.ds(0, P), pl.ds(0, P), ki <= qi)

    qi2 = lax.broadcasted_iota(jnp.int32, (H * P, 2 * P), 0) % P
    ki2 = lax.broadcasted_iota(jnp.int32, (H * P, 2 * P), 1)
    valid2 = (ki2 >= qi2) & (ki2 <= qi2 + P)

    def body(n, _):
        start = pl.multiple_of(n * P, P)
        attend(pl.ds(start, P), pl.ds(pl.multiple_of(start - P, P), 2 * P), valid2)
        return 0

    lax.fori_loop(1, n_blocks, body, 0, unroll=True if n_blocks <= 8 else 2)


def _dil_attention(q, k, v, dilation):
    B, d, L, _ = q.shape
    assert d == dilation
    spec = pl.BlockSpec((None, None, L, SLAB), lambda b, r: (b, r, 0, 0))
    return pl.pallas_call(
        functools.partial(_dil_attn_kernel, n_blocks=L // DIL_SPAN),
        out_shape=[jax.ShapeDtypeStruct((B, d, L, SLAB), F32)] * 2,
        grid=(B, d),
        in_specs=[spec] * 3,
        out_specs=[spec] * 2,
        compiler_params=_params(2),
        name=f"dil_attention_d{dilation}",
    )(q, k, v)


def _dil_merge_kernel(*refs):
    n_groups = len(DIL_PATTERNS)
    o_refs, l_refs = refs[:n_groups], refs[n_groups:2 * n_groups]
    out_ref, stage_ref = refs[2 * n_groups], refs[2 * n_groups + 1]
    tm = out_ref.shape[0]
    slot = 0
    outs, lses = [], []
    for src, dst in ((o_refs, outs), (l_refs, lses)):
        for (_, d), ref in zip(DIL_PATTERNS, src):
            if d == 1:
                dst.append(ref[0, 0])
                continue
            halves = []
            for j in range(SLAB // LANES):
                for r in range(d):
                    stage_ref[slot, pl.ds(r, tm // d, stride=d), :] = ref[0, r, :, j * LANES:(j + 1) * LANES]
                halves.append(stage_ref[slot])
                slot += 1
            dst.append(jnp.concatenate(halves, axis=1))
    m = functools.reduce(jnp.maximum, lses)
    es = [jnp.exp(l - m) for l in lses]
    num = es[0] * outs[0] + es[1] * outs[1] + es[2] * outs[2]
    out_ref[...] = (num / (es[0] + es[1] + es[2])).astype(BF16)


def _dil_merge(outs, lses, B, S, tm):
    T = B * S
    tiles_per_seq = S // tm
    specs = [pl.BlockSpec((1, d, tm // d, SLAB), lambda i: (i // tiles_per_seq, 0, i % tiles_per_seq, 0))
             for _, d in DIL_PATTERNS]
    n_strided = 2 * (SLAB // LANES) * sum(d > 1 for _, d in DIL_PATTERNS)
    return pl.pallas_call(
        _dil_merge_kernel,
        out_shape=jax.ShapeDtypeStruct((T, SLAB), BF16),
        grid=(T // tm,),
        in_specs=specs * 2,
        out_specs=pl.BlockSpec((tm, SLAB), lambda i: (i, 0)),
        scratch_shapes=[pltpu.VMEM((n_strided, tm, LANES), F32)],
        compiler_params=_params(1),
        name="dil_merge",
    )(*outs, *lses)


def _gated_merge_kernel(h_ref, oa_ref, ob_ref, oc_ref, od_ref, g0, g1, g2, g3, wa, wb, wc, wd, out_ref):
    h = h_ref[...]
    acc = None
    for o_ref, g_ref, w_ref in ((oa_ref, g0, wa), (ob_ref, g1, wb), (oc_ref, g2, wc), (od_ref, g3, wd)):
        term = jax.nn.sigmoid(_dot(h, g_ref[...])) * _dot(o_ref[...], w_ref[...])
        acc = term if acc is None else acc + term
    out_ref[...] = acc.astype(BF16)


def _gated_merge(h, branches, w_mix, gate_block, w_branches, layer, tm, tn):
    T, D = h.shape
    n_col = D // tn
    row = lambda n: pl.BlockSpec((tm, n), lambda j, i: (i, 0))
    gate_specs = [pl.BlockSpec((None, D, tn), functools.partial(lambda j, i, br: (layer, 0, gate_block + br * n_col + j), br=br))
                  for br in range(N_BRANCH)]
    w_specs = [pl.BlockSpec((None, w.shape[1], tn), lambda j, i: (layer, 0, j)) for w in w_branches]
    return pl.pallas_call(
        _gated_merge_kernel,
        out_shape=jax.ShapeDtypeStruct((T, D), BF16),
        grid=(n_col, T // tm),
        in_specs=[row(D)] + [row(o.shape[1]) for o in branches] + gate_specs + w_specs,
        out_specs=pl.BlockSpec((tm, tn), lambda j, i: (i, j)),
        compiler_params=_params(2),
        name="gated_merge",
    )(h, *branches, w_mix, w_mix, w_mix, w_mix, *w_branches)


def _residual_proj_kernel(a_ref, w_ref, x_ref, g_ref, xo_ref, h_ref):
    x = x_ref[...] + _dot(a_ref[...], w_ref[...])
    xo_ref[...] = x
    h_ref[...] = _rms(x, g_ref[...]).astype(h_ref.dtype)


def _residual_proj(a, w, x, g_next, layer, tm):
    T, D = x.shape
    K = a.shape[1]
    row = lambda n: pl.BlockSpec((tm, n), lambda i: (i, 0))
    return pl.pallas_call(
        _residual_proj_kernel,
        out_shape=[jax.ShapeDtypeStruct((T, D), F32), jax.ShapeDtypeStruct((T, D), BF16)],
        grid=(T // tm,),
        in_specs=[row(K), _layer_weight(layer, K, D), row(D), _resident((1, D))],
        out_specs=[row(D), row(D)],
        compiler_params=_params(1),
        name=f"residual_proj_k{K}",
    )(a, w, x, g_next)


FFN_CHUNK = SLAB


def _ffn_up_kernel(h_ref, wg_ref, wv_ref, cg_ref, cv_ref, act_ref, wg_bf, wv_bf, prev_g, prev_v, *, tiles_per_seq):
    i = pl.program_id(1)

    @pl.when(i == 0)
    def _():
        wg_bf[...] = wg_ref[...].astype(BF16)
        wv_bf[...] = wv_ref[...].astype(BF16)

    @pl.when(i % tiles_per_seq == 0)
    def _():
        prev_g[...] = jnp.zeros_like(prev_g)
        prev_v[...] = jnp.zeros_like(prev_v)

    h = h_ref[...]
    tm = h.shape[0]
    for c in range(act_ref.shape[1] // FFN_CHUNK):
        cols = slice(c * FFN_CHUNK, (c + 1) * FFN_CHUNK)
        zg = _dot(h, wg_bf[:, cols])
        zv = _dot(h, wv_bf[:, cols])
        up_gate = _causal_conv3(zg, cg_ref[:, cols], prev_g[:, cols])
        up_val = _causal_conv3(zv, cv_ref[:, cols], prev_v[:, cols])
        prev_g[:, cols] = zg[tm - SUBLANES:, :]
        prev_v[:, cols] = zv[tm - SUBLANES:, :]
        act_ref[:, cols] = (jax.nn.silu(up_gate) * up_val).astype(BF16)


def _ffn_up(h, w_up, conv_w, layer, tm, tn, tiles_per_seq):
    T, D = h.shape
    d_ff = w_up.shape[2] // 2
    n_col = d_ff // tn
    return pl.pallas_call(
        functools.partial(_ffn_up_kernel, tiles_per_seq=tiles_per_seq),
        out_shape=jax.ShapeDtypeStruct((T, d_ff), BF16),
        grid=(n_col, T // tm),
        in_specs=[pl.BlockSpec((tm, D), lambda j, i: (i, 0)),
                  pl.BlockSpec((None, D, tn), lambda j, i: (layer, 0, j)),
                  pl.BlockSpec((None, D, tn), lambda j, i: (layer, 0, n_col + j)),
                  pl.BlockSpec((None, CONV_TAPS, tn), lambda j, i: (layer, 0, j)),
                  pl.BlockSpec((None, CONV_TAPS, tn), lambda j, i: (layer, 0, n_col + j))],
        out_specs=pl.BlockSpec((tm, tn), lambda j, i: (i, j)),
        scratch_shapes=[pltpu.VMEM((D, tn), BF16)] * 2 + [pltpu.VMEM((SUBLANES, tn), F32)] * 2,
        compiler_params=_params(2),
        name="ffn_up",
    )(h, w_up, w_up, conv_w, conv_w)


def _ple_kernel(h_ref, wg_ref, p_ref, wp_ref, x_ref, g_ref, *out_refs, final):
    gate = jax.nn.sigmoid(_dot(h_ref[...], wg_ref[...]))
    x = x_ref[...] + gate * _dot(p_ref[...].astype(BF16), wp_ref[...])
    if not final:
        out_refs[0][...] = x
    out_refs[-1][...] = _rms(x, g_ref[...]).astype(out_refs[-1].dtype)


def _ple(h, w_gate, p, w_proj, x, g_next, layer, tm, final):
    T, D = x.shape
    P = p.shape[2]
    row = lambda n: pl.BlockSpec((tm, n), lambda i: (i, 0))
    if final:
        out_shape = [jax.ShapeDtypeStruct((T, D), F32)]
    else:
        out_shape = [jax.ShapeDtypeStruct((T, D), F32), jax.ShapeDtypeStruct((T, D), BF16)]
    return pl.pallas_call(
        functools.partial(_ple_kernel, final=final),
        out_shape=out_shape,
        grid=(T // tm,),
        in_specs=[row(D), _layer_weight(layer, D, D), pl.BlockSpec((None, tm, P), lambda i: (layer, i, 0)),
                  _layer_weight(layer, P, D), row(D), _resident((1, D))],
        out_specs=[row(D)] * len(out_shape),
        compiler_params=_params(1),
        name="ple_final" if final else "ple",
    )(h, w_gate, p, w_proj, x, g_next)


def kernel(x, p, positions, norm_mix_g, w_in, fox_forget_b, shortconv_w, sgu_norm_g, sgu_w, sgu_b, w_br_fox, w_br_conv, w_br_sgu, w_br_dil, w_out, norm_ffn_g, w_up, ffn_conv_w, w_down, norm_ple_g, w_ple_gate, w_ple_proj, final_norm_g):
    B, S, D = x.shape
    depth = w_in.shape[0]
    T = B * S
    d_ff = w_down.shape[1]

    tm = _tile(S, 512)
    tm_wide = _tile(S, 256)
    tq = _tile(S, 512)
    tn_merge = _tile(D, 512)
    tn_ffn = _tile(d_ff, 512)
    tiles_per_seq = S // tm

    sizes = (3 * FOX_WIDTH, FOX_HEADS, 3 * CONV_WIDTH, 2 * SGU_WIDTH, 3 * DIL_WIDTH, N_BRANCH * D)
    offs = [0]
    for n in sizes:
        offs.append(offs[-1] + n)

    bf = lambda w: w.astype(BF16)
    w_fox = bf(w_in[:, :, :offs[1] + LANES])
    b_f = jnp.pad(fox_forget_b.astype(F32), ((0, 0), (0, LANES - FOX_HEADS)))[:, None, :]
    pieces, seg_block, pos = [], {}, 0
    for name, seg, block in (("gate", 5, tn_merge), ("sgu", 3, sizes[3]), ("conv", 2, sizes[2]), ("dil", 4, sizes[4])):
        pad = -pos % block
        if pad:
            pieces.append(jnp.zeros((depth, D, pad), BF16))
        seg_block[name] = (pos + pad) // block
        pieces.append(bf(w_in[:, :, offs[seg]:offs[seg + 1]]))
        pos += pad + sizes[seg]
    w_mix = jnp.concatenate(pieces, axis=-1)
    w_brs = [bf(w_br_fox), bf(w_br_conv), bf(w_br_sgu), bf(w_br_dil)]
    w_out_b, w_down_b = bf(w_out), bf(w_down)
    w_pg_b, w_pp_b = bf(w_ple_gate), bf(w_ple_proj)
    sgu_bias = jnp.repeat(jnp.swapaxes(sgu_b, 1, 2), SGU_GROUP_DIM, axis=2)
    sgu_norm = sgu_norm_g[:, None, :]
    p_tok = p.reshape(depth, T, p.shape[-1])
    row_vec = lambda g: g[None, :]

    tables = _rope_tables(positions, tm)
    xf = x.reshape(T, D)
    h = _rmsnorm(xf, row_vec(norm_mix_g[0]), tm)
    out = None
    for i in range(depth):
        qa, ka, va = _fox_proj(h, w_fox, b_f, i, tm, tiles_per_seq)
        o_a = _fox_attention(qa.reshape(B, S, -1), ka.reshape(B, S, -1), va.reshape(B, S, -1),
                             B, S, tq).reshape(T, FOX_WIDTH)
        o_b = _conv_branch(h, w_mix, seg_block["conv"], shortconv_w, i, tm, tiles_per_seq)
        o_c = _sgu_branch(h, w_mix, seg_block["sgu"], sgu_norm, sgu_w, sgu_bias, i, tm)
        qkv = _dil_proj(h, w_mix, seg_block["dil"], tables, i, B, S, tm)
        outs, lses = [], []
        for g, (window, dil) in enumerate(DIL_PATTERNS):
            assert window // dil == DIL_SPAN
            o_g, l_g = _dil_attention(*qkv[3 * g:3 * g + 3], dil)
            outs.append(o_g)
            lses.append(l_g)
        o_d = _dil_merge(outs, lses, B, S, tm)

        merged = _gated_merge(h, (o_a, o_b, o_c, o_d), w_mix, seg_block["gate"], w_brs, i, tm, tn_merge)
        xf, h = _residual_proj(merged, w_out_b, xf, row_vec(norm_ffn_g[i]), i, tm_wide)

        act = _ffn_up(h, w_up, ffn_conv_w, i, tm, tn_ffn, tiles_per_seq)
        xf, h = _residual_proj(act, w_down_b, xf, row_vec(norm_ple_g[i]), i, tm_wide)

        final = i == depth - 1
        g_next = final_norm_g if final else norm_mix_g[i + 1]
        res = _ple(h, w_pg_b, p_tok, w_pp_b, xf, row_vec(g_next), i, tm_wide, final)
        if final:
            out = res[0]
        else:
            xf, h = res
    return out.reshape(B, S, D)
```

```python
import functools

import jax
import jax.numpy as jnp
from jax import lax
from jax.experimental import pallas as pl
from jax.experimental.pallas import tpu as pltpu

F32 = jnp.float32
BF16 = jnp.bfloat16

HEAD_DIM = 64
EPS = 1e-6
FOX_HEADS = 8
FOX_WIDTH = FOX_HEADS * HEAD_DIM
CONV_WIDTH = 512
CONV_TAPS = 3
SGU_GROUPS = 4
SGU_GROUP_DIM = 128
SGU_WIDTH = SGU_GROUPS * SGU_GROUP_DIM
SGU_CHUNK = 128
DIL_PATTERNS = ((128, 1), (512, 4), (2048, 16))
DIL_HEADS_PER_GROUP = 4
DIL_HEADS = DIL_HEADS_PER_GROUP * len(DIL_PATTERNS)
DIL_WIDTH = DIL_HEADS * HEAD_DIM
DIL_OUT = DIL_HEADS_PER_GROUP * HEAD_DIM
DIL_SPAN = 128
ROPE_THETA = 500000.0
ROPE_DIM = HEAD_DIM // 4
N_BRANCH = 4

LANES = 128
SUBLANES = 8
BF16_ROWS = 16
HEADS_PER_SLAB = 4
SLAB = HEADS_PER_SLAB * HEAD_DIM
VMEM_LIMIT = 56 * 1024 * 1024
NEG = -1e30
QK_SCALE = HEAD_DIM ** -0.5


def _params(n_axes):
    return pltpu.CompilerParams(dimension_semantics=("arbitrary",) * n_axes, vmem_limit_bytes=VMEM_LIMIT)


def _resident(shape, index=None):
    index = (0,) * len(shape) if index is None else index
    return pl.BlockSpec(shape, lambda *_: index, pipeline_mode=pl.Buffered(1))


def _layer_weight(layer, rows, cols):
    return _resident((None, rows, cols), (layer, 0, 0))


def _tile(n, pref):
    t = min(n, pref)
    assert n % t == 0, (n, t)
    return t


def _rms(x, g):
    var = jnp.mean(x * x, axis=-1, keepdims=True)
    return x * lax.rsqrt(var + EPS) * g


def _dot(a, b):
    return jnp.dot(a, b, preferred_element_type=F32)


def _dot_nt(a, b):
    return lax.dot_general(a, b, (((1,), (1,)), ((), ())), preferred_element_type=F32)


def _causal_conv3(z, w, prev):
    y = w[0:1] * pltpu.roll(z, 2, 0) + w[1:2] * pltpu.roll(z, 1, 0) + w[2:3] * z
    zh = z[0:BF16_ROWS]
    row = lax.broadcasted_iota(jnp.int32, zh.shape, 0)
    p1 = prev[SUBLANES - 1:SUBLANES]
    p2 = prev[SUBLANES - 2:SUBLANES - 1]
    z1 = jnp.where(row == 0, p1, pltpu.roll(zh, 1, 0))
    z2 = jnp.where(row == 0, p2, jnp.where(row == 1, p1, pltpu.roll(zh, 2, 0)))
    yh = w[0:1] * z2 + w[1:2] * z1 + w[2:3] * zh
    return jnp.concatenate([yh, y[BF16_ROWS:]], axis=0)


def _rmsnorm_kernel(x_ref, g_ref, h_ref):
    h_ref[...] = _rms(x_ref[...], g_ref[...]).astype(h_ref.dtype)


def _rmsnorm(x, g, tm):
    T, D = x.shape
    return pl.pallas_call(
        _rmsnorm_kernel,
        out_shape=jax.ShapeDtypeStruct((T, D), BF16),
        grid=(T // tm,),
        in_specs=[pl.BlockSpec((tm, D), lambda i: (i, 0)), _resident((1, D))],
        out_specs=pl.BlockSpec((tm, D), lambda i: (i, 0)),
        compiler_params=_params(1),
        name="rmsnorm",
    )(x, g)


def _rope_table_kernel(pos_ref, inv_ref, c_ref, s1_ref, s2_ref):
    ang = pos_ref[...] * inv_ref[...]
    cos = jnp.cos(ang)
    sin = jnp.sin(ang)
    lane = lax.broadcasted_iota(jnp.int32, ang.shape, 1) % HEAD_DIM
    half = ROPE_DIM // 2
    c_ref[...] = jnp.where(lane < ROPE_DIM, cos, 1.0)
    s1_ref[...] = jnp.where((lane >= half) & (lane < ROPE_DIM), sin, 0.0)
    s2_ref[...] = jnp.where(lane < half, -sin, 0.0)


def _rope_tables(positions, tm):
    T = positions.size
    half = ROPE_DIM // 2
    inv = ROPE_THETA ** (-jnp.arange(half, dtype=F32) * (2.0 / ROPE_DIM))
    lane = jnp.arange(LANES) % HEAD_DIM
    inv_lane = jnp.where(lane < ROPE_DIM, inv[lane % half], 0.0).astype(F32)[None, :]
    posf = jnp.broadcast_to(positions.reshape(T, 1).astype(F32), (T, LANES))
    spec = pl.BlockSpec((tm, LANES), lambda i: (i, 0))
    return pl.pallas_call(
        _rope_table_kernel,
        out_shape=[jax.ShapeDtypeStruct((T, LANES), F32)] * 3,
        grid=(T // tm,),
        in_specs=[spec, _resident((1, LANES))],
        out_specs=[spec] * 3,
        compiler_params=_params(1),
        name="rope_tables",
    )(posf, inv_lane)


def _repack_kernel(w_ref, fox_ref, mix_ref, *, fox_width, moves, mix_width):
    fox_ref[...] = w_ref[:, 0:fox_width].astype(BF16)
    end = 0
    for src, width, dst in moves:
        if dst > end:
            mix_ref[:, end:dst] = jnp.zeros((mix_ref.shape[0], dst - end), BF16)
        mix_ref[:, dst:dst + width] = w_ref[:, src:src + width].astype(BF16)
        end = dst + width
    assert end == mix_width


def _repack_w_in(w_in, fox_width, moves, mix_width):
    depth, D, d_in = w_in.shape
    tr = _tile(D, 128)
    return pl.pallas_call(
        functools.partial(_repack_kernel, fox_width=fox_width, moves=moves, mix_width=mix_width),
        out_shape=[jax.ShapeDtypeStruct((depth, D, fox_width), BF16),
                   jax.ShapeDtypeStruct((depth, D, mix_width), BF16)],
        grid=(depth, D // tr),
        in_specs=[pl.BlockSpec((None, tr, d_in), lambda l, r: (l, r, 0))],
        out_specs=[pl.BlockSpec((None, tr, fox_width), lambda l, r: (l, r, 0)),
                   pl.BlockSpec((None, tr, mix_width), lambda l, r: (l, r, 0))],
        compiler_params=_params(2),
        name="repack_w_in",
    )(w_in)


def _fox_proj_kernel(h_ref, w_ref, wf_ref, bf_ref, q_ref, k_ref, v_ref, carry_ref, *, tiles_per_seq):
    @pl.when(pl.program_id(0) % tiles_per_seq == 0)
    def _():
        carry_ref[...] = jnp.zeros_like(carry_ref)

    h = h_ref[...]
    W = FOX_WIDTH
    q_ref[...] = (_dot(h, w_ref[:, 0:W]) * QK_SCALE).astype(BF16)
    k = _dot(h, w_ref[:, W:2 * W]).astype(BF16)
    v = _dot(h, w_ref[:, 2 * W:3 * W]).astype(BF16)

    x = jax.nn.log_sigmoid(_dot(h, wf_ref[...]) + bf_ref[...])
    tm = x.shape[0]
    row = lax.broadcasted_iota(jnp.int32, x.shape, 0)
    lane = lax.broadcasted_iota(jnp.int32, x.shape, 1)
    s = 1
    while s < tm:
        x = x + jnp.where(row >= s, pltpu.roll(x, s, 0), 0.0)
        s *= 2
    x = x + carry_ref[0:1, :]
    carry_ref[...] = jnp.broadcast_to(x[tm - 1:tm, :], carry_ref.shape)

    x = jnp.where(lane < FOX_HEADS, x, 0.0)
    hi = x.astype(BF16).astype(F32)
    mid = (x - hi).astype(BF16).astype(F32)
    lo = (x - hi - mid).astype(BF16).astype(F32)
    f_lanes = (hi + pltpu.roll(mid, FOX_HEADS, 1) + pltpu.roll(lo, 2 * FOX_HEADS, 1)).astype(BF16)
    one_lane = jnp.where(lane == 0, 1.0, 0.0).astype(BF16)
    for j in range(FOX_WIDTH // LANES):
        k_ref[:, j * SLAB:j * SLAB + LANES] = k[:, j * LANES:(j + 1) * LANES]
        k_ref[:, j * SLAB + LANES:(j + 1) * SLAB] = f_lanes
        v_ref[:, j * SLAB:j * SLAB + LANES] = v[:, j * LANES:(j + 1) * LANES]
        v_ref[:, j * SLAB + LANES:(j + 1) * SLAB] = one_lane


def _fox_proj(h, w_fox, b_f, layer, tm, tiles_per_seq):
    T, D = h.shape
    W = FOX_WIDTH
    n_pairs = FOX_WIDTH // LANES
    row = lambda n: pl.BlockSpec((tm, n), lambda i: (i, 0))
    return pl.pallas_call(
        functools.partial(_fox_proj_kernel, tiles_per_seq=tiles_per_seq),
        out_shape=[jax.ShapeDtypeStruct((T, W), BF16)] + [jax.ShapeDtypeStruct((T, n_pairs * SLAB), BF16)] * 2,
        grid=(T // tm,),
        in_specs=[row(D), _resident((None, D, 3 * W), (layer, 0, 0)),
                  _resident((None, D, LANES), (layer, 0, 3 * W // LANES)), _layer_weight(layer, 1, LANES)],
        out_specs=[row(W), row(n_pairs * SLAB), row(n_pairs * SLAB)],
        scratch_shapes=[pltpu.VMEM((SUBLANES, LANES), F32)],
        compiler_params=_params(1),
        name="fox_proj",
    )(h, w_fox, w_fox, b_f)


def _conv_branch_kernel(h_ref, w_ref, cw_ref, o_ref, prev_ref, *, tiles_per_seq):
    @pl.when(pl.program_id(0) % tiles_per_seq == 0)
    def _():
        prev_ref[...] = jnp.zeros_like(prev_ref)

    h = h_ref[...]
    W = CONV_WIDTH
    tm = h.shape[0]
    xb = _dot(h, w_ref[:, 0:W])
    gate_b = _dot(h, w_ref[:, W:2 * W])
    gate_c = _dot(h, w_ref[:, 2 * W:3 * W])
    z = gate_c * xb
    y = _causal_conv3(z, cw_ref[...], prev_ref[...])
    prev_ref[...] = z[tm - SUBLANES:, :]
    o_ref[...] = (gate_b * y).astype(BF16)


def _conv_branch(h, w_mix, col_block, conv_w, layer, tm, tiles_per_seq):
    T, D = h.shape
    W = CONV_WIDTH
    return pl.pallas_call(
        functools.partial(_conv_branch_kernel, tiles_per_seq=tiles_per_seq),
        out_shape=jax.ShapeDtypeStruct((T, W), BF16),
        grid=(T // tm,),
        in_specs=[pl.BlockSpec((tm, D), lambda i: (i, 0)), _resident((None, D, 3 * W), (layer, 0, col_block)),
                  _layer_weight(layer, CONV_TAPS, W)],
        out_specs=pl.BlockSpec((tm, W), lambda i: (i, 0)),
        scratch_shapes=[pltpu.VMEM((SUBLANES, W), F32)],
        compiler_params=_params(1),
        name="conv_branch",
    )(h, w_mix, conv_w)


def _sgu_branch_kernel(h_ref, w_ref, ng_ref, ws_ref, bias_ref, o_ref):
    h = h_ref[...]
    W = SGU_WIDTH
    C = SGU_CHUNK
    G = SGU_GROUP_DIM
    u = jax.nn.gelu(_dot(h, w_ref[:, 0:W]))
    v = _rms(jax.nn.gelu(_dot(h, w_ref[:, W:2 * W])), ng_ref[...]).astype(BF16)
    row = lax.broadcasted_iota(jnp.int32, (C, C), 0)
    col = lax.broadcasted_iota(jnp.int32, (C, C), 1)
    for g in range(SGU_GROUPS):
        w_tril = jnp.where(col <= row, ws_ref[g], 0.0).astype(BF16)
        cols = slice(g * G, (g + 1) * G)
        for c in range(h.shape[0] // C):
            rows = slice(c * C, (c + 1) * C)
            mixed = _dot(w_tril, v[rows, cols]) + bias_ref[:, cols]
            o_ref[rows, cols] = (u[rows, cols] * mixed).astype(BF16)


def _sgu_branch(h, w_mix, col_block, norm_g, w_s, bias_full, layer, tm):
    T, D = h.shape
    W = SGU_WIDTH
    return pl.pallas_call(
        _sgu_branch_kernel,
        out_shape=jax.ShapeDtypeStruct((T, W), BF16),
        grid=(T // tm,),
        in_specs=[pl.BlockSpec((tm, D), lambda i: (i, 0)), _resident((None, D, 2 * W), (layer, 0, col_block)),
                  _layer_weight(layer, 1, W),
                  _resident((None, SGU_GROUPS, SGU_CHUNK, SGU_CHUNK), (layer, 0, 0, 0)),
                  _layer_weight(layer, SGU_CHUNK, W)],
        out_specs=pl.BlockSpec((tm, W), lambda i: (i, 0)),
        compiler_params=_params(1),
        name="sgu_branch",
    )(h, w_mix, norm_g, w_s, bias_full)


def _dil_proj_kernel(h_ref, w_ref, c_ref, s1_ref, s2_ref, *refs):
    n_groups = len(DIL_PATTERNS)
    out_refs = refs[:3 * n_groups]
    stage_ref = refs[3 * n_groups]
    h = h_ref[...]
    tm = h.shape[0]
    cos = c_ref[...]
    s1 = s1_ref[...]
    s2 = s2_ref[...]
    shift = ROPE_DIM // 2
    slot = 0
    for which, rope, scale in ((0, True, QK_SCALE), (1, True, None), (2, False, None)):
        for g, (_, d) in enumerate(DIL_PATTERNS):
            col = which * DIL_WIDTH + g * SLAB
            x = _dot(h, w_ref[:, col:col + SLAB])
            parts = []
            for j in range(SLAB // LANES):
                xj = x[:, j * LANES:(j + 1) * LANES]
                if rope:
                    xj = xj * cos + pltpu.roll(xj, shift, 1) * s1 + pltpu.roll(xj, LANES - shift, 1) * s2
                if scale is not None:
                    xj = xj * scale
                parts.append(xj)
            out = out_refs[g * 3 + which]
            for j, xj in enumerate(parts):
                lanes = slice(j * LANES, (j + 1) * LANES)
                if d == 1:
                    out[0, 0, :, lanes] = xj.astype(BF16)
                else:
                    stage_ref[slot] = xj
                    for r in range(d):
                        out[0, r, :, lanes] = stage_ref[slot, pl.ds(r, tm // d, stride=d), :].astype(BF16)
                    slot += 1


def _dil_proj(h, w_mix, col_block, tables, layer, B, S, tm):
    T, D = h.shape
    tiles_per_seq = S // tm
    row = lambda n: pl.BlockSpec((tm, n), lambda i: (i, 0))
    out_shape, out_specs = [], []
    n_strided = 0
    for _, d in DIL_PATTERNS:
        assert (tm // d) % BF16_ROWS == 0
        n_strided += 3 * (SLAB // LANES) * (d > 1)
        for _ in range(3):
            out_shape.append(jax.ShapeDtypeStruct((B, d, S // d, SLAB), BF16))
            out_specs.append(pl.BlockSpec((1, d, tm // d, SLAB),
                                          lambda i: (i // tiles_per_seq, 0, i % tiles_per_seq, 0)))
    return pl.pallas_call(
        _dil_proj_kernel,
        out_shape=out_shape,
        grid=(T // tm,),
        in_specs=[row(D), _resident((None, D, 3 * DIL_WIDTH), (layer, 0, col_block)), row(LANES), row(LANES),
                  row(LANES)],
        out_specs=out_specs,
        scratch_shapes=[pltpu.VMEM((n_strided, tm, LANES), F32)],
        compiler_params=_params(1),
        name="dil_proj",
    )(h, w_mix, *tables)


def _slab_head_masks(n_rows):
    lane = lax.broadcasted_iota(jnp.int32, (n_rows, SLAB), 1)
    return [(lane >= h * HEAD_DIM) & (lane < (h + 1) * HEAD_DIM) for h in range(HEADS_PER_SLAB)]


FOX_PAIRS_PER_STEP = 2


def _fox_attn_kernel(q_ref, k_ref, v_ref, o_ref, *, tq):
    group = pl.program_id(1)
    i = pl.program_id(2)
    q0 = pl.multiple_of(i * tq, tq)
    lane = lax.broadcasted_iota(jnp.int32, (tq, LANES), 1)
    rowi = lax.broadcasted_iota(jnp.int32, (2 * tq, tq), 0) % tq
    coli = lax.broadcasted_iota(jnp.int32, (2 * tq, tq), 1)

    def stacked_queries(c):
        q = q_ref[0, :, c * LANES:(c + 1) * LANES]
        zero = jnp.zeros_like(q)
        halves = []
        for e in range(2):
            head = 2 * (FOX_PAIRS_PER_STEP * group + c) + e
            f_pick = (lane == head) | (lane == head + FOX_HEADS) | (lane == head + 2 * FOX_HEADS)
            halves.append(jnp.concatenate(
                [jnp.where(lane // HEAD_DIM == e, q, zero), jnp.where(f_pick, -1.0, 0.0).astype(BF16)], axis=1))
        return jnp.concatenate(halves, axis=0)

    qs = [stacked_queries(c) for c in range(FOX_PAIRS_PER_STEP)]

    def block(k0, carry, diagonal):
        slabs = [slice(c * SLAB, (c + 1) * SLAB) for c in range(FOX_PAIRS_PER_STEP)]
        logits = [_dot_nt(qs[c], k_ref[0, pl.ds(k0, tq), slabs[c]]) for c in range(FOX_PAIRS_PER_STEP)]
        stats = []
        for (m, _), s in zip(carry, logits):
            if diagonal:
                s = jnp.where(coli <= rowi, s, NEG)
            m_new = jnp.maximum(m, jnp.max(s, axis=-1, keepdims=True))
            stats.append((m_new, jnp.exp(m - m_new), jnp.exp(s - m_new).astype(BF16)))
        return tuple((m_new, alpha * acc + _dot(p, v_ref[0, pl.ds(k0, tq), slabs[c]]))
                     for c, ((_, acc), (m_new, alpha, p)) in enumerate(zip(carry, stats)))

    init = tuple((jnp.full((2 * tq, 1), NEG, F32), jnp.zeros((2 * tq, SLAB), F32))
                 for _ in range(FOX_PAIRS_PER_STEP))
    carry = lax.fori_loop(0, i, lambda kb, c: block(pl.multiple_of(kb * tq, tq), c, False), init)
    for c, (_, acc) in enumerate(block(q0, carry, True)):
        out = acc[:, 0:LANES] / acc[:, LANES:LANES + 1]
        o_ref[0, :, c * LANES:(c + 1) * LANES] = jnp.where(lane // HEAD_DIM == 0, out[0:tq],
                                                           out[tq:2 * tq]).astype(BF16)


def _fox_attention(q, k_slabs, v_slabs, B, S, tq):
    n_groups = FOX_WIDTH // LANES // FOX_PAIRS_PER_STEP
    kv_spec = pl.BlockSpec((1, S, FOX_PAIRS_PER_STEP * SLAB), lambda b, g, i: (b, 0, g))
    q_spec = pl.BlockSpec((1, tq, FOX_PAIRS_PER_STEP * LANES), lambda b, g, i: (b, i, g))
    return pl.pallas_call(
        functools.partial(_fox_attn_kernel, tq=tq),
        out_shape=jax.ShapeDtypeStruct((B, S, FOX_WIDTH), BF16),
        grid=(B, n_groups, S // tq),
        in_specs=[q_spec, kv_spec, kv_spec],
        out_specs=q_spec,
        compiler_params=_params(3),
        name="fox_attention",
    )(q, k_slabs, v_slabs)


def _dil_attn_kernel(q_ref, k_ref, v_ref, o_ref, lse_ref, *, n_blocks):
    P = DIL_SPAN
    H = HEADS_PER_SLAB
    masks = _slab_head_masks(P)

    def attend(q_rows, k_rows, valid):
        q = q_ref[q_rows, :]
        zero = jnp.zeros_like(q)
        qs = jnp.concatenate([jnp.where(masks[h], q, zero) for h in range(H)], axis=0)
        s = jnp.where(valid, _dot_nt(qs, k_ref[k_rows, :]), NEG)
        m = jnp.max(s, axis=-1, keepdims=True)
        p = jnp.exp(s - m)
        l = jnp.sum(p, axis=-1, keepdims=True)
        pv = _dot(p.astype(BF16), v_ref[k_rows, :]) / l
        lse = jnp.broadcast_to(m + jnp.log(l), (H * P, SLAB))
        o, ls = pv[0:P], lse[0:P]
        for h in range(1, H):
            o = jnp.where(masks[h], pv[h * P:(h + 1) * P], o)
            ls = jnp.where(masks[h], lse[h * P:(h + 1) * P], ls)
        o_ref[q_rows, :] = o
        lse_ref[q_rows, :] = ls

    qi = lax.broadcasted_iota(jnp.int32, (H * P, P), 0) % P
    ki = lax.broadcasted_iota(jnp.int32, (H * P, P), 1)
    attend(pl.ds(0, P), pl.ds(0, P), ki <= qi)

    qi2 = lax.broadcasted_iota(jnp.int32, (H * P, 2 * P), 0) % P
    ki2 = lax.broadcasted_iota(jnp.int32, (H * P, 2 * P), 1)
    valid2 = (ki2 >= qi2) & (ki2 <= qi2 + P)

    def body(n, _):
        start = pl.multiple_of(n * P, P)
        attend(pl.ds(start, P), pl.ds(pl.multiple_of(start - P, P), 2 * P), valid2)
        return 0

    lax.fori_loop(1, n_blocks, body, 0, unroll=True if n_blocks <= 8 else 2)


def _dil_attention(q, k, v, dilation):
    B, d, L, _ = q.shape
    assert d == dilation
    spec = pl.BlockSpec((None, None, L, SLAB), lambda b, r: (b, r, 0, 0))
    return pl.pallas_call(
        functools.partial(_dil_attn_kernel, n_blocks=L // DIL_SPAN),
        out_shape=[jax.ShapeDtypeStruct((B, d, L, SLAB), F32)] * 2,
        grid=(B, d),
        in_specs=[spec] * 3,
        out_specs=[spec] * 2,
        compiler_params=_params(2),
        name=f"dil_attention_d{dilation}",
    )(q, k, v)


def _dil_merge_kernel(*refs):
    n_groups = len(DIL_PATTERNS)
    o_refs, l_refs = refs[:n_groups], refs[n_groups:2 * n_groups]
    out_ref, stage_ref = refs[2 * n_groups], refs[2 * n_groups + 1]
    tm = out_ref.shape[0]
    slot = 0
    outs, lses = [], []
    for src, dst in ((o_refs, outs), (l_refs, lses)):
        for (_, d), ref in zip(DIL_PATTERNS, src):
            if d == 1:
                dst.append(ref[0, 0])
                continue
            halves = []
            for j in range(SLAB // LANES):
                for r in range(d):
                    stage_ref[slot, pl.ds(r, tm // d, stride=d), :] = ref[0, r, :, j * LANES:(j + 1) * LANES]
                halves.append(stage_ref[slot])
                slot += 1
            dst.append(jnp.concatenate(halves, axis=1))
    m = functools.reduce(jnp.maximum, lses)
    es = [jnp.exp(l - m) for l in lses]
    num = es[0] * outs[0] + es[1] * outs[1] + es[2] * outs[2]
    out_ref[...] = (num / (es[0] + es[1] + es[2])).astype(BF16)


def _dil_merge(outs, lses, B, S, tm):
    T = B * S
    tiles_per_seq = S // tm
    specs = [pl.BlockSpec((1, d, tm // d, SLAB), lambda i: (i // tiles_per_seq, 0, i % tiles_per_seq, 0))
             for _, d in DIL_PATTERNS]
    n_strided = 2 * (SLAB // LANES) * sum(d > 1 for _, d in DIL_PATTERNS)
    return pl.pallas_call(
        _dil_merge_kernel,
        out_shape=jax.ShapeDtypeStruct((T, SLAB), BF16),
        grid=(T // tm,),
        in_specs=specs * 2,
        out_specs=pl.BlockSpec((tm, SLAB), lambda i: (i, 0)),
        scratch_shapes=[pltpu.VMEM((n_strided, tm, LANES), F32)],
        compiler_params=_params(1),
        name="dil_merge",
    )(*outs, *lses)


def _gated_merge_kernel(h_ref, oa_ref, ob_ref, oc_ref, od_ref, g0, g1, g2, g3, wa, wb, wc, wd, out_ref):
    h = h_ref[...]
    acc = None
    for o_ref, g_ref, w_ref in ((oa_ref, g0, wa), (ob_ref, g1, wb), (oc_ref, g2, wc), (od_ref, g3, wd)):
        term = jax.nn.sigmoid(_dot(h, g_ref[...])) * _dot(o_ref[...], w_ref[...])
        acc = term if acc is None else acc + term
    out_ref[...] = acc.astype(BF16)


def _gated_merge(h, branches, w_mix, gate_block, w_branches, layer, tm, tn):
    T, D = h.shape
    n_col = D // tn
    row = lambda n: pl.BlockSpec((tm, n), lambda j, i: (i, 0))
    gate_specs = [pl.BlockSpec((None, D, tn), functools.partial(lambda j, i, br: (layer, 0, gate_block + br * n_col + j), br=br))
                  for br in range(N_BRANCH)]
    w_specs = [pl.BlockSpec((None, w.shape[1], tn), lambda j, i: (layer, 0, j)) for w in w_branches]
    return pl.pallas_call(
        _gated_merge_kernel,
        out_shape=jax.ShapeDtypeStruct((T, D), BF16),
        grid=(n_col, T // tm),
        in_specs=[row(D)] + [row(o.shape[1]) for o in branches] + gate_specs + w_specs,
        out_specs=pl.BlockSpec((tm, tn), lambda j, i: (i, j)),
        compiler_params=_params(2),
        name="gated_merge",
    )(h, *branches, w_mix, w_mix, w_mix, w_mix, *w_branches)


def _residual_proj_kernel(a_ref, w_ref, x_ref, g_ref, xo_ref, h_ref):
    x = x_ref[...] + _dot(a_ref[...], w_ref[...])
    xo_ref[...] = x
    h_ref[...] = _rms(x, g_ref[...]).astype(h_ref.dtype)


def _residual_proj(a, w, x, g_next, layer, tm):
    T, D = x.shape
    K = a.shape[1]
    row = lambda n: pl.BlockSpec((tm, n), lambda i: (i, 0))
    return pl.pallas_call(
        _residual_proj_kernel,
        out_shape=[jax.ShapeDtypeStruct((T, D), F32), jax.ShapeDtypeStruct((T, D), BF16)],
        grid=(T // tm,),
        in_specs=[row(K), _layer_weight(layer, K, D), row(D), _resident((1, D))],
        out_specs=[row(D), row(D)],
        compiler_params=_params(1),
        name=f"residual_proj_k{K}",
    )(a, w, x, g_next)


FFN_CHUNK = SLAB


def _ffn_up_kernel(h_ref, wg_ref, wv_ref, cg_ref, cv_ref, act_ref, wg_bf, wv_bf, prev_g, prev_v, *, tiles_per_seq):
    i = pl.program_id(1)

    @pl.when(i == 0)
    def _():
        wg_bf[...] = wg_ref[...].astype(BF16)
        wv_bf[...] = wv_ref[...].astype(BF16)

    @pl.when(i % tiles_per_seq == 0)
    def _():
        prev_g[...] = jnp.zeros_like(prev_g)
        prev_v[...] = jnp.zeros_like(prev_v)

    h = h_ref[...]
    tm = h.shape[0]
    chunks = [slice(c * FFN_CHUNK, (c + 1) * FFN_CHUNK) for c in range(act_ref.shape[1] // FFN_CHUNK)]
    pre = [(_dot(h, wg_bf[:, cols]), _dot(h, wv_bf[:, cols])) for cols in chunks]
    for cols, (zg, zv) in zip(chunks, pre):
        up_gate = _causal_conv3(zg, cg_ref[:, cols], prev_g[:, cols])
        up_val = _causal_conv3(zv, cv_ref[:, cols], prev_v[:, cols])
        prev_g[:, cols] = zg[tm - SUBLANES:, :]
        prev_v[:, cols] = zv[tm - SUBLANES:, :]
        act_ref[:, cols] = (jax.nn.silu(up_gate) * up_val).astype(BF16)


def _ffn_up(h, w_up, conv_w, layer, tm, tn, tiles_per_seq):
    T, D = h.shape
    d_ff = w_up.shape[2] // 2
    n_col = d_ff // tn
    return pl.pallas_call(
        functools.partial(_ffn_up_kernel, tiles_per_seq=tiles_per_seq),
        out_shape=jax.ShapeDtypeStruct((T, d_ff), BF16),
        grid=(n_col, T // tm),
        in_specs=[pl.BlockSpec((tm, D), lambda j, i: (i, 0)),
                  pl.BlockSpec((None, D, tn), lambda j, i: (layer, 0, j)),
                  pl.BlockSpec((None, D, tn), lambda j, i: (layer, 0, n_col + j)),
                  pl.BlockSpec((None, CONV_TAPS, tn), lambda j, i: (layer, 0, j)),
                  pl.BlockSpec((None, CONV_TAPS, tn), lambda j, i: (layer, 0, n_col + j))],
        out_specs=pl.BlockSpec((tm, tn), lambda j, i: (i, j)),
        scratch_shapes=[pltpu.VMEM((D, tn), BF16)] * 2 + [pltpu.VMEM((SUBLANES, tn), F32)] * 2,
        compiler_params=_params(2),
        name="ffn_up",
    )(h, w_up, w_up, conv_w, conv_w)


def _ple_kernel(h_ref, wg_ref, p_ref, wp_ref, x_ref, g_ref, *out_refs, final):
    gate = jax.nn.sigmoid(_dot(h_ref[...], wg_ref[...]))
    x = x_ref[...] + gate * _dot(p_ref[...].astype(BF16), wp_ref[...])
    if not final:
        out_refs[0][...] = x
    out_refs[-1][...] = _rms(x, g_ref[...]).astype(out_refs[-1].dtype)


def _ple(h, w_gate, p, w_proj, x, g_next, layer, tm, final):
    T, D = x.shape
    P = p.shape[2]
    row = lambda n: pl.BlockSpec((tm, n), lambda i: (i, 0))
    if final:
        out_shape = [jax.ShapeDtypeStruct((T, D), F32)]
    else:
        out_shape = [jax.ShapeDtypeStruct((T, D), F32), jax.ShapeDtypeStruct((T, D), BF16)]
    return pl.pallas_call(
        functools.partial(_ple_kernel, final=final),
        out_shape=out_shape,
        grid=(T // tm,),
        in_specs=[row(D), _layer_weight(layer, D, D), pl.BlockSpec((None, tm, P), lambda i: (layer, i, 0)),
                  _layer_weight(layer, P, D), row(D), _resident((1, D))],
        out_specs=[row(D)] * len(out_shape),
        compiler_params=_params(1),
        name="ple_final" if final else "ple",
    )(h, w_gate, p, w_proj, x, g_next)


def kernel(x, p, positions, norm_mix_g, w_in, fox_forget_b, shortconv_w, sgu_norm_g, sgu_w, sgu_b, w_br_fox, w_br_conv, w_br_sgu, w_br_dil, w_out, norm_ffn_g, w_up, ffn_conv_w, w_down, norm_ple_g, w_ple_gate, w_ple_proj, final_norm_g):
    B, S, D = x.shape
    depth = w_in.shape[0]
    T = B * S
    d_ff = w_down.shape[1]

    tm = _tile(S, 512)
    tm_big = _tile(S, 1024)
    tm_wide = _tile(S, 512)
    tm_down = _tile(S, 256)
    tq = _tile(S, 512)
    tn_merge = _tile(D, 512)
    tn_ffn = _tile(d_ff, 512)
    tiles_per_seq = S // tm

    sizes = (3 * FOX_WIDTH, FOX_HEADS, 3 * CONV_WIDTH, 2 * SGU_WIDTH, 3 * DIL_WIDTH, N_BRANCH * D)
    offs = [0]
    for n in sizes:
        offs.append(offs[-1] + n)

    bf = lambda w: w.astype(BF16)
    b_f = jnp.pad(fox_forget_b.astype(F32), ((0, 0), (0, LANES - FOX_HEADS)))[:, None, :]
    moves, seg_block, pos = [], {}, 0
    for name, seg, block in (("gate", 5, tn_merge), ("sgu", 3, sizes[3]), ("conv", 2, sizes[2]), ("dil", 4, sizes[4])):
        pos += -pos % block
        seg_block[name] = pos // block
        moves.append((offs[seg], sizes[seg], pos))
        pos += sizes[seg]
    w_fox, w_mix = _repack_w_in(w_in, offs[1] + LANES, tuple(moves), pos)
    w_brs = [bf(w_br_fox), bf(w_br_conv), bf(w_br_sgu), bf(w_br_dil)]
    w_out_b, w_down_b = bf(w_out), bf(w_down)
    w_pg_b, w_pp_b = bf(w_ple_gate), bf(w_ple_proj)
    sgu_bias = jnp.repeat(jnp.swapaxes(sgu_b, 1, 2), SGU_GROUP_DIM, axis=2)
    sgu_norm = sgu_norm_g[:, None, :]
    p_tok = p.reshape(depth, T, p.shape[-1])
    row_vec = lambda g: g[None, :]

    tables = _rope_tables(positions, tm)
    xf = x.reshape(T, D)
    h = _rmsnorm(xf, row_vec(norm_mix_g[0]), tm)
    out = None
    for i in range(depth):
        qa, ka, va = _fox_proj(h, w_fox, b_f, i, tm, tiles_per_seq)
        o_a = _fox_attention(qa.reshape(B, S, -1), ka.reshape(B, S, -1), va.reshape(B, S, -1),
                             B, S, tq).reshape(T, FOX_WIDTH)
        o_b = _conv_branch(h, w_mix, seg_block["conv"], shortconv_w, i, tm, tiles_per_seq)
        o_c = _sgu_branch(h, w_mix, seg_block["sgu"], sgu_norm, sgu_w, sgu_bias, i, tm)
        qkv = _dil_proj(h, w_mix, seg_block["dil"], tables, i, B, S, tm)
        outs, lses = [], []
        for g, (window, dil) in enumerate(DIL_PATTERNS):
            assert window // dil == DIL_SPAN
            o_g, l_g = _dil_attention(*qkv[3 * g:3 * g + 3], dil)
            outs.append(o_g)
            lses.append(l_g)
        o_d = _dil_merge(outs, lses, B, S, tm)

        merged = _gated_merge(h, (o_a, o_b, o_c, o_d), w_mix, seg_block["gate"], w_brs, i, tm_big, tn_merge)
        xf, h = _residual_proj(merged, w_out_b, xf, row_vec(norm_ffn_g[i]), i, tm_wide)

        act = _ffn_up(h, w_up, ffn_conv_w, i, tm_big, tn_ffn, S // tm_big)
        xf, h = _residual_proj(act, w_down_b, xf, row_vec(norm_ple_g[i]), i, tm_down)

        final = i == depth - 1
        g_next = final_norm_g if final else norm_mix_g[i + 1]
        res = _ple(h, w_pg_b, p_tok, w_pp_b, xf, row_vec(g_next), i, tm_wide, final)
        if final:
            out = res[0]
        else:
            xf, h = res
    return out.reshape(B, S, D)
```

```python
import functools

import jax
import jax.numpy as jnp
from jax import lax
from jax.experimental import pallas as pl
from jax.experimental.pallas import tpu as pltpu

F32 = jnp.float32
BF16 = jnp.bfloat16

HEAD_DIM = 64
EPS = 1e-6
FOX_HEADS = 8
FOX_WIDTH = FOX_HEADS * HEAD_DIM
CONV_WIDTH = 512
CONV_TAPS = 3
SGU_GROUPS = 4
SGU_GROUP_DIM = 128
SGU_WIDTH = SGU_GROUPS * SGU_GROUP_DIM
SGU_CHUNK = 128
DIL_PATTERNS = ((128, 1), (512, 4), (2048, 16))
DIL_HEADS_PER_GROUP = 4
DIL_HEADS = DIL_HEADS_PER_GROUP * len(DIL_PATTERNS)
DIL_WIDTH = DIL_HEADS * HEAD_DIM
DIL_OUT = DIL_HEADS_PER_GROUP * HEAD_DIM
DIL_SPAN = 128
ROPE_THETA = 500000.0
ROPE_DIM = HEAD_DIM // 4
N_BRANCH = 4

LANES = 128
SUBLANES = 8
BF16_ROWS = 16
HEADS_PER_SLAB = 4
SLAB = HEADS_PER_SLAB * HEAD_DIM
VMEM_LIMIT = 56 * 1024 * 1024
NEG = -1e30
QK_SCALE = HEAD_DIM ** -0.5


def _params(n_axes):
    return pltpu.CompilerParams(dimension_semantics=("arbitrary",) * n_axes, vmem_limit_bytes=VMEM_LIMIT)


def _resident(shape, index=None):
    index = (0,) * len(shape) if index is None else index
    return pl.BlockSpec(shape, lambda *_: index, pipeline_mode=pl.Buffered(1))


def _layer_weight(layer, rows, cols):
    return _resident((None, rows, cols), (layer, 0, 0))


def _tile(n, pref):
    t = min(n, pref)
    assert n % t == 0, (n, t)
    return t


def _rms(x, g):
    var = jnp.mean(x * x, axis=-1, keepdims=True)
    return x * lax.rsqrt(var + EPS) * g


def _dot(a, b):
    return jnp.dot(a, b, preferred_element_type=F32)


def _dot_nt(a, b):
    return lax.dot_general(a, b, (((1,), (1,)), ((), ())), preferred_element_type=F32)


def _causal_conv3(z, w, prev):
    y = w[0:1] * pltpu.roll(z, 2, 0) + w[1:2] * pltpu.roll(z, 1, 0) + w[2:3] * z
    zh = z[0:BF16_ROWS]
    row = lax.broadcasted_iota(jnp.int32, zh.shape, 0)
    p1 = prev[SUBLANES - 1:SUBLANES]
    p2 = prev[SUBLANES - 2:SUBLANES - 1]
    z1 = jnp.where(row == 0, p1, pltpu.roll(zh, 1, 0))
    z2 = jnp.where(row == 0, p2, jnp.where(row == 1, p1, pltpu.roll(zh, 2, 0)))
    yh = w[0:1] * z2 + w[1:2] * z1 + w[2:3] * zh
    return jnp.concatenate([yh, y[BF16_ROWS:]], axis=0)


def _rmsnorm_kernel(x_ref, g_ref, h_ref):
    h_ref[...] = _rms(x_ref[...], g_ref[...]).astype(h_ref.dtype)


def _rmsnorm(x, g, tm):
    T, D = x.shape
    return pl.pallas_call(
        _rmsnorm_kernel,
        out_shape=jax.ShapeDtypeStruct((T, D), BF16),
        grid=(T // tm,),
        in_specs=[pl.BlockSpec((tm, D), lambda i: (i, 0)), _resident((1, D))],
        out_specs=pl.BlockSpec((tm, D), lambda i: (i, 0)),
        compiler_params=_params(1),
        name="rmsnorm",
    )(x, g)


def _rope_table_kernel(pos_ref, inv_ref, c_ref, s1_ref, s2_ref):
    ang = pos_ref[...] * inv_ref[...]
    cos = jnp.cos(ang)
    sin = jnp.sin(ang)
    lane = lax.broadcasted_iota(jnp.int32, ang.shape, 1) % HEAD_DIM
    half = ROPE_DIM // 2
    c_ref[...] = jnp.where(lane < ROPE_DIM, cos, 1.0)
    s1_ref[...] = jnp.where((lane >= half) & (lane < ROPE_DIM), sin, 0.0)
    s2_ref[...] = jnp.where(lane < half, -sin, 0.0)


def _rope_tables(positions, tm):
    T = positions.size
    half = ROPE_DIM // 2
    inv = ROPE_THETA ** (-jnp.arange(half, dtype=F32) * (2.0 / ROPE_DIM))
    lane = jnp.arange(LANES) % HEAD_DIM
    inv_lane = jnp.where(lane < ROPE_DIM, inv[lane % half], 0.0).astype(F32)[None, :]
    posf = jnp.broadcast_to(positions.reshape(T, 1).astype(F32), (T, LANES))
    spec = pl.BlockSpec((tm, LANES), lambda i: (i, 0))
    return pl.pallas_call(
        _rope_table_kernel,
        out_shape=[jax.ShapeDtypeStruct((T, LANES), F32)] * 3,
        grid=(T // tm,),
        in_specs=[spec, _resident((1, LANES))],
        out_specs=[spec] * 3,
        compiler_params=_params(1),
        name="rope_tables",
    )(posf, inv_lane)


def _repack_kernel(wt_ref, fox_ref, mix_ref, *, fox_width, moves, mix_width):
    def rows_of(src, width):
        return wt_ref[src:src + width, :].T.astype(BF16)

    fox_ref[...] = rows_of(0, fox_width)
    end = 0
    for src, width, dst in moves:
        if dst > end:
            mix_ref[:, end:dst] = jnp.zeros((mix_ref.shape[0], dst - end), BF16)
        mix_ref[:, dst:dst + width] = rows_of(src, width)
        end = dst + width
    assert end == mix_width


def _repack_w_in(w_in, fox_width, moves, mix_width):
    depth, D, d_in = w_in.shape
    assert all(src % SUBLANES == 0 for src, _, _ in moves)
    tr = _tile(D, LANES)
    return pl.pallas_call(
        functools.partial(_repack_kernel, fox_width=fox_width, moves=moves, mix_width=mix_width),
        out_shape=[jax.ShapeDtypeStruct((depth, D, fox_width), BF16),
                   jax.ShapeDtypeStruct((depth, D, mix_width), BF16)],
        grid=(depth, D // tr),
        in_specs=[pl.BlockSpec((None, d_in, tr), lambda l, r: (l, 0, r))],
        out_specs=[pl.BlockSpec((None, tr, fox_width), lambda l, r: (l, r, 0)),
                   pl.BlockSpec((None, tr, mix_width), lambda l, r: (l, r, 0))],
        compiler_params=_params(2),
        name="repack_w_in",
    )(jnp.swapaxes(w_in, 1, 2))


def _fox_proj_kernel(h_ref, w_ref, wf_ref, bf_ref, q_ref, k_ref, v_ref, carry_ref, *, tiles_per_seq):
    @pl.when(pl.program_id(0) % tiles_per_seq == 0)
    def _():
        carry_ref[...] = jnp.zeros_like(carry_ref)

    h = h_ref[...]
    W = FOX_WIDTH
    q_ref[...] = (_dot(h, w_ref[:, 0:W]) * QK_SCALE).astype(BF16)
    k = _dot(h, w_ref[:, W:2 * W]).astype(BF16)
    v = _dot(h, w_ref[:, 2 * W:3 * W]).astype(BF16)

    x = jax.nn.log_sigmoid(_dot(h, wf_ref[...]) + bf_ref[...])
    tm = x.shape[0]
    row = lax.broadcasted_iota(jnp.int32, x.shape, 0)
    lane = lax.broadcasted_iota(jnp.int32, x.shape, 1)
    s = 1
    while s < tm:
        x = x + jnp.where(row >= s, pltpu.roll(x, s, 0), 0.0)
        s *= 2
    x = x + carry_ref[0:1, :]
    carry_ref[...] = jnp.broadcast_to(x[tm - 1:tm, :], carry_ref.shape)

    x = jnp.where(lane < FOX_HEADS, x, 0.0)
    hi = x.astype(BF16).astype(F32)
    mid = (x - hi).astype(BF16).astype(F32)
    lo = (x - hi - mid).astype(BF16).astype(F32)
    f_lanes = (hi + pltpu.roll(mid, FOX_HEADS, 1) + pltpu.roll(lo, 2 * FOX_HEADS, 1)).astype(BF16)
    one_lane = jnp.where(lane == 0, 1.0, 0.0).astype(BF16)
    for j in range(FOX_WIDTH // LANES):
        k_ref[:, j * SLAB:j * SLAB + LANES] = k[:, j * LANES:(j + 1) * LANES]
        k_ref[:, j * SLAB + LANES:(j + 1) * SLAB] = f_lanes
        v_ref[:, j * SLAB:j * SLAB + LANES] = v[:, j * LANES:(j + 1) * LANES]
        v_ref[:, j * SLAB + LANES:(j + 1) * SLAB] = one_lane


def _fox_proj(h, w_fox, b_f, layer, tm, tiles_per_seq):
    T, D = h.shape
    W = FOX_WIDTH
    n_pairs = FOX_WIDTH // LANES
    row = lambda n: pl.BlockSpec((tm, n), lambda i: (i, 0))
    return pl.pallas_call(
        functools.partial(_fox_proj_kernel, tiles_per_seq=tiles_per_seq),
        out_shape=[jax.ShapeDtypeStruct((T, W), BF16)] + [jax.ShapeDtypeStruct((T, n_pairs * SLAB), BF16)] * 2,
        grid=(T // tm,),
        in_specs=[row(D), _resident((None, D, 3 * W), (layer, 0, 0)),
                  _resident((None, D, LANES), (layer, 0, 3 * W // LANES)), _layer_weight(layer, 1, LANES)],
        out_specs=[row(W), row(n_pairs * SLAB), row(n_pairs * SLAB)],
        scratch_shapes=[pltpu.VMEM((SUBLANES, LANES), F32)],
        compiler_params=_params(1),
        name="fox_proj",
    )(h, w_fox, w_fox, b_f)


def _conv_branch_kernel(h_ref, w_ref, cw_ref, o_ref, prev_ref, *, tiles_per_seq):
    @pl.when(pl.program_id(0) % tiles_per_seq == 0)
    def _():
        prev_ref[...] = jnp.zeros_like(prev_ref)

    h = h_ref[...]
    W = CONV_WIDTH
    tm = h.shape[0]
    xb = _dot(h, w_ref[:, 0:W])
    gate_b = _dot(h, w_ref[:, W:2 * W])
    gate_c = _dot(h, w_ref[:, 2 * W:3 * W])
    z = gate_c * xb
    y = _causal_conv3(z, cw_ref[...], prev_ref[...])
    prev_ref[...] = z[tm - SUBLANES:, :]
    o_ref[...] = (gate_b * y).astype(BF16)


def _conv_branch(h, w_mix, col_block, conv_w, layer, tm, tiles_per_seq):
    T, D = h.shape
    W = CONV_WIDTH
    return pl.pallas_call(
        functools.partial(_conv_branch_kernel, tiles_per_seq=tiles_per_seq),
        out_shape=jax.ShapeDtypeStruct((T, W), BF16),
        grid=(T // tm,),
        in_specs=[pl.BlockSpec((tm, D), lambda i: (i, 0)), _resident((None, D, 3 * W), (layer, 0, col_block)),
                  _layer_weight(layer, CONV_TAPS, W)],
        out_specs=pl.BlockSpec((tm, W), lambda i: (i, 0)),
        scratch_shapes=[pltpu.VMEM((SUBLANES, W), F32)],
        compiler_params=_params(1),
        name="conv_branch",
    )(h, w_mix, conv_w)


def _sgu_branch_kernel(h_ref, w_ref, ng_ref, ws_ref, bias_ref, o_ref):
    h = h_ref[...]
    W = SGU_WIDTH
    C = SGU_CHUNK
    G = SGU_GROUP_DIM
    u = jax.nn.gelu(_dot(h, w_ref[:, 0:W]))
    v = _rms(jax.nn.gelu(_dot(h, w_ref[:, W:2 * W])), ng_ref[...]).astype(BF16)
    row = lax.broadcasted_iota(jnp.int32, (C, C), 0)
    col = lax.broadcasted_iota(jnp.int32, (C, C), 1)
    for g in range(SGU_GROUPS):
        w_tril = jnp.where(col <= row, ws_ref[g], 0.0).astype(BF16)
        cols = slice(g * G, (g + 1) * G)
        for c in range(h.shape[0] // C):
            rows = slice(c * C, (c + 1) * C)
            mixed = _dot(w_tril, v[rows, cols]) + bias_ref[:, cols]
            o_ref[rows, cols] = (u[rows, cols] * mixed).astype(BF16)


def _sgu_branch(h, w_mix, col_block, norm_g, w_s, bias_full, layer, tm):
    T, D = h.shape
    W = SGU_WIDTH
    return pl.pallas_call(
        _sgu_branch_kernel,
        out_shape=jax.ShapeDtypeStruct((T, W), BF16),
        grid=(T // tm,),
        in_specs=[pl.BlockSpec((tm, D), lambda i: (i, 0)), _resident((None, D, 2 * W), (layer, 0, col_block)),
                  _layer_weight(layer, 1, W),
                  _resident((None, SGU_GROUPS, SGU_CHUNK, SGU_CHUNK), (layer, 0, 0, 0)),
                  _layer_weight(layer, SGU_CHUNK, W)],
        out_specs=pl.BlockSpec((tm, W), lambda i: (i, 0)),
        compiler_params=_params(1),
        name="sgu_branch",
    )(h, w_mix, norm_g, w_s, bias_full)


def _dil_proj_kernel(h_ref, w_ref, c_ref, s1_ref, s2_ref, *refs):
    n_groups = len(DIL_PATTERNS)
    out_refs = refs[:3 * n_groups]
    stage_ref = refs[3 * n_groups]
    h = h_ref[...]
    tm = h.shape[0]
    cos = c_ref[...]
    s1 = s1_ref[...]
    s2 = s2_ref[...]
    shift = ROPE_DIM // 2
    slot = 0
    for which, rope, scale in ((0, True, QK_SCALE), (1, True, None), (2, False, None)):
        for g, (_, d) in enumerate(DIL_PATTERNS):
            col = which * DIL_WIDTH + g * SLAB
            x = _dot(h, w_ref[:, col:col + SLAB])
            parts = []
            for j in range(SLAB // LANES):
                xj = x[:, j * LANES:(j + 1) * LANES]
                if rope:
                    xj = xj * cos + pltpu.roll(xj, shift, 1) * s1 + pltpu.roll(xj, LANES - shift, 1) * s2
                if scale is not None:
                    xj = xj * scale
                parts.append(xj)
            out = out_refs[g * 3 + which]
            for j, xj in enumerate(parts):
                lanes = slice(j * LANES, (j + 1) * LANES)
                if d == 1:
                    out[0, 0, :, lanes] = xj.astype(BF16)
                else:
                    stage_ref[slot] = xj
                    for r in range(d):
                        out[0, r, :, lanes] = stage_ref[slot, pl.ds(r, tm // d, stride=d), :].astype(BF16)
                    slot += 1


def _dil_proj(h, w_mix, col_block, tables, layer, B, S, tm):
    T, D = h.shape
    tiles_per_seq = S // tm
    row = lambda n: pl.BlockSpec((tm, n), lambda i: (i, 0))
    out_shape, out_specs = [], []
    n_strided = 0
    for _, d in DIL_PATTERNS:
        assert (tm // d) % BF16_ROWS == 0
        n_strided += 3 * (SLAB // LANES) * (d > 1)
        for _ in range(3):
            out_shape.append(jax.ShapeDtypeStruct((B, d, S // d, SLAB), BF16))
            out_specs.append(pl.BlockSpec((1, d, tm // d, SLAB),
                                          lambda i: (i // tiles_per_seq, 0, i % tiles_per_seq, 0)))
    return pl.pallas_call(
        _dil_proj_kernel,
        out_shape=out_shape,
        grid=(T // tm,),
        in_specs=[row(D), _resident((None, D, 3 * DIL_WIDTH), (layer, 0, col_block)), row(LANES), row(LANES),
                  row(LANES)],
        out_specs=out_specs,
        scratch_shapes=[pltpu.VMEM((n_strided, tm, LANES), F32)],
        compiler_params=_params(1),
        name="dil_proj",
    )(h, w_mix, *tables)


def _slab_head_masks(n_rows):
    lane = lax.broadcasted_iota(jnp.int32, (n_rows, SLAB), 1)
    return [(lane >= h * HEAD_DIM) & (lane < (h + 1) * HEAD_DIM) for h in range(HEADS_PER_SLAB)]


FOX_PAIRS_PER_STEP = 2


def _fox_attn_kernel(q_ref, k_ref, v_ref, o_ref, *, tq):
    group = pl.program_id(1)
    i = pl.program_id(2)
    q0 = pl.multiple_of(i * tq, tq)
    lane = lax.broadcasted_iota(jnp.int32, (tq, LANES), 1)
    rowi = lax.broadcasted_iota(jnp.int32, (2 * tq, tq), 0) % tq
    coli = lax.broadcasted_iota(jnp.int32, (2 * tq, tq), 1)

    def stacked_queries(c):
        q = q_ref[0, :, c * LANES:(c + 1) * LANES]
        zero = jnp.zeros_like(q)
        halves = []
        for e in range(2):
            head = 2 * (FOX_PAIRS_PER_STEP * group + c) + e
            f_pick = (lane == head) | (lane == head + FOX_HEADS) | (lane == head + 2 * FOX_HEADS)
            halves.append(jnp.concatenate(
                [jnp.where(lane // HEAD_DIM == e, q, zero), jnp.where(f_pick, -1.0, 0.0).astype(BF16)], axis=1))
        return jnp.concatenate(halves, axis=0)

    qs = [stacked_queries(c) for c in range(FOX_PAIRS_PER_STEP)]

    def block(k0, carry, diagonal):
        slabs = [slice(c * SLAB, (c + 1) * SLAB) for c in range(FOX_PAIRS_PER_STEP)]
        logits = [_dot_nt(qs[c], k_ref[0, pl.ds(k0, tq), slabs[c]]) for c in range(FOX_PAIRS_PER_STEP)]
        stats = []
        for (m, _), s in zip(carry, logits):
            if diagonal:
                s = jnp.where(coli <= rowi, s, NEG)
            m_new = jnp.maximum(m, jnp.max(s, axis=-1, keepdims=True))
            stats.append((m_new, jnp.exp(m - m_new), jnp.exp(s - m_new).astype(BF16)))
        return tuple((m_new, alpha * acc + _dot(p, v_ref[0, pl.ds(k0, tq), slabs[c]]))
                     for c, ((_, acc), (m_new, alpha, p)) in enumerate(zip(carry, stats)))

    init = tuple((jnp.full((2 * tq, 1), NEG, F32), jnp.zeros((2 * tq, SLAB), F32))
                 for _ in range(FOX_PAIRS_PER_STEP))
    carry = lax.fori_loop(0, i, lambda kb, c: block(pl.multiple_of(kb * tq, tq), c, False), init)
    for c, (_, acc) in enumerate(block(q0, carry, True)):
        out = acc[:, 0:LANES] / acc[:, LANES:LANES + 1]
        o_ref[0, :, c * LANES:(c + 1) * LANES] = jnp.where(lane // HEAD_DIM == 0, out[0:tq],
                                                           out[tq:2 * tq]).astype(BF16)


def _fox_attention(q, k_slabs, v_slabs, B, S, tq):
    n_groups = FOX_WIDTH // LANES // FOX_PAIRS_PER_STEP
    kv_spec = pl.BlockSpec((1, S, FOX_PAIRS_PER_STEP * SLAB), lambda b, g, i: (b, 0, g))
    q_spec = pl.BlockSpec((1, tq, FOX_PAIRS_PER_STEP * LANES), lambda b, g, i: (b, i, g))
    return pl.pallas_call(
        functools.partial(_fox_attn_kernel, tq=tq),
        out_shape=jax.ShapeDtypeStruct((B, S, FOX_WIDTH), BF16),
        grid=(B, n_groups, S // tq),
        in_specs=[q_spec, kv_spec, kv_spec],
        out_specs=q_spec,
        compiler_params=_params(3),
        name="fox_attention",
    )(q, k_slabs, v_slabs)


def _dil_attn_kernel(q_ref, k_ref, v_ref, o_ref, lse_ref, *, n_blocks):
    P = DIL_SPAN
    H = HEADS_PER_SLAB
    masks = _slab_head_masks(P)

    def attend(q_rows, k_rows, valid):
        q = q_ref[q_rows, :]
        zero = jnp.zeros_like(q)
        qs = jnp.concatenate([jnp.where(masks[h], q, zero) for h in range(H)], axis=0)
        s = jnp.where(valid, _dot_nt(qs, k_ref[k_rows, :]), NEG)
        m = jnp.max(s, axis=-1, keepdims=True)
        p = jnp.exp(s - m)
        l = jnp.sum(p, axis=-1, keepdims=True)
        pv = _dot(p.astype(BF16), v_ref[k_rows, :]) / l
        lse = jnp.broadcast_to(m + jnp.log(l), (H * P, SLAB))
        o, ls = pv[0:P], lse[0:P]
        for h in range(1, H):
            o = jnp.where(masks[h], pv[h * P:(h + 1) * P], o)
            ls = jnp.where(masks[h], lse[h * P:(h + 1) * P], ls)
        o_ref[q_rows, :] = o
        lse_ref[q_rows, :] = ls

    qi = lax.broadcasted_iota(jnp.int32, (H * P, P), 0) % P
    ki = lax.broadcasted_iota(jnp.int32, (H * P, P), 1)
    attend(pl.ds(0, P), pl.ds(0, P), ki <= qi)

    qi2 = lax.broadcasted_iota(jnp.int32, (H * P, 2 * P), 0) % P
    ki2 = lax.broadcasted_iota(jnp.int32, (H * P, 2 * P), 1)
    valid2 = (ki2 >= qi2) & (ki2 <= qi2 + P)

    def body(n, _):
        start = pl.multiple_of(n * P, P)
        attend(pl.ds(start, P), pl.ds(pl.multiple_of(start - P, P), 2 * P), valid2)
        return 0

    lax.fori_loop(1, n_blocks, body, 0, unroll=True if n_blocks <= 8 else 2)


def _dil_attention(q, k, v, dilation):
    B, d, L, _ = q.shape
    assert d == dilation
    spec = pl.BlockSpec((None, None, L, SLAB), lambda b, r: (b, r, 0, 0))
    return pl.pallas_call(
        functools.partial(_dil_attn_kernel, n_blocks=L // DIL_SPAN),
        out_shape=[jax.ShapeDtypeStruct((B, d, L, SLAB), F32)] * 2,
        grid=(B, d),
        in_specs=[spec] * 3,
        out_specs=[spec] * 2,
        compiler_params=_params(2),
        name=f"dil_attention_d{dilation}",
    )(q, k, v)


def _dil_merge_kernel(*refs):
    n_groups = len(DIL_PATTERNS)
    o_refs, l_refs = refs[:n_groups], refs[n_groups:2 * n_groups]
    out_ref, stage_ref = refs[2 * n_groups], refs[2 * n_groups + 1]
    tm = out_ref.shape[0]
    slot = 0
    outs, lses = [], []
    for src, dst in ((o_refs, outs), (l_refs, lses)):
        for (_, d), ref in zip(DIL_PATTERNS, src):
            if d == 1:
                dst.append(ref[0, 0])
                continue
            halves = []
            for j in range(SLAB // LANES):
                for r in range(d):
                    stage_ref[slot, pl.ds(r, tm // d, stride=d), :] = ref[0, r, :, j * LANES:(j + 1) * LANES]
                halves.append(stage_ref[slot])
                slot += 1
            dst.append(jnp.concatenate(halves, axis=1))
    m = functools.reduce(jnp.maximum, lses)
    es = [jnp.exp(l - m) for l in lses]
    num = es[0] * outs[0] + es[1] * outs[1] + es[2] * outs[2]
    out_ref[...] = (num / (es[0] + es[1] + es[2])).astype(BF16)


def _dil_merge(outs, lses, B, S, tm):
    T = B * S
    tiles_per_seq = S // tm
    specs = [pl.BlockSpec((1, d, tm // d, SLAB), lambda i: (i // tiles_per_seq, 0, i % tiles_per_seq, 0))
             for _, d in DIL_PATTERNS]
    n_strided = 2 * (SLAB // LANES) * sum(d > 1 for _, d in DIL_PATTERNS)
    return pl.pallas_call(
        _dil_merge_kernel,
        out_shape=jax.ShapeDtypeStruct((T, SLAB), BF16),
        grid=(T // tm,),
        in_specs=specs * 2,
        out_specs=pl.BlockSpec((tm, SLAB), lambda i: (i, 0)),
        scratch_shapes=[pltpu.VMEM((n_strided, tm, LANES), F32)],
        compiler_params=_params(1),
        name="dil_merge",
    )(*outs, *lses)


def _gated_merge_kernel(h_ref, oa_ref, ob_ref, oc_ref, od_ref, g0, g1, g2, g3, wa, wb, wc, wd, out_ref):
    h = h_ref[...]
    acc = None
    for o_ref, g_ref, w_ref in ((oa_ref, g0, wa), (ob_ref, g1, wb), (oc_ref, g2, wc), (od_ref, g3, wd)):
        term = jax.nn.sigmoid(_dot(h, g_ref[...])) * _dot(o_ref[...], w_ref[...])
        acc = term if acc is None else acc + term
    out_ref[...] = acc.astype(BF16)


def _gated_merge(h, branches, w_mix, gate_block, w_branches, layer, tm, tn):
    T, D = h.shape
    n_col = D // tn
    row = lambda n: pl.BlockSpec((tm, n), lambda j, i: (i, 0))
    gate_specs = [pl.BlockSpec((None, D, tn), functools.partial(lambda j, i, br: (layer, 0, gate_block + br * n_col + j), br=br))
                  for br in range(N_BRANCH)]
    w_specs = [pl.BlockSpec((None, w.shape[1], tn), lambda j, i: (layer, 0, j)) for w in w_branches]
    return pl.pallas_call(
        _gated_merge_kernel,
        out_shape=jax.ShapeDtypeStruct((T, D), BF16),
        grid=(n_col, T // tm),
        in_specs=[row(D)] + [row(o.shape[1]) for o in branches] + gate_specs + w_specs,
        out_specs=pl.BlockSpec((tm, tn), lambda j, i: (i, j)),
        compiler_params=_params(2),
        name="gated_merge",
    )(h, *branches, w_mix, w_mix, w_mix, w_mix, *w_branches)


def _residual_proj_kernel(a_ref, w_ref, x_ref, g_ref, xo_ref, h_ref):
    x = x_ref[...] + _dot(a_ref[...], w_ref[...])
    xo_ref[...] = x
    h_ref[...] = _rms(x, g_ref[...]).astype(h_ref.dtype)


def _residual_proj(a, w, x, g_next, layer, tm):
    T, D = x.shape
    K = a.shape[1]
    row = lambda n: pl.BlockSpec((tm, n), lambda i: (i, 0))
    return pl.pallas_call(
        _residual_proj_kernel,
        out_shape=[jax.ShapeDtypeStruct((T, D), F32), jax.ShapeDtypeStruct((T, D), BF16)],
        grid=(T // tm,),
        in_specs=[row(K), _layer_weight(layer, K, D), row(D), _resident((1, D))],
        out_specs=[row(D), row(D)],
        compiler_params=_params(1),
        name=f"residual_proj_k{K}",
    )(a, w, x, g_next)


FFN_CHUNK = SLAB


def _ffn_up_kernel(h_ref, wg_ref, wv_ref, cg_ref, cv_ref, act_ref, wg_bf, wv_bf, prev_g, prev_v, *, tiles_per_seq):
    i = pl.program_id(1)

    @pl.when(i == 0)
    def _():
        wg_bf[...] = wg_ref[...].astype(BF16)
        wv_bf[...] = wv_ref[...].astype(BF16)

    @pl.when(i % tiles_per_seq == 0)
    def _():
        prev_g[...] = jnp.zeros_like(prev_g)
        prev_v[...] = jnp.zeros_like(prev_v)

    h = h_ref[...]
    tm = h.shape[0]
    chunks = [slice(c * FFN_CHUNK, (c + 1) * FFN_CHUNK) for c in range(act_ref.shape[1] // FFN_CHUNK)]
    pre = [(_dot(h, wg_bf[:, cols]), _dot(h, wv_bf[:, cols])) for cols in chunks]
    for cols, (zg, zv) in zip(chunks, pre):
        up_gate = _causal_conv3(zg, cg_ref[:, cols], prev_g[:, cols])
        up_val = _causal_conv3(zv, cv_ref[:, cols], prev_v[:, cols])
        prev_g[:, cols] = zg[tm - SUBLANES:, :]
        prev_v[:, cols] = zv[tm - SUBLANES:, :]
        act_ref[:, cols] = (jax.nn.silu(up_gate) * up_val).astype(BF16)


def _ffn_up(h, w_up, conv_w, layer, tm, tn, tiles_per_seq):
    T, D = h.shape
    d_ff = w_up.shape[2] // 2
    n_col = d_ff // tn
    return pl.pallas_call(
        functools.partial(_ffn_up_kernel, tiles_per_seq=tiles_per_seq),
        out_shape=jax.ShapeDtypeStruct((T, d_ff), BF16),
        grid=(n_col, T // tm),
        in_specs=[pl.BlockSpec((tm, D), lambda j, i: (i, 0)),
                  pl.BlockSpec((None, D, tn), lambda j, i: (layer, 0, j)),
                  pl.BlockSpec((None, D, tn), lambda j, i: (layer, 0, n_col + j)),
                  pl.BlockSpec((None, CONV_TAPS, tn), lambda j, i: (layer, 0, j)),
                  pl.BlockSpec((None, CONV_TAPS, tn), lambda j, i: (layer, 0, n_col + j))],
        out_specs=pl.BlockSpec((tm, tn), lambda j, i: (i, j)),
        scratch_shapes=[pltpu.VMEM((D, tn), BF16)] * 2 + [pltpu.VMEM((SUBLANES, tn), F32)] * 2,
        compiler_params=_params(2),
        name="ffn_up",
    )(h, w_up, w_up, conv_w, conv_w)


def _ple_kernel(h_ref, wg_ref, p_ref, wp_ref, x_ref, g_ref, *out_refs, final):
    gate = jax.nn.sigmoid(_dot(h_ref[...], wg_ref[...]))
    x = x_ref[...] + gate * _dot(p_ref[...].astype(BF16), wp_ref[...])
    if not final:
        out_refs[0][...] = x
    out_refs[-1][...] = _rms(x, g_ref[...]).astype(out_refs[-1].dtype)


def _ple(h, w_gate, p, w_proj, x, g_next, layer, tm, final):
    T, D = x.shape
    P = p.shape[2]
    row = lambda n: pl.BlockSpec((tm, n), lambda i: (i, 0))
    if final:
        out_shape = [jax.ShapeDtypeStruct((T, D), F32)]
    else:
        out_shape = [jax.ShapeDtypeStruct((T, D), F32), jax.ShapeDtypeStruct((T, D), BF16)]
    return pl.pallas_call(
        functools.partial(_ple_kernel, final=final),
        out_shape=out_shape,
        grid=(T // tm,),
        in_specs=[row(D), _layer_weight(layer, D, D), pl.BlockSpec((None, tm, P), lambda i: (layer, i, 0)),
                  _layer_weight(layer, P, D), row(D), _resident((1, D))],
        out_specs=[row(D)] * len(out_shape),
        compiler_params=_params(1),
        name="ple_final" if final else "ple",
    )(h, w_gate, p, w_proj, x, g_next)


def kernel(x, p, positions, norm_mix_g, w_in, fox_forget_b, shortconv_w, sgu_norm_g, sgu_w, sgu_b, w_br_fox, w_br_conv, w_br_sgu, w_br_dil, w_out, norm_ffn_g, w_up, ffn_conv_w, w_down, norm_ple_g, w_ple_gate, w_ple_proj, final_norm_g):
    B, S, D = x.shape
    depth = w_in.shape[0]
    T = B * S
    d_ff = w_down.shape[1]

    tm = _tile(S, 512)
    tm_big = _tile(S, 1024)
    tm_wide = _tile(S, 512)
    tm_down = _tile(S, 256)
    tq = _tile(S, 512)
    tn_merge = _tile(D, 512)
    tn_ffn = _tile(d_ff, 512)
    tiles_per_seq = S // tm

    sizes = (3 * FOX_WIDTH, FOX_HEADS, 3 * CONV_WIDTH, 2 * SGU_WIDTH, 3 * DIL_WIDTH, N_BRANCH * D)
    offs = [0]
    for n in sizes:
        offs.append(offs[-1] + n)

    bf = lambda w: w.astype(BF16)
    b_f = jnp.pad(fox_forget_b.astype(F32), ((0, 0), (0, LANES - FOX_HEADS)))[:, None, :]
    moves, seg_block, pos = [], {}, 0
    for name, seg, block in (("gate", 5, tn_merge), ("sgu", 3, sizes[3]), ("conv", 2, sizes[2]), ("dil", 4, sizes[4])):
        pos += -pos % block
        seg_block[name] = pos // block
        moves.append((offs[seg], sizes[seg], pos))
        pos += sizes[seg]
    w_fox, w_mix = _repack_w_in(w_in, offs[1] + LANES, tuple(moves), pos)
    w_brs = [bf(w_br_fox), bf(w_br_conv), bf(w_br_sgu), bf(w_br_dil)]
    w_out_b, w_down_b = bf(w_out), bf(w_down)
    w_pg_b, w_pp_b = bf(w_ple_gate), bf(w_ple_proj)
    sgu_bias = jnp.repeat(jnp.swapaxes(sgu_b, 1, 2), SGU_GROUP_DIM, axis=2)
    sgu_norm = sgu_norm_g[:, None, :]
    p_tok = p.reshape(depth, T, p.shape[-1])
    row_vec = lambda g: g[None, :]

    tables = _rope_tables(positions, tm)
    xf = x.reshape(T, D)
    h = _rmsnorm(xf, row_vec(norm_mix_g[0]), tm)
    out = None
    for i in range(depth):
        qa, ka, va = _fox_proj(h, w_fox, b_f, i, tm, tiles_per_seq)
        o_a = _fox_attention(qa.reshape(B, S, -1), ka.reshape(B, S, -1), va.reshape(B, S, -1),
                             B, S, tq).reshape(T, FOX_WIDTH)
        o_b = _conv_branch(h, w_mix, seg_block["conv"], shortconv_w, i, tm, tiles_per_seq)
        o_c = _sgu_branch(h, w_mix, seg_block["sgu"], sgu_norm, sgu_w, sgu_bias, i, tm)
        qkv = _dil_proj(h, w_mix, seg_block["dil"], tables, i, B, S, tm)
        outs, lses = [], []
        for g, (window, dil) in enumerate(DIL_PATTERNS):
            assert window // dil == DIL_SPAN
            o_g, l_g = _dil_attention(*qkv[3 * g:3 * g + 3], dil)
            outs.append(o_g)
            lses.append(l_g)
        o_d = _dil_merge(outs, lses, B, S, tm)

        merged = _gated_merge(h, (o_a, o_b, o_c, o_d), w_mix, seg_block["gate"], w_brs, i, tm_big, tn_merge)
        xf, h = _residual_proj(merged, w_out_b, xf, row_vec(norm_ffn_g[i]), i, tm_wide)

        act = _ffn_up(h, w_up, ffn_conv_w, i, tm_big, tn_ffn, S // tm_big)
        xf, h = _residual_proj(act, w_down_b, xf, row_vec(norm_ple_g[i]), i, tm_down)

        final = i == depth - 1
        g_next = final_norm_g if final else norm_mix_g[i + 1]
        res = _ple(h, w_pg_b, p_tok, w_pp_b, xf, row_vec(g_next), i, tm_wide, final)
        if final:
            out = res[0]
        else:
            xf, h = res
    return out.reshape(B, S, D)
```

```python
import functools

import jax
import jax.numpy as jnp
from jax import lax
from jax.experimental import pallas as pl
from jax.experimental.pallas import tpu as pltpu

F32 = jnp.float32
BF16 = jnp.bfloat16

HEAD_DIM = 64
EPS = 1e-6
FOX_HEADS = 8
FOX_WIDTH = FOX_HEADS * HEAD_DIM
CONV_WIDTH = 512
CONV_TAPS = 3
SGU_GROUPS = 4
SGU_GROUP_DIM = 128
SGU_WIDTH = SGU_GROUPS * SGU_GROUP_DIM
SGU_CHUNK = 128
DIL_PATTERNS = ((128, 1), (512, 4), (2048, 16))
DIL_HEADS_PER_GROUP = 4
DIL_HEADS = DIL_HEADS_PER_GROUP * len(DIL_PATTERNS)
DIL_WIDTH = DIL_HEADS * HEAD_DIM
DIL_OUT = DIL_HEADS_PER_GROUP * HEAD_DIM
DIL_SPAN = 128
ROPE_THETA = 500000.0
ROPE_DIM = HEAD_DIM // 4
N_BRANCH = 4

LANES = 128
SUBLANES = 8
BF16_ROWS = 16
HEADS_PER_SLAB = 4
SLAB = HEADS_PER_SLAB * HEAD_DIM
VMEM_LIMIT = 56 * 1024 * 1024
NEG = -1e30
QK_SCALE = HEAD_DIM ** -0.5


def _params(n_axes):
    return pltpu.CompilerParams(dimension_semantics=("arbitrary",) * n_axes, vmem_limit_bytes=VMEM_LIMIT)


def _resident(shape, index=None):
    index = (0,) * len(shape) if index is None else index
    return pl.BlockSpec(shape, lambda *_: index, pipeline_mode=pl.Buffered(1))


def _layer_weight(layer, rows, cols):
    return _resident((None, rows, cols), (layer, 0, 0))


def _tile(n, pref):
    t = min(n, pref)
    assert n % t == 0, (n, t)
    return t


def _rms(x, g):
    var = jnp.mean(x * x, axis=-1, keepdims=True)
    return x * lax.rsqrt(var + EPS) * g


def _dot(a, b):
    return jnp.dot(a, b, preferred_element_type=F32)


def _dot_nt(a, b):
    return lax.dot_general(a, b, (((1,), (1,)), ((), ())), preferred_element_type=F32)


def _causal_conv3(z, w, prev):
    y = w[0:1] * pltpu.roll(z, 2, 0) + w[1:2] * pltpu.roll(z, 1, 0) + w[2:3] * z
    zh = z[0:BF16_ROWS]
    row = lax.broadcasted_iota(jnp.int32, zh.shape, 0)
    p1 = prev[SUBLANES - 1:SUBLANES]
    p2 = prev[SUBLANES - 2:SUBLANES - 1]
    z1 = jnp.where(row == 0, p1, pltpu.roll(zh, 1, 0))
    z2 = jnp.where(row == 0, p2, jnp.where(row == 1, p1, pltpu.roll(zh, 2, 0)))
    yh = w[0:1] * z2 + w[1:2] * z1 + w[2:3] * zh
    return jnp.concatenate([yh, y[BF16_ROWS:]], axis=0)


def _rmsnorm_kernel(x_ref, g_ref, h_ref):
    h_ref[...] = _rms(x_ref[...], g_ref[...]).astype(h_ref.dtype)


def _rmsnorm(x, g, tm):
    T, D = x.shape
    return pl.pallas_call(
        _rmsnorm_kernel,
        out_shape=jax.ShapeDtypeStruct((T, D), BF16),
        grid=(T // tm,),
        in_specs=[pl.BlockSpec((tm, D), lambda i: (i, 0)), _resident((1, D))],
        out_specs=pl.BlockSpec((tm, D), lambda i: (i, 0)),
        compiler_params=_params(1),
        name="rmsnorm",
    )(x, g)


def _rope_table_kernel(pos_ref, inv_ref, c_ref, s1_ref, s2_ref):
    ang = pos_ref[...] * inv_ref[...]
    cos = jnp.cos(ang)
    sin = jnp.sin(ang)
    lane = lax.broadcasted_iota(jnp.int32, ang.shape, 1) % HEAD_DIM
    half = ROPE_DIM // 2
    c_ref[...] = jnp.where(lane < ROPE_DIM, cos, 1.0)
    s1_ref[...] = jnp.where((lane >= half) & (lane < ROPE_DIM), sin, 0.0)
    s2_ref[...] = jnp.where(lane < half, -sin, 0.0)


def _rope_tables(positions, tm):
    T = positions.size
    half = ROPE_DIM // 2
    inv = ROPE_THETA ** (-jnp.arange(half, dtype=F32) * (2.0 / ROPE_DIM))
    lane = jnp.arange(LANES) % HEAD_DIM
    inv_lane = jnp.where(lane < ROPE_DIM, inv[lane % half], 0.0).astype(F32)[None, :]
    posf = jnp.broadcast_to(positions.reshape(T, 1).astype(F32), (T, LANES))
    spec = pl.BlockSpec((tm, LANES), lambda i: (i, 0))
    return pl.pallas_call(
        _rope_table_kernel,
        out_shape=[jax.ShapeDtypeStruct((T, LANES), F32)] * 3,
        grid=(T // tm,),
        in_specs=[spec, _resident((1, LANES))],
        out_specs=[spec] * 3,
        compiler_params=_params(1),
        name="rope_tables",
    )(posf, inv_lane)


def _repack_kernel(wt_ref, fox_ref, mix_ref, *, fox_width, moves, mix_width):
    def rows_of(src, width):
        return wt_ref[src:src + width, :].T.astype(BF16)

    fox_ref[...] = rows_of(0, fox_width)
    end = 0
    for src, width, dst in moves:
        if dst > end:
            mix_ref[:, end:dst] = jnp.zeros((mix_ref.shape[0], dst - end), BF16)
        mix_ref[:, dst:dst + width] = rows_of(src, width)
        end = dst + width
    assert end == mix_width


def _repack_w_in(w_in, fox_width, moves, mix_width):
    depth, D, d_in = w_in.shape
    assert all(src % SUBLANES == 0 for src, _, _ in moves)
    tr = _tile(D, LANES)
    return pl.pallas_call(
        functools.partial(_repack_kernel, fox_width=fox_width, moves=moves, mix_width=mix_width),
        out_shape=[jax.ShapeDtypeStruct((depth, D, fox_width), BF16),
                   jax.ShapeDtypeStruct((depth, D, mix_width), BF16)],
        grid=(depth, D // tr),
        in_specs=[pl.BlockSpec((None, d_in, tr), lambda l, r: (l, 0, r))],
        out_specs=[pl.BlockSpec((None, tr, fox_width), lambda l, r: (l, r, 0)),
                   pl.BlockSpec((None, tr, mix_width), lambda l, r: (l, r, 0))],
        compiler_params=_params(2),
        name="repack_w_in",
    )(jnp.swapaxes(w_in, 1, 2))


def _fox_proj_kernel(h_ref, w_ref, wf_ref, bf_ref, q_ref, k_ref, v_ref, carry_ref, *, tiles_per_seq):
    @pl.when(pl.program_id(0) % tiles_per_seq == 0)
    def _():
        carry_ref[...] = jnp.zeros_like(carry_ref)

    h = h_ref[...]
    W = FOX_WIDTH
    q_ref[...] = (_dot(h, w_ref[:, 0:W]) * QK_SCALE).astype(BF16)
    k = _dot(h, w_ref[:, W:2 * W]).astype(BF16)
    v = _dot(h, w_ref[:, 2 * W:3 * W]).astype(BF16)

    x = jax.nn.log_sigmoid(_dot(h, wf_ref[...]) + bf_ref[...])
    tm = x.shape[0]
    row = lax.broadcasted_iota(jnp.int32, x.shape, 0)
    lane = lax.broadcasted_iota(jnp.int32, x.shape, 1)
    s = 1
    while s < tm:
        x = x + jnp.where(row >= s, pltpu.roll(x, s, 0), 0.0)
        s *= 2
    x = x + carry_ref[0:1, :]
    carry_ref[...] = jnp.broadcast_to(x[tm - 1:tm, :], carry_ref.shape)

    x = jnp.where(lane < FOX_HEADS, x, 0.0)
    hi = x.astype(BF16).astype(F32)
    mid = (x - hi).astype(BF16).astype(F32)
    lo = (x - hi - mid).astype(BF16).astype(F32)
    f_lanes = (hi + pltpu.roll(mid, FOX_HEADS, 1) + pltpu.roll(lo, 2 * FOX_HEADS, 1)).astype(BF16)
    one_lane = jnp.where(lane == 0, 1.0, 0.0).astype(BF16)
    for j in range(FOX_WIDTH // LANES):
        k_ref[:, j * SLAB:j * SLAB + LANES] = k[:, j * LANES:(j + 1) * LANES]
        k_ref[:, j * SLAB + LANES:(j + 1) * SLAB] = f_lanes
        v_ref[:, j * SLAB:j * SLAB + LANES] = v[:, j * LANES:(j + 1) * LANES]
        v_ref[:, j * SLAB + LANES:(j + 1) * SLAB] = one_lane


def _conv_branch_kernel(h_ref, w_ref, cw_ref, o_ref, prev_ref, *, tiles_per_seq):
    @pl.when(pl.program_id(0) % tiles_per_seq == 0)
    def _():
        prev_ref[...] = jnp.zeros_like(prev_ref)

    h = h_ref[...]
    W = CONV_WIDTH
    tm = h.shape[0]
    xb = _dot(h, w_ref[:, 0:W])
    gate_b = _dot(h, w_ref[:, W:2 * W])
    gate_c = _dot(h, w_ref[:, 2 * W:3 * W])
    z = gate_c * xb
    y = _causal_conv3(z, cw_ref[...], prev_ref[...])
    prev_ref[...] = z[tm - SUBLANES:, :]
    o_ref[...] = (gate_b * y).astype(BF16)


def _sgu_branch_kernel(h_ref, w_ref, ng_ref, ws_ref, bias_ref, o_ref):
    h = h_ref[...]
    W = SGU_WIDTH
    C = SGU_CHUNK
    G = SGU_GROUP_DIM
    u = jax.nn.gelu(_dot(h, w_ref[:, 0:W]))
    v = _rms(jax.nn.gelu(_dot(h, w_ref[:, W:2 * W])), ng_ref[...]).astype(BF16)
    row = lax.broadcasted_iota(jnp.int32, (C, C), 0)
    col = lax.broadcasted_iota(jnp.int32, (C, C), 1)
    for g in range(SGU_GROUPS):
        w_tril = jnp.where(col <= row, ws_ref[g], 0.0).astype(BF16)
        cols = slice(g * G, (g + 1) * G)
        for c in range(h.shape[0] // C):
            rows = slice(c * C, (c + 1) * C)
            mixed = _dot(w_tril, v[rows, cols]) + bias_ref[:, cols]
            o_ref[rows, cols] = (u[rows, cols] * mixed).astype(BF16)


def _dil_proj_kernel(h_ref, w_ref, c_ref, s1_ref, s2_ref, *refs):
    n_groups = len(DIL_PATTERNS)
    out_refs = refs[:3 * n_groups]
    stage_ref = refs[3 * n_groups]
    h = h_ref[...]
    tm = h.shape[0]
    cos = c_ref[...]
    s1 = s1_ref[...]
    s2 = s2_ref[...]
    shift = ROPE_DIM // 2
    slot = 0
    for which, rope, scale in ((0, True, QK_SCALE), (1, True, None), (2, False, None)):
        for g, (_, d) in enumerate(DIL_PATTERNS):
            col = which * DIL_WIDTH + g * SLAB
            x = _dot(h, w_ref[:, col:col + SLAB])
            parts = []
            for j in range(SLAB // LANES):
                xj = x[:, j * LANES:(j + 1) * LANES]
                if rope:
                    xj = xj * cos + pltpu.roll(xj, shift, 1) * s1 + pltpu.roll(xj, LANES - shift, 1) * s2
                if scale is not None:
                    xj = xj * scale
                parts.append(xj)
            out = out_refs[g * 3 + which]
            for j, xj in enumerate(parts):
                lanes = slice(j * LANES, (j + 1) * LANES)
                if d == 1:
                    out[0, 0, :, lanes] = xj.astype(BF16)
                else:
                    stage_ref[slot] = xj
                    for r in range(d):
                        out[0, r, :, lanes] = stage_ref[slot, pl.ds(r, tm // d, stride=d), :].astype(BF16)
                    slot += 1


N_FOX_PROJ_IN, N_FOX_PROJ_OUT = 3, 3
N_CONV_IN, N_SGU_IN = 2, 4


def _attn_proj_kernel(h_ref, *refs, tiles_per_seq):
    n_dil_out = 3 * len(DIL_PATTERNS)
    fox_in, refs = refs[:N_FOX_PROJ_IN], refs[N_FOX_PROJ_IN:]
    dil_in, refs = refs[:4], refs[4:]
    fox_out, refs = refs[:N_FOX_PROJ_OUT], refs[N_FOX_PROJ_OUT:]
    dil_out, (carry_ref, stage_ref) = refs[:n_dil_out], refs[n_dil_out:]
    _fox_proj_kernel(h_ref, *fox_in, *fox_out, carry_ref, tiles_per_seq=tiles_per_seq)
    _dil_proj_kernel(h_ref, *dil_in, *dil_out, stage_ref)


def _attn_proj(h, w_fox, b_f, w_mix, dil_block, tables, layer, B, S, tm):
    T, D = h.shape
    W = FOX_WIDTH
    n_pairs = FOX_WIDTH // LANES
    tiles_per_seq = S // tm
    row = lambda n: pl.BlockSpec((tm, n), lambda i: (i, 0))
    out_shape = [jax.ShapeDtypeStruct((T, W), BF16)] + [jax.ShapeDtypeStruct((T, n_pairs * SLAB), BF16)] * 2
    out_specs = [row(W), row(n_pairs * SLAB), row(n_pairs * SLAB)]
    n_strided = 0
    for _, d in DIL_PATTERNS:
        assert (tm // d) % BF16_ROWS == 0
        n_strided += 3 * (SLAB // LANES) * (d > 1)
        for _ in range(3):
            out_shape.append(jax.ShapeDtypeStruct((B, d, S // d, SLAB), BF16))
            out_specs.append(pl.BlockSpec((1, d, tm // d, SLAB),
                                          lambda i: (i // tiles_per_seq, 0, i % tiles_per_seq, 0)))
    res = pl.pallas_call(
        functools.partial(_attn_proj_kernel, tiles_per_seq=tiles_per_seq),
        out_shape=out_shape,
        grid=(T // tm,),
        in_specs=[row(D), _resident((None, D, 3 * W), (layer, 0, 0)),
                  _resident((None, D, LANES), (layer, 0, 3 * W // LANES)), _layer_weight(layer, 1, LANES),
                  _resident((None, D, 3 * DIL_WIDTH), (layer, 0, dil_block)), row(LANES), row(LANES), row(LANES)],
        out_specs=out_specs,
        scratch_shapes=[pltpu.VMEM((SUBLANES, LANES), F32), pltpu.VMEM((n_strided, tm, LANES), F32)],
        compiler_params=_params(1),
        name="attn_proj",
    )(h, w_fox, w_fox, b_f, w_mix, *tables)
    return res[:3], res[3:]


def _conv_sgu_kernel(h_ref, *refs, tiles_per_seq):
    conv_in, refs = refs[:N_CONV_IN], refs[N_CONV_IN:]
    sgu_in, (ob_ref, oc_ref, prev_ref) = refs[:N_SGU_IN], refs[N_SGU_IN:]
    _conv_branch_kernel(h_ref, *conv_in, ob_ref, prev_ref, tiles_per_seq=tiles_per_seq)
    _sgu_branch_kernel(h_ref, *sgu_in, oc_ref)


def _conv_sgu_branches(h, w_mix, conv_block, sgu_block, conv_w, norm_g, w_s, bias_full, layer, S, tm):
    T, D = h.shape
    row = lambda n: pl.BlockSpec((tm, n), lambda i: (i, 0))
    return pl.pallas_call(
        functools.partial(_conv_sgu_kernel, tiles_per_seq=S // tm),
        out_shape=[jax.ShapeDtypeStruct((T, CONV_WIDTH), BF16), jax.ShapeDtypeStruct((T, SGU_WIDTH), BF16)],
        grid=(T // tm,),
        in_specs=[row(D), _resident((None, D, 3 * CONV_WIDTH), (layer, 0, conv_block)),
                  _layer_weight(layer, CONV_TAPS, CONV_WIDTH),
                  _resident((None, D, 2 * SGU_WIDTH), (layer, 0, sgu_block)), _layer_weight(layer, 1, SGU_WIDTH),
                  _resident((None, SGU_GROUPS, SGU_CHUNK, SGU_CHUNK), (layer, 0, 0, 0)),
                  _layer_weight(layer, SGU_CHUNK, SGU_WIDTH)],
        out_specs=[row(CONV_WIDTH), row(SGU_WIDTH)],
        scratch_shapes=[pltpu.VMEM((SUBLANES, CONV_WIDTH), F32)],
        compiler_params=_params(1),
        name="conv_sgu_branches",
    )(h, w_mix, conv_w, w_mix, norm_g, w_s, bias_full)


def _slab_head_masks(n_rows):
    lane = lax.broadcasted_iota(jnp.int32, (n_rows, SLAB), 1)
    return [(lane >= h * HEAD_DIM) & (lane < (h + 1) * HEAD_DIM) for h in range(HEADS_PER_SLAB)]


FOX_PAIRS_PER_STEP = 2


def _fox_attn_kernel(q_ref, k_ref, v_ref, o_ref, *, tq):
    group = pl.program_id(1)
    i = pl.program_id(2)
    q0 = pl.multiple_of(i * tq, tq)
    lane = lax.broadcasted_iota(jnp.int32, (tq, LANES), 1)
    rowi = lax.broadcasted_iota(jnp.int32, (2 * tq, tq), 0) % tq
    coli = lax.broadcasted_iota(jnp.int32, (2 * tq, tq), 1)

    def stacked_queries(c):
        q = q_ref[0, :, c * LANES:(c + 1) * LANES]
        zero = jnp.zeros_like(q)
        halves = []
        for e in range(2):
            head = 2 * (FOX_PAIRS_PER_STEP * group + c) + e
            f_pick = (lane == head) | (lane == head + FOX_HEADS) | (lane == head + 2 * FOX_HEADS)
            halves.append(jnp.concatenate(
                [jnp.where(lane // HEAD_DIM == e, q, zero), jnp.where(f_pick, -1.0, 0.0).astype(BF16)], axis=1))
        return jnp.concatenate(halves, axis=0)

    qs = [stacked_queries(c) for c in range(FOX_PAIRS_PER_STEP)]

    def block(k0, carry, diagonal):
        slabs = [slice(c * SLAB, (c + 1) * SLAB) for c in range(FOX_PAIRS_PER_STEP)]
        logits = [_dot_nt(qs[c], k_ref[0, pl.ds(k0, tq), slabs[c]]) for c in range(FOX_PAIRS_PER_STEP)]
        stats = []
        for (m, _), s in zip(carry, logits):
            if diagonal:
                s = jnp.where(coli <= rowi, s, NEG)
            m_new = jnp.maximum(m, jnp.max(s, axis=-1, keepdims=True))
            stats.append((m_new, jnp.exp(m - m_new), jnp.exp(s - m_new).astype(BF16)))
        return tuple((m_new, alpha * acc + _dot(p, v_ref[0, pl.ds(k0, tq), slabs[c]]))
                     for c, ((_, acc), (m_new, alpha, p)) in enumerate(zip(carry, stats)))

    init = tuple((jnp.full((2 * tq, 1), NEG, F32), jnp.zeros((2 * tq, SLAB), F32))
                 for _ in range(FOX_PAIRS_PER_STEP))
    carry = lax.fori_loop(0, i, lambda kb, c: block(pl.multiple_of(kb * tq, tq), c, False), init)
    for c, (_, acc) in enumerate(block(q0, carry, True)):
        out = acc[:, 0:LANES] / acc[:, LANES:LANES + 1]
        o_ref[0, :, c * LANES:(c + 1) * LANES] = jnp.where(lane // HEAD_DIM == 0, out[0:tq],
                                                           out[tq:2 * tq]).astype(BF16)


def _fox_attention(q, k_slabs, v_slabs, B, S, tq):
    n_groups = FOX_WIDTH // LANES // FOX_PAIRS_PER_STEP
    kv_spec = pl.BlockSpec((1, S, FOX_PAIRS_PER_STEP * SLAB), lambda b, g, i: (b, 0, g))
    q_spec = pl.BlockSpec((1, tq, FOX_PAIRS_PER_STEP * LANES), lambda b, g, i: (b, i, g))
    return pl.pallas_call(
        functools.partial(_fox_attn_kernel, tq=tq),
        out_shape=jax.ShapeDtypeStruct((B, S, FOX_WIDTH), BF16),
        grid=(B, n_groups, S // tq),
        in_specs=[q_spec, kv_spec, kv_spec],
        out_specs=q_spec,
        compiler_params=_params(3),
        name="fox_attention",
    )(q, k_slabs, v_slabs)


DIL_BLOCKS_PER_STEP = 8


def _dil_attn_kernel(q_ref, k_ref, v_ref, o_ref, lse_ref, *, n_blocks):
    P = DIL_SPAN
    H = HEADS_PER_SLAB
    masks = _slab_head_masks(P)

    def attend(r, q_rows, k_rows, valid):
        q = q_ref[r, q_rows, :]
        zero = jnp.zeros_like(q)
        qs = jnp.concatenate([jnp.where(masks[h], q, zero) for h in range(H)], axis=0)
        s = jnp.where(valid, _dot_nt(qs, k_ref[r, k_rows, :]), NEG)
        m = jnp.max(s, axis=-1, keepdims=True)
        p = jnp.exp(s - m)
        l = jnp.sum(p, axis=-1, keepdims=True)
        pv = _dot(p.astype(BF16), v_ref[r, k_rows, :]) / l
        lse = jnp.broadcast_to(m + jnp.log(l), (H * P, SLAB))
        o, ls = pv[0:P], lse[0:P]
        for h in range(1, H):
            o = jnp.where(masks[h], pv[h * P:(h + 1) * P], o)
            ls = jnp.where(masks[h], lse[h * P:(h + 1) * P], ls)
        o_ref[r, q_rows, :] = o
        lse_ref[r, q_rows, :] = ls

    qi = lax.broadcasted_iota(jnp.int32, (H * P, P), 0) % P
    ki = lax.broadcasted_iota(jnp.int32, (H * P, P), 1)
    qi2 = lax.broadcasted_iota(jnp.int32, (H * P, 2 * P), 0) % P
    ki2 = lax.broadcasted_iota(jnp.int32, (H * P, 2 * P), 1)
    valid2 = (ki2 >= qi2) & (ki2 <= qi2 + P)

    for r in range(q_ref.shape[0]):
        attend(r, pl.ds(0, P), pl.ds(0, P), ki <= qi)

        def body(n, _, r=r):
            start = pl.multiple_of(n * P, P)
            attend(r, pl.ds(start, P), pl.ds(pl.multiple_of(start - P, P), 2 * P), valid2)
            return 0

        lax.fori_loop(1, n_blocks, body, 0, unroll=True if n_blocks <= 8 else 2)


def _dil_attention(q, k, v, dilation):
    B, d, L, _ = q.shape
    assert d == dilation
    n_blocks = L // DIL_SPAN
    per_step = max(1, min(d, DIL_BLOCKS_PER_STEP // n_blocks))
    spec = pl.BlockSpec((None, per_step, L, SLAB), lambda b, r: (b, r, 0, 0))
    return pl.pallas_call(
        functools.partial(_dil_attn_kernel, n_blocks=n_blocks),
        out_shape=[jax.ShapeDtypeStruct((B, d, L, SLAB), F32)] * 2,
        grid=(B, d // per_step),
        in_specs=[spec] * 3,
        out_specs=[spec] * 2,
        compiler_params=_params(2),
        name=f"dil_attention_d{dilation}",
    )(q, k, v)


def _dil_merge_kernel(*refs):
    n_groups = len(DIL_PATTERNS)
    o_refs, l_refs = refs[:n_groups], refs[n_groups:2 * n_groups]
    out_ref, stage_ref = refs[2 * n_groups], refs[2 * n_groups + 1]
    tm = out_ref.shape[0]
    slot = 0
    outs, lses = [], []
    for src, dst in ((o_refs, outs), (l_refs, lses)):
        for (_, d), ref in zip(DIL_PATTERNS, src):
            if d == 1:
                dst.append(ref[0, 0])
                continue
            halves = []
            for j in range(SLAB // LANES):
                for r in range(d):
                    stage_ref[slot, pl.ds(r, tm // d, stride=d), :] = ref[0, r, :, j * LANES:(j + 1) * LANES]
                halves.append(stage_ref[slot])
                slot += 1
            dst.append(jnp.concatenate(halves, axis=1))
    m = functools.reduce(jnp.maximum, lses)
    es = [jnp.exp(l - m) for l in lses]
    num = es[0] * outs[0] + es[1] * outs[1] + es[2] * outs[2]
    out_ref[...] = (num / (es[0] + es[1] + es[2])).astype(BF16)


def _dil_merge(outs, lses, B, S, tm):
    T = B * S
    tiles_per_seq = S // tm
    specs = [pl.BlockSpec((1, d, tm // d, SLAB), lambda i: (i // tiles_per_seq, 0, i % tiles_per_seq, 0))
             for _, d in DIL_PATTERNS]
    n_strided = 2 * (SLAB // LANES) * sum(d > 1 for _, d in DIL_PATTERNS)
    return pl.pallas_call(
        _dil_merge_kernel,
        out_shape=jax.ShapeDtypeStruct((T, SLAB), BF16),
        grid=(T // tm,),
        in_specs=specs * 2,
        out_specs=pl.BlockSpec((tm, SLAB), lambda i: (i, 0)),
        scratch_shapes=[pltpu.VMEM((n_strided, tm, LANES), F32)],
        compiler_params=_params(1),
        name="dil_merge",
    )(*outs, *lses)


def _gated_merge_kernel(h_ref, oa_ref, ob_ref, oc_ref, od_ref, g0, g1, g2, g3, wa, wb, wc, wd, out_ref):
    h = h_ref[...]
    acc = None
    for o_ref, g_ref, w_ref in ((oa_ref, g0, wa), (ob_ref, g1, wb), (oc_ref, g2, wc), (od_ref, g3, wd)):
        term = jax.nn.sigmoid(_dot(h, g_ref[...])) * _dot(o_ref[...], w_ref[...])
        acc = term if acc is None else acc + term
    out_ref[...] = acc.astype(BF16)


def _gated_merge(h, branches, w_mix, gate_block, w_branches, layer, tm, tn):
    T, D = h.shape
    n_col = D // tn
    row = lambda n: pl.BlockSpec((tm, n), lambda j, i: (i, 0))
    gate_specs = [pl.BlockSpec((None, D, tn), functools.partial(lambda j, i, br: (layer, 0, gate_block + br * n_col + j), br=br))
                  for br in range(N_BRANCH)]
    w_specs = [pl.BlockSpec((None, w.shape[1], tn), lambda j, i: (layer, 0, j)) for w in w_branches]
    return pl.pallas_call(
        _gated_merge_kernel,
        out_shape=jax.ShapeDtypeStruct((T, D), BF16),
        grid=(n_col, T // tm),
        in_specs=[row(D)] + [row(o.shape[1]) for o in branches] + gate_specs + w_specs,
        out_specs=pl.BlockSpec((tm, tn), lambda j, i: (i, j)),
        compiler_params=_params(2),
        name="gated_merge",
    )(h, *branches, w_mix, w_mix, w_mix, w_mix, *w_branches)


def _residual_proj_kernel(a_ref, w_ref, x_ref, g_ref, xo_ref, h_ref):
    x = x_ref[...] + _dot(a_ref[...], w_ref[...])
    xo_ref[...] = x
    h_ref[...] = _rms(x, g_ref[...]).astype(h_ref.dtype)


def _residual_proj(a, w, x, g_next, layer, tm):
    T, D = x.shape
    K = a.shape[1]
    row = lambda n: pl.BlockSpec((tm, n), lambda i: (i, 0))
    return pl.pallas_call(
        _residual_proj_kernel,
        out_shape=[jax.ShapeDtypeStruct((T, D), F32), jax.ShapeDtypeStruct((T, D), BF16)],
        grid=(T // tm,),
        in_specs=[row(K), _layer_weight(layer, K, D), row(D), _resident((1, D))],
        out_specs=[row(D), row(D)],
        compiler_params=_params(1),
        name=f"residual_proj_k{K}",
    )(a, w, x, g_next)


FFN_CHUNK = SLAB


def _ffn_up_kernel(h_ref, wg_ref, wv_ref, cg_ref, cv_ref, act_ref, wg_bf, wv_bf, prev_g, prev_v, *, tiles_per_seq):
    i = pl.program_id(1)

    @pl.when(i == 0)
    def _():
        wg_bf[...] = wg_ref[...].astype(BF16)
        wv_bf[...] = wv_ref[...].astype(BF16)

    @pl.when(i % tiles_per_seq == 0)
    def _():
        prev_g[...] = jnp.zeros_like(prev_g)
        prev_v[...] = jnp.zeros_like(prev_v)

    h = h_ref[...]
    tm = h.shape[0]
    chunks = [slice(c * FFN_CHUNK, (c + 1) * FFN_CHUNK) for c in range(act_ref.shape[1] // FFN_CHUNK)]
    pre = [(_dot(h, wg_bf[:, cols]), _dot(h, wv_bf[:, cols])) for cols in chunks]
    for cols, (zg, zv) in zip(chunks, pre):
        up_gate = _causal_conv3(zg, cg_ref[:, cols], prev_g[:, cols])
        up_val = _causal_conv3(zv, cv_ref[:, cols], prev_v[:, cols])
        prev_g[:, cols] = zg[tm - SUBLANES:, :]
        prev_v[:, cols] = zv[tm - SUBLANES:, :]
        act_ref[:, cols] = (jax.nn.silu(up_gate) * up_val).astype(BF16)


def _ffn_up(h, w_up, conv_w, layer, tm, tn, tiles_per_seq):
    T, D = h.shape
    d_ff = w_up.shape[2] // 2
    n_col = d_ff // tn
    return pl.pallas_call(
        functools.partial(_ffn_up_kernel, tiles_per_seq=tiles_per_seq),
        out_shape=jax.ShapeDtypeStruct((T, d_ff), BF16),
        grid=(n_col, T // tm),
        in_specs=[pl.BlockSpec((tm, D), lambda j, i: (i, 0)),
                  pl.BlockSpec((None, D, tn), lambda j, i: (layer, 0, j)),
                  pl.BlockSpec((None, D, tn), lambda j, i: (layer, 0, n_col + j)),
                  pl.BlockSpec((None, CONV_TAPS, tn), lambda j, i: (layer, 0, j)),
                  pl.BlockSpec((None, CONV_TAPS, tn), lambda j, i: (layer, 0, n_col + j))],
        out_specs=pl.BlockSpec((tm, tn), lambda j, i: (i, j)),
        scratch_shapes=[pltpu.VMEM((D, tn), BF16)] * 2 + [pltpu.VMEM((SUBLANES, tn), F32)] * 2,
        compiler_params=_params(2),
        name="ffn_up",
    )(h, w_up, w_up, conv_w, conv_w)


def _ple_kernel(h_ref, wg_ref, p_ref, wp_ref, x_ref, g_ref, *out_refs, final):
    gate = jax.nn.sigmoid(_dot(h_ref[...], wg_ref[...]))
    x = x_ref[...] + gate * _dot(p_ref[...].astype(BF16), wp_ref[...])
    if not final:
        out_refs[0][...] = x
    out_refs[-1][...] = _rms(x, g_ref[...]).astype(out_refs[-1].dtype)


def _ple(h, w_gate, p, w_proj, x, g_next, layer, tm, final):
    T, D = x.shape
    P = p.shape[2]
    row = lambda n: pl.BlockSpec((tm, n), lambda i: (i, 0))
    if final:
        out_shape = [jax.ShapeDtypeStruct((T, D), F32)]
    else:
        out_shape = [jax.ShapeDtypeStruct((T, D), F32), jax.ShapeDtypeStruct((T, D), BF16)]
    return pl.pallas_call(
        functools.partial(_ple_kernel, final=final),
        out_shape=out_shape,
        grid=(T // tm,),
        in_specs=[row(D), _layer_weight(layer, D, D), pl.BlockSpec((None, tm, P), lambda i: (layer, i, 0)),
                  _layer_weight(layer, P, D), row(D), _resident((1, D))],
        out_specs=[row(D)] * len(out_shape),
        compiler_params=_params(1),
        name="ple_final" if final else "ple",
    )(h, w_gate, p, w_proj, x, g_next)


def kernel(x, p, positions, norm_mix_g, w_in, fox_forget_b, shortconv_w, sgu_norm_g, sgu_w, sgu_b, w_br_fox, w_br_conv, w_br_sgu, w_br_dil, w_out, norm_ffn_g, w_up, ffn_conv_w, w_down, norm_ple_g, w_ple_gate, w_ple_proj, final_norm_g):
    B, S, D = x.shape
    depth = w_in.shape[0]
    T = B * S
    d_ff = w_down.shape[1]

    tm = _tile(S, 1024)
    tm_attn = _tile(S, 512)
    tm_big = _tile(S, 1024)
    tm_wide = _tile(S, 512)
    tm_down = _tile(S, 256)
    tq = _tile(S, 512)
    tn_merge = _tile(D, 512)
    tn_ffn = _tile(d_ff, 512)

    sizes = (3 * FOX_WIDTH, FOX_HEADS, 3 * CONV_WIDTH, 2 * SGU_WIDTH, 3 * DIL_WIDTH, N_BRANCH * D)
    offs = [0]
    for n in sizes:
        offs.append(offs[-1] + n)

    bf = lambda w: w.astype(BF16)
    b_f = jnp.pad(fox_forget_b.astype(F32), ((0, 0), (0, LANES - FOX_HEADS)))[:, None, :]
    moves, seg_block, pos = [], {}, 0
    for name, seg, block in (("gate", 5, tn_merge), ("sgu", 3, sizes[3]), ("conv", 2, sizes[2]), ("dil", 4, sizes[4])):
        pos += -pos % block
        seg_block[name] = pos // block
        moves.append((offs[seg], sizes[seg], pos))
        pos += sizes[seg]
    w_fox, w_mix = _repack_w_in(w_in, offs[1] + LANES, tuple(moves), pos)
    w_brs = [bf(w_br_fox), bf(w_br_conv), bf(w_br_sgu), bf(w_br_dil)]
    w_out_b, w_down_b = bf(w_out), bf(w_down)
    w_pg_b, w_pp_b = bf(w_ple_gate), bf(w_ple_proj)
    sgu_bias = jnp.repeat(jnp.swapaxes(sgu_b, 1, 2), SGU_GROUP_DIM, axis=2)
    sgu_norm = sgu_norm_g[:, None, :]
    p_tok = p.reshape(depth, T, p.shape[-1])
    row_vec = lambda g: g[None, :]

    tables = _rope_tables(positions, tm)
    xf = x.reshape(T, D)
    h = _rmsnorm(xf, row_vec(norm_mix_g[0]), tm)
    out = None
    for i in range(depth):
        (qa, ka, va), qkv = _attn_proj(h, w_fox, b_f, w_mix, seg_block["dil"], tables, i, B, S, tm_attn)
        o_a = _fox_attention(qa.reshape(B, S, -1), ka.reshape(B, S, -1), va.reshape(B, S, -1),
                             B, S, tq).reshape(T, FOX_WIDTH)
        o_b, o_c = _conv_sgu_branches(h, w_mix, seg_block["conv"], seg_block["sgu"], shortconv_w, sgu_norm, sgu_w,
                                      sgu_bias, i, S, tm)
        outs, lses = [], []
        for g, (window, dil) in enumerate(DIL_PATTERNS):
            assert window // dil == DIL_SPAN
            o_g, l_g = _dil_attention(*qkv[3 * g:3 * g + 3], dil)
            outs.append(o_g)
            lses.append(l_g)
        o_d = _dil_merge(outs, lses, B, S, tm)

        merged = _gated_merge(h, (o_a, o_b, o_c, o_d), w_mix, seg_block["gate"], w_brs, i, tm_big, tn_merge)
        xf, h = _residual_proj(merged, w_out_b, xf, row_vec(norm_ffn_g[i]), i, tm_wide)

        act = _ffn_up(h, w_up, ffn_conv_w, i, tm_big, tn_ffn, S // tm_big)
        xf, h = _residual_proj(act, w_down_b, xf, row_vec(norm_ple_g[i]), i, tm_down)

        final = i == depth - 1
        g_next = final_norm_g if final else norm_mix_g[i + 1]
        res = _ple(h, w_pg_b, p_tok, w_pp_b, xf, row_vec(g_next), i, tm_wide, final)
        if final:
            out = res[0]
        else:
            xf, h = res
    return out.reshape(B, S, D)
```

```python
import functools

import jax
import jax.numpy as jnp
from jax import lax
from jax.experimental import pallas as pl
from jax.experimental.pallas import tpu as pltpu

F32 = jnp.float32
BF16 = jnp.bfloat16

HEAD_DIM = 64
EPS = 1e-6
FOX_HEADS = 8
FOX_WIDTH = FOX_HEADS * HEAD_DIM
CONV_WIDTH = 512
CONV_TAPS = 3
SGU_GROUPS = 4
SGU_GROUP_DIM = 128
SGU_WIDTH = SGU_GROUPS * SGU_GROUP_DIM
SGU_CHUNK = 128
DIL_PATTERNS = ((128, 1), (512, 4), (2048, 16))
DIL_HEADS_PER_GROUP = 4
DIL_HEADS = DIL_HEADS_PER_GROUP * len(DIL_PATTERNS)
DIL_WIDTH = DIL_HEADS * HEAD_DIM
DIL_OUT = DIL_HEADS_PER_GROUP * HEAD_DIM
DIL_SPAN = 128
ROPE_THETA = 500000.0
ROPE_DIM = HEAD_DIM // 4
N_BRANCH = 4

LANES = 128
SUBLANES = 8
BF16_ROWS = 16
HEADS_PER_SLAB = 4
SLAB = HEADS_PER_SLAB * HEAD_DIM
VMEM_LIMIT = 56 * 1024 * 1024
NEG = -1e30
QK_SCALE = HEAD_DIM ** -0.5


def _params(n_axes):
    return pltpu.CompilerParams(dimension_semantics=("arbitrary",) * n_axes, vmem_limit_bytes=VMEM_LIMIT)


def _resident(shape, index=None):
    index = (0,) * len(shape) if index is None else index
    return pl.BlockSpec(shape, lambda *_: index, pipeline_mode=pl.Buffered(1))


def _layer_weight(layer, rows, cols):
    return _resident((None, rows, cols), (layer, 0, 0))


def _tile(n, pref):
    t = min(n, pref)
    assert n % t == 0, (n, t)
    return t


def _rms(x, g):
    var = jnp.mean(x * x, axis=-1, keepdims=True)
    return x * lax.rsqrt(var + EPS) * g


def _dot(a, b):
    return jnp.dot(a, b, preferred_element_type=F32)


def _dot_nt(a, b):
    return lax.dot_general(a, b, (((1,), (1,)), ((), ())), preferred_element_type=F32)


def _causal_conv3(z, w, prev):
    y = w[0:1] * pltpu.roll(z, 2, 0) + w[1:2] * pltpu.roll(z, 1, 0) + w[2:3] * z
    zh = z[0:BF16_ROWS]
    row = lax.broadcasted_iota(jnp.int32, zh.shape, 0)
    p1 = prev[SUBLANES - 1:SUBLANES]
    p2 = prev[SUBLANES - 2:SUBLANES - 1]
    z1 = jnp.where(row == 0, p1, pltpu.roll(zh, 1, 0))
    z2 = jnp.where(row == 0, p2, jnp.where(row == 1, p1, pltpu.roll(zh, 2, 0)))
    yh = w[0:1] * z2 + w[1:2] * z1 + w[2:3] * zh
    return jnp.concatenate([yh, y[BF16_ROWS:]], axis=0)


def _rmsnorm_kernel(x_ref, g_ref, h_ref):
    h_ref[...] = _rms(x_ref[...], g_ref[...]).astype(h_ref.dtype)


def _rmsnorm(x, g, tm):
    T, D = x.shape
    return pl.pallas_call(
        _rmsnorm_kernel,
        out_shape=jax.ShapeDtypeStruct((T, D), BF16),
        grid=(T // tm,),
        in_specs=[pl.BlockSpec((tm, D), lambda i: (i, 0)), _resident((1, D))],
        out_specs=pl.BlockSpec((tm, D), lambda i: (i, 0)),
        compiler_params=_params(1),
        name="rmsnorm",
    )(x, g)


def _rope_table_kernel(pos_ref, inv_ref, c_ref, s1_ref, s2_ref):
    ang = pos_ref[...] * inv_ref[...]
    cos = jnp.cos(ang)
    sin = jnp.sin(ang)
    lane = lax.broadcasted_iota(jnp.int32, ang.shape, 1) % HEAD_DIM
    half = ROPE_DIM // 2
    c_ref[...] = jnp.where(lane < ROPE_DIM, cos, 1.0)
    s1_ref[...] = jnp.where((lane >= half) & (lane < ROPE_DIM), sin, 0.0)
    s2_ref[...] = jnp.where(lane < half, -sin, 0.0)


def _rope_tables(positions, tm):
    T = positions.size
    half = ROPE_DIM // 2
    inv = ROPE_THETA ** (-jnp.arange(half, dtype=F32) * (2.0 / ROPE_DIM))
    lane = jnp.arange(LANES) % HEAD_DIM
    inv_lane = jnp.where(lane < ROPE_DIM, inv[lane % half], 0.0).astype(F32)[None, :]
    posf = jnp.broadcast_to(positions.reshape(T, 1).astype(F32), (T, LANES))
    spec = pl.BlockSpec((tm, LANES), lambda i: (i, 0))
    return pl.pallas_call(
        _rope_table_kernel,
        out_shape=[jax.ShapeDtypeStruct((T, LANES), F32)] * 3,
        grid=(T // tm,),
        in_specs=[spec, _resident((1, LANES))],
        out_specs=[spec] * 3,
        compiler_params=_params(1),
        name="rope_tables",
    )(posf, inv_lane)


def _repack_kernel(wt_ref, fox_ref, mix_ref, *, fox_width, moves, mix_width):
    def rows_of(src, width):
        return wt_ref[src:src + width, :].T.astype(BF16)

    fox_ref[...] = rows_of(0, fox_width)
    end = 0
    for src, width, dst in moves:
        if dst > end:
            mix_ref[:, end:dst] = jnp.zeros((mix_ref.shape[0], dst - end), BF16)
        mix_ref[:, dst:dst + width] = rows_of(src, width)
        end = dst + width
    assert end == mix_width


def _repack_w_in(w_in, fox_width, moves, mix_width):
    depth, D, d_in = w_in.shape
    assert all(src % SUBLANES == 0 for src, _, _ in moves)
    tr = _tile(D, LANES)
    return pl.pallas_call(
        functools.partial(_repack_kernel, fox_width=fox_width, moves=moves, mix_width=mix_width),
        out_shape=[jax.ShapeDtypeStruct((depth, D, fox_width), BF16),
                   jax.ShapeDtypeStruct((depth, D, mix_width), BF16)],
        grid=(depth, D // tr),
        in_specs=[pl.BlockSpec((None, d_in, tr), lambda l, r: (l, 0, r))],
        out_specs=[pl.BlockSpec((None, tr, fox_width), lambda l, r: (l, r, 0)),
                   pl.BlockSpec((None, tr, mix_width), lambda l, r: (l, r, 0))],
        compiler_params=_params(2),
        name="repack_w_in",
    )(jnp.swapaxes(w_in, 1, 2))


def _fox_proj_kernel(h_ref, w_ref, wf_ref, bf_ref, q_ref, k_ref, v_ref, carry_ref, *, tiles_per_seq):
    @pl.when(pl.program_id(0) % tiles_per_seq == 0)
    def _():
        carry_ref[...] = jnp.zeros_like(carry_ref)

    h = h_ref[...]
    W = FOX_WIDTH
    q_ref[...] = (_dot(h, w_ref[:, 0:W]) * QK_SCALE).astype(BF16)
    k = _dot(h, w_ref[:, W:2 * W]).astype(BF16)
    v = _dot(h, w_ref[:, 2 * W:3 * W]).astype(BF16)

    x = jax.nn.log_sigmoid(_dot(h, wf_ref[...]) + bf_ref[...])
    tm = x.shape[0]
    row = lax.broadcasted_iota(jnp.int32, x.shape, 0)
    lane = lax.broadcasted_iota(jnp.int32, x.shape, 1)
    s = 1
    while s < tm:
        x = x + jnp.where(row >= s, pltpu.roll(x, s, 0), 0.0)
        s *= 2
    x = x + carry_ref[0:1, :]
    carry_ref[...] = jnp.broadcast_to(x[tm - 1:tm, :], carry_ref.shape)

    x = jnp.where(lane < FOX_HEADS, x, 0.0)
    hi = x.astype(BF16).astype(F32)
    mid = (x - hi).astype(BF16).astype(F32)
    lo = (x - hi - mid).astype(BF16).astype(F32)
    f_lanes = (hi + pltpu.roll(mid, FOX_HEADS, 1) + pltpu.roll(lo, 2 * FOX_HEADS, 1)).astype(BF16)
    one_lane = jnp.where(lane == 0, 1.0, 0.0).astype(BF16)
    for j in range(FOX_WIDTH // LANES):
        k_ref[:, j * SLAB:j * SLAB + LANES] = k[:, j * LANES:(j + 1) * LANES]
        k_ref[:, j * SLAB + LANES:(j + 1) * SLAB] = f_lanes
        v_ref[:, j * SLAB:j * SLAB + LANES] = v[:, j * LANES:(j + 1) * LANES]
        v_ref[:, j * SLAB + LANES:(j + 1) * SLAB] = one_lane


def _conv_branch_kernel(h_ref, w_ref, cw_ref, o_ref, prev_ref, *, tiles_per_seq):
    @pl.when(pl.program_id(0) % tiles_per_seq == 0)
    def _():
        prev_ref[...] = jnp.zeros_like(prev_ref)

    h = h_ref[...]
    W = CONV_WIDTH
    tm = h.shape[0]
    xb = _dot(h, w_ref[:, 0:W])
    gate_b = _dot(h, w_ref[:, W:2 * W])
    gate_c = _dot(h, w_ref[:, 2 * W:3 * W])
    z = gate_c * xb
    y = _causal_conv3(z, cw_ref[...], prev_ref[...])
    prev_ref[...] = z[tm - SUBLANES:, :]
    o_ref[...] = (gate_b * y).astype(BF16)


def _sgu_branch_kernel(h_ref, w_ref, ng_ref, ws_ref, bias_ref, o_ref):
    h = h_ref[...]
    W = SGU_WIDTH
    C = SGU_CHUNK
    G = SGU_GROUP_DIM
    u = jax.nn.gelu(_dot(h, w_ref[:, 0:W]))
    v = _rms(jax.nn.gelu(_dot(h, w_ref[:, W:2 * W])), ng_ref[...]).astype(BF16)
    row = lax.broadcasted_iota(jnp.int32, (C, C), 0)
    col = lax.broadcasted_iota(jnp.int32, (C, C), 1)
    for g in range(SGU_GROUPS):
        w_tril = jnp.where(col <= row, ws_ref[g], 0.0).astype(BF16)
        cols = slice(g * G, (g + 1) * G)
        for c in range(h.shape[0] // C):
            rows = slice(c * C, (c + 1) * C)
            mixed = _dot(w_tril, v[rows, cols]) + bias_ref[:, cols]
            o_ref[rows, cols] = (u[rows, cols] * mixed).astype(BF16)


def _dil_proj_kernel(h_ref, w_ref, c_ref, s1_ref, s2_ref, *refs):
    n_groups = len(DIL_PATTERNS)
    out_refs = refs[:3 * n_groups]
    stage_ref = refs[3 * n_groups]
    h = h_ref[...]
    tm = h.shape[0]
    cos = c_ref[...]
    s1 = s1_ref[...]
    s2 = s2_ref[...]
    shift = ROPE_DIM // 2
    slot = 0
    for which, rope, scale in ((0, True, QK_SCALE), (1, True, None), (2, False, None)):
        for g, (_, d) in enumerate(DIL_PATTERNS):
            col = which * DIL_WIDTH + g * SLAB
            x = _dot(h, w_ref[:, col:col + SLAB])
            parts = []
            for j in range(SLAB // LANES):
                xj = x[:, j * LANES:(j + 1) * LANES]
                if rope:
                    xj = xj * cos + pltpu.roll(xj, shift, 1) * s1 + pltpu.roll(xj, LANES - shift, 1) * s2
                if scale is not None:
                    xj = xj * scale
                parts.append(xj)
            out = out_refs[g * 3 + which]
            for j, xj in enumerate(parts):
                lanes = slice(j * LANES, (j + 1) * LANES)
                if d == 1:
                    out[0, 0, :, lanes] = xj.astype(BF16)
                else:
                    stage_ref[slot] = xj
                    for r in range(d):
                        out[0, r, :, lanes] = stage_ref[slot, pl.ds(r, tm // d, stride=d), :].astype(BF16)
                    slot += 1


N_FOX_PROJ_IN, N_FOX_PROJ_OUT = 3, 3
N_CONV_IN, N_SGU_IN = 2, 4


def _attn_proj_kernel(h_ref, *refs, tiles_per_seq):
    n_dil_out = 3 * len(DIL_PATTERNS)
    fox_in, refs = refs[:N_FOX_PROJ_IN], refs[N_FOX_PROJ_IN:]
    dil_in, refs = refs[:4], refs[4:]
    fox_out, refs = refs[:N_FOX_PROJ_OUT], refs[N_FOX_PROJ_OUT:]
    dil_out, (carry_ref, stage_ref) = refs[:n_dil_out], refs[n_dil_out:]
    _fox_proj_kernel(h_ref, *fox_in, *fox_out, carry_ref, tiles_per_seq=tiles_per_seq)
    _dil_proj_kernel(h_ref, *dil_in, *dil_out, stage_ref)


def _attn_proj(h, w_fox, b_f, w_mix, dil_block, tables, layer, B, S, tm):
    T, D = h.shape
    W = FOX_WIDTH
    n_pairs = FOX_WIDTH // LANES
    tiles_per_seq = S // tm
    row = lambda n: pl.BlockSpec((tm, n), lambda i: (i, 0))
    out_shape = [jax.ShapeDtypeStruct((T, W), BF16)] + [jax.ShapeDtypeStruct((T, n_pairs * SLAB), BF16)] * 2
    out_specs = [row(W), row(n_pairs * SLAB), row(n_pairs * SLAB)]
    n_strided = 0
    for _, d in DIL_PATTERNS:
        assert (tm // d) % BF16_ROWS == 0
        n_strided += 3 * (SLAB // LANES) * (d > 1)
        for _ in range(3):
            out_shape.append(jax.ShapeDtypeStruct((B, d, S // d, SLAB), BF16))
            out_specs.append(pl.BlockSpec((1, d, tm // d, SLAB),
                                          lambda i: (i // tiles_per_seq, 0, i % tiles_per_seq, 0)))
    res = pl.pallas_call(
        functools.partial(_attn_proj_kernel, tiles_per_seq=tiles_per_seq),
        out_shape=out_shape,
        grid=(T // tm,),
        in_specs=[row(D), _resident((None, D, 3 * W), (layer, 0, 0)),
                  _resident((None, D, LANES), (layer, 0, 3 * W // LANES)), _layer_weight(layer, 1, LANES),
                  _resident((None, D, 3 * DIL_WIDTH), (layer, 0, dil_block)), row(LANES), row(LANES), row(LANES)],
        out_specs=out_specs,
        scratch_shapes=[pltpu.VMEM((SUBLANES, LANES), F32), pltpu.VMEM((n_strided, tm, LANES), F32)],
        compiler_params=_params(1),
        name="attn_proj",
    )(h, w_fox, w_fox, b_f, w_mix, *tables)
    return res[:3], res[3:]


def _conv_sgu_kernel(h_ref, *refs, tiles_per_seq):
    conv_in, refs = refs[:N_CONV_IN], refs[N_CONV_IN:]
    sgu_in, (ob_ref, oc_ref, prev_ref) = refs[:N_SGU_IN], refs[N_SGU_IN:]
    _conv_branch_kernel(h_ref, *conv_in, ob_ref, prev_ref, tiles_per_seq=tiles_per_seq)
    _sgu_branch_kernel(h_ref, *sgu_in, oc_ref)


def _conv_sgu_branches(h, w_mix, conv_block, sgu_block, conv_w, norm_g, w_s, bias_full, layer, S, tm):
    T, D = h.shape
    row = lambda n: pl.BlockSpec((tm, n), lambda i: (i, 0))
    return pl.pallas_call(
        functools.partial(_conv_sgu_kernel, tiles_per_seq=S // tm),
        out_shape=[jax.ShapeDtypeStruct((T, CONV_WIDTH), BF16), jax.ShapeDtypeStruct((T, SGU_WIDTH), BF16)],
        grid=(T // tm,),
        in_specs=[row(D), _resident((None, D, 3 * CONV_WIDTH), (layer, 0, conv_block)),
                  _layer_weight(layer, CONV_TAPS, CONV_WIDTH),
                  _resident((None, D, 2 * SGU_WIDTH), (layer, 0, sgu_block)), _layer_weight(layer, 1, SGU_WIDTH),
                  _resident((None, SGU_GROUPS, SGU_CHUNK, SGU_CHUNK), (layer, 0, 0, 0)),
                  _layer_weight(layer, SGU_CHUNK, SGU_WIDTH)],
        out_specs=[row(CONV_WIDTH), row(SGU_WIDTH)],
        scratch_shapes=[pltpu.VMEM((SUBLANES, CONV_WIDTH), F32)],
        compiler_params=_params(1),
        name="conv_sgu_branches",
    )(h, w_mix, conv_w, w_mix, norm_g, w_s, bias_full)


def _slab_head_masks(n_rows):
    lane = lax.broadcasted_iota(jnp.int32, (n_rows, SLAB), 1)
    return [(lane >= h * HEAD_DIM) & (lane < (h + 1) * HEAD_DIM) for h in range(HEADS_PER_SLAB)]


FOX_PAIRS_PER_STEP = 2


def _fox_attn_kernel(q_ref, k_ref, v_ref, o_ref, *, tq):
    group = pl.program_id(1)
    i = pl.program_id(2)
    q0 = pl.multiple_of(i * tq, tq)
    lane = lax.broadcasted_iota(jnp.int32, (tq, LANES), 1)
    rowi = lax.broadcasted_iota(jnp.int32, (2 * tq, tq), 0) % tq
    coli = lax.broadcasted_iota(jnp.int32, (2 * tq, tq), 1)

    def stacked_queries(c):
        q = q_ref[0, :, c * LANES:(c + 1) * LANES]
        zero = jnp.zeros_like(q)
        halves = []
        for e in range(2):
            head = 2 * (FOX_PAIRS_PER_STEP * group + c) + e
            f_pick = (lane == head) | (lane == head + FOX_HEADS) | (lane == head + 2 * FOX_HEADS)
            halves.append(jnp.concatenate(
                [jnp.where(lane // HEAD_DIM == e, q, zero), jnp.where(f_pick, -1.0, 0.0).astype(BF16)], axis=1))
        return jnp.concatenate(halves, axis=0)

    qs = [stacked_queries(c) for c in range(FOX_PAIRS_PER_STEP)]

    def block(k0, carry, diagonal):
        slabs = [slice(c * SLAB, (c + 1) * SLAB) for c in range(FOX_PAIRS_PER_STEP)]
        logits = [_dot_nt(qs[c], k_ref[0, pl.ds(k0, tq), slabs[c]]) for c in range(FOX_PAIRS_PER_STEP)]
        stats = []
        for (m, _), s in zip(carry, logits):
            if diagonal:
                s = jnp.where(coli <= rowi, s, NEG)
            m_new = jnp.maximum(m, jnp.max(s, axis=-1, keepdims=True))
            stats.append((m_new, jnp.exp(m - m_new), jnp.exp(s - m_new).astype(BF16)))
        return tuple((m_new, alpha * acc + _dot(p, v_ref[0, pl.ds(k0, tq), slabs[c]]))
                     for c, ((_, acc), (m_new, alpha, p)) in enumerate(zip(carry, stats)))

    init = tuple((jnp.full((2 * tq, 1), NEG, F32), jnp.zeros((2 * tq, SLAB), F32))
                 for _ in range(FOX_PAIRS_PER_STEP))
    carry = lax.fori_loop(0, i, lambda kb, c: block(pl.multiple_of(kb * tq, tq), c, False), init)
    for c, (_, acc) in enumerate(block(q0, carry, True)):
        out = acc[:, 0:LANES] / acc[:, LANES:LANES + 1]
        o_ref[0, :, c * LANES:(c + 1) * LANES] = jnp.where(lane // HEAD_DIM == 0, out[0:tq],
                                                           out[tq:2 * tq]).astype(BF16)


def _fox_attention(q, k_slabs, v_slabs, B, S, tq):
    n_groups = FOX_WIDTH // LANES // FOX_PAIRS_PER_STEP
    kv_spec = pl.BlockSpec((1, S, FOX_PAIRS_PER_STEP * SLAB), lambda b, g, i: (b, 0, g))
    q_spec = pl.BlockSpec((1, tq, FOX_PAIRS_PER_STEP * LANES), lambda b, g, i: (b, i, g))
    return pl.pallas_call(
        functools.partial(_fox_attn_kernel, tq=tq),
        out_shape=jax.ShapeDtypeStruct((B, S, FOX_WIDTH), BF16),
        grid=(B, n_groups, S // tq),
        in_specs=[q_spec, kv_spec, kv_spec],
        out_specs=q_spec,
        compiler_params=_params(3),
        name="fox_attention",
    )(q, k_slabs, v_slabs)


DIL_BLOCKS_PER_STEP = 8


def _dil_attn_kernel(q_ref, k_ref, v_ref, o_ref, lse_ref, *, n_blocks):
    P = DIL_SPAN
    H = HEADS_PER_SLAB
    masks = _slab_head_masks(P)

    def attend(r, q_rows, k_rows, valid):
        q = q_ref[r, q_rows, :]
        zero = jnp.zeros_like(q)
        qs = jnp.concatenate([jnp.where(masks[h], q, zero) for h in range(H)], axis=0)
        s = jnp.where(valid, _dot_nt(qs, k_ref[r, k_rows, :]), NEG)
        m = jnp.max(s, axis=-1, keepdims=True)
        p = jnp.exp(s - m)
        l = jnp.sum(p, axis=-1, keepdims=True)
        pv = _dot(p.astype(BF16), v_ref[r, k_rows, :]) / l
        lse = jnp.broadcast_to(m + jnp.log(l), (H * P, SLAB))
        o, ls = pv[0:P], lse[0:P]
        for h in range(1, H):
            o = jnp.where(masks[h], pv[h * P:(h + 1) * P], o)
            ls = jnp.where(masks[h], lse[h * P:(h + 1) * P], ls)
        o_ref[r, q_rows, :] = o
        lse_ref[r, q_rows, :] = ls

    qi = lax.broadcasted_iota(jnp.int32, (H * P, P), 0) % P
    ki = lax.broadcasted_iota(jnp.int32, (H * P, P), 1)
    qi2 = lax.broadcasted_iota(jnp.int32, (H * P, 2 * P), 0) % P
    ki2 = lax.broadcasted_iota(jnp.int32, (H * P, 2 * P), 1)
    valid2 = (ki2 >= qi2) & (ki2 <= qi2 + P)

    for r in range(q_ref.shape[0]):
        attend(r, pl.ds(0, P), pl.ds(0, P), ki <= qi)

        def body(n, _, r=r):
            start = pl.multiple_of(n * P, P)
            attend(r, pl.ds(start, P), pl.ds(pl.multiple_of(start - P, P), 2 * P), valid2)
            return 0

        lax.fori_loop(1, n_blocks, body, 0, unroll=True if n_blocks <= 8 else 2)


def _dil_attention(q, k, v, dilation):
    B, d, L, _ = q.shape
    assert d == dilation
    n_blocks = L // DIL_SPAN
    per_step = max(1, min(d, DIL_BLOCKS_PER_STEP // n_blocks))
    spec = pl.BlockSpec((None, per_step, L, SLAB), lambda b, r: (b, r, 0, 0))
    return pl.pallas_call(
        functools.partial(_dil_attn_kernel, n_blocks=n_blocks),
        out_shape=[jax.ShapeDtypeStruct((B, d, L, SLAB), F32)] * 2,
        grid=(B, d // per_step),
        in_specs=[spec] * 3,
        out_specs=[spec] * 2,
        compiler_params=_params(2),
        name=f"dil_attention_d{dilation}",
    )(q, k, v)


def _dil_merge_kernel(*refs):
    n_groups = len(DIL_PATTERNS)
    o_refs, l_refs = refs[:n_groups], refs[n_groups:2 * n_groups]
    out_ref, stage_ref = refs[2 * n_groups], refs[2 * n_groups + 1]
    tm = out_ref.shape[0]
    slot = 0
    outs, lses = [], []
    for src, dst in ((o_refs, outs), (l_refs, lses)):
        for (_, d), ref in zip(DIL_PATTERNS, src):
            if d == 1:
                dst.append(ref[0, 0])
                continue
            halves = []
            for j in range(SLAB // LANES):
                for r in range(d):
                    stage_ref[slot, pl.ds(r, tm // d, stride=d), :] = ref[0, r, :, j * LANES:(j + 1) * LANES]
                halves.append(stage_ref[slot])
                slot += 1
            dst.append(jnp.concatenate(halves, axis=1))
    m = functools.reduce(jnp.maximum, lses)
    es = [jnp.exp(l - m) for l in lses]
    num = es[0] * outs[0] + es[1] * outs[1] + es[2] * outs[2]
    out_ref[...] = (num / (es[0] + es[1] + es[2])).astype(BF16)


def _dil_merge(outs, lses, B, S, tm):
    T = B * S
    tiles_per_seq = S // tm
    specs = [pl.BlockSpec((1, d, tm // d, SLAB), lambda i: (i // tiles_per_seq, 0, i % tiles_per_seq, 0))
             for _, d in DIL_PATTERNS]
    n_strided = 2 * (SLAB // LANES) * sum(d > 1 for _, d in DIL_PATTERNS)
    return pl.pallas_call(
        _dil_merge_kernel,
        out_shape=jax.ShapeDtypeStruct((T, SLAB), BF16),
        grid=(T // tm,),
        in_specs=specs * 2,
        out_specs=pl.BlockSpec((tm, SLAB), lambda i: (i, 0)),
        scratch_shapes=[pltpu.VMEM((n_strided, tm, LANES), F32)],
        compiler_params=_params(1),
        name="dil_merge",
    )(*outs, *lses)


def _gated_merge_kernel(h_ref, oa_ref, ob_ref, oc_ref, od_ref, g0, g1, g2, g3, wa, wb, wc, wd, out_ref):
    h = h_ref[...]
    acc = None
    for o_ref, g_ref, w_ref in ((oa_ref, g0, wa), (ob_ref, g1, wb), (oc_ref, g2, wc), (od_ref, g3, wd)):
        term = jax.nn.sigmoid(_dot(h, g_ref[...])) * _dot(o_ref[...], w_ref[...])
        acc = term if acc is None else acc + term
    out_ref[...] = acc.astype(BF16)


def _gated_merge(h, branches, w_mix, gate_block, w_branches, layer, tm, tn):
    T, D = h.shape
    n_col = D // tn
    row = lambda n: pl.BlockSpec((tm, n), lambda j, i: (i, 0))
    gate_specs = [pl.BlockSpec((None, D, tn), functools.partial(lambda j, i, br: (layer, 0, gate_block + br * n_col + j), br=br))
                  for br in range(N_BRANCH)]
    w_specs = [pl.BlockSpec((None, w.shape[1], tn), lambda j, i: (layer, 0, j)) for w in w_branches]
    return pl.pallas_call(
        _gated_merge_kernel,
        out_shape=jax.ShapeDtypeStruct((T, D), BF16),
        grid=(n_col, T // tm),
        in_specs=[row(D)] + [row(o.shape[1]) for o in branches] + gate_specs + w_specs,
        out_specs=pl.BlockSpec((tm, tn), lambda j, i: (i, j)),
        compiler_params=_params(2),
        name="gated_merge",
    )(h, *branches, w_mix, w_mix, w_mix, w_mix, *w_branches)


def _residual_proj_kernel(a_ref, w_ref, x_ref, g_ref, xo_ref, h_ref):
    x = x_ref[...] + _dot(a_ref[...], w_ref[...])
    xo_ref[...] = x
    h_ref[...] = _rms(x, g_ref[...]).astype(h_ref.dtype)


def _residual_proj(a, w, x, g_next, layer, tm):
    T, D = x.shape
    K = a.shape[1]
    row = lambda n: pl.BlockSpec((tm, n), lambda i: (i, 0))
    return pl.pallas_call(
        _residual_proj_kernel,
        out_shape=[jax.ShapeDtypeStruct((T, D), F32), jax.ShapeDtypeStruct((T, D), BF16)],
        grid=(T // tm,),
        in_specs=[row(K), _layer_weight(layer, K, D), row(D), _resident((1, D))],
        out_specs=[row(D), row(D)],
        compiler_params=_params(1),
        name=f"residual_proj_k{K}",
    )(a, w, x, g_next)


FFN_CHUNK = SLAB
SHIFT_PAD = 16


def _shifted_rows(z, s_ref):
    tm = z.shape[0]
    half = tm // 2
    P = SHIFT_PAD
    s_ref[P - 2:P - 1, :] = s_ref[P + tm - 1:P + tm, :]
    s_ref[P - 4:P - 3, :] = s_ref[P + tm - 3:P + tm - 2, :]
    s_ref[pl.ds(P, half, stride=2), :] = z[:half]
    s_ref[pl.ds(P + 1, half, stride=2), :] = z[half:]
    s_ref[P - 1:P, :] = z[half - 1:half]
    s_ref[P - 3:P - 2, :] = z[half - 2:half - 1]
    z1 = jnp.concatenate([s_ref[pl.ds(P - 2, half, stride=2), :], s_ref[pl.ds(P - 1, half, stride=2), :]], axis=0)
    z2 = jnp.concatenate([s_ref[pl.ds(P - 4, half, stride=2), :], s_ref[pl.ds(P - 3, half, stride=2), :]], axis=0)
    return z1, z2


def _ffn_up_kernel(h_ref, wg_ref, wv_ref, cg_ref, cv_ref, act_ref, wg_bf, wv_bf, shift_ref, *, tiles_per_seq):
    i = pl.program_id(1)
    tm = h_ref.shape[0]

    @pl.when(i == 0)
    def _():
        wg_bf[...] = wg_ref[...].astype(BF16)
        wv_bf[...] = wv_ref[...].astype(BF16)

    @pl.when(i % tiles_per_seq == 0)
    def _():
        shift_ref[:, SHIFT_PAD + tm - SUBLANES:, :] = jnp.zeros((shift_ref.shape[0], SUBLANES, LANES), F32)

    h = h_ref[...]

    def conv(z, w_ref, col, buf):
        parts = []
        for j in range(FFN_CHUNK // LANES):
            zj = z[:, j * LANES:(j + 1) * LANES]
            z1, z2 = _shifted_rows(zj, shift_ref.at[buf + j])
            w = w_ref[:, col + j * LANES:col + (j + 1) * LANES]
            parts.append(w[0:1] * z2 + w[1:2] * z1 + w[2:3] * zj)
        return jnp.concatenate(parts, axis=1)

    lane_blocks = act_ref.shape[1] // LANES
    for c in range(act_ref.shape[1] // FFN_CHUNK):
        col = c * FFN_CHUNK
        cols = slice(col, col + FFN_CHUNK)
        up_gate = conv(_dot(h, wg_bf[:, cols]), cg_ref, col, col // LANES)
        up_val = conv(_dot(h, wv_bf[:, cols]), cv_ref, col, lane_blocks + col // LANES)
        act_ref[:, cols] = (jax.nn.silu(up_gate) * up_val).astype(BF16)


def _ffn_up(h, w_up, conv_w, layer, tm, tn, tiles_per_seq):
    T, D = h.shape
    d_ff = w_up.shape[2] // 2
    n_col = d_ff // tn
    return pl.pallas_call(
        functools.partial(_ffn_up_kernel, tiles_per_seq=tiles_per_seq),
        out_shape=jax.ShapeDtypeStruct((T, d_ff), BF16),
        grid=(n_col, T // tm),
        in_specs=[pl.BlockSpec((tm, D), lambda j, i: (i, 0)),
                  pl.BlockSpec((None, D, tn), lambda j, i: (layer, 0, j)),
                  pl.BlockSpec((None, D, tn), lambda j, i: (layer, 0, n_col + j)),
                  pl.BlockSpec((None, CONV_TAPS, tn), lambda j, i: (layer, 0, j)),
                  pl.BlockSpec((None, CONV_TAPS, tn), lambda j, i: (layer, 0, n_col + j))],
        out_specs=pl.BlockSpec((tm, tn), lambda j, i: (i, j)),
        scratch_shapes=[pltpu.VMEM((D, tn), BF16)] * 2 + [pltpu.VMEM((2 * tn // LANES, SHIFT_PAD + tm, LANES), F32)],
        compiler_params=_params(2),
        name="ffn_up",
    )(h, w_up, w_up, conv_w, conv_w)


def _ple_kernel(h_ref, wg_ref, p_ref, wp_ref, x_ref, g_ref, *out_refs, final):
    gate = jax.nn.sigmoid(_dot(h_ref[...], wg_ref[...]))
    x = x_ref[...] + gate * _dot(p_ref[...].astype(BF16), wp_ref[...])
    if not final:
        out_refs[0][...] = x
    out_refs[-1][...] = _rms(x, g_ref[...]).astype(out_refs[-1].dtype)


def _ple(h, w_gate, p, w_proj, x, g_next, layer, tm, final):
    T, D = x.shape
    P = p.shape[2]
    row = lambda n: pl.BlockSpec((tm, n), lambda i: (i, 0))
    if final:
        out_shape = [jax.ShapeDtypeStruct((T, D), F32)]
    else:
        out_shape = [jax.ShapeDtypeStruct((T, D), F32), jax.ShapeDtypeStruct((T, D), BF16)]
    return pl.pallas_call(
        functools.partial(_ple_kernel, final=final),
        out_shape=out_shape,
        grid=(T // tm,),
        in_specs=[row(D), _layer_weight(layer, D, D), pl.BlockSpec((None, tm, P), lambda i: (layer, i, 0)),
                  _layer_weight(layer, P, D), row(D), _resident((1, D))],
        out_specs=[row(D)] * len(out_shape),
        compiler_params=_params(1),
        name="ple_final" if final else "ple",
    )(h, w_gate, p, w_proj, x, g_next)


def kernel(x, p, positions, norm_mix_g, w_in, fox_forget_b, shortconv_w, sgu_norm_g, sgu_w, sgu_b, w_br_fox, w_br_conv, w_br_sgu, w_br_dil, w_out, norm_ffn_g, w_up, ffn_conv_w, w_down, norm_ple_g, w_ple_gate, w_ple_proj, final_norm_g):
    B, S, D = x.shape
    depth = w_in.shape[0]
    T = B * S
    d_ff = w_down.shape[1]

    tm = _tile(S, 1024)
    tm_attn = _tile(S, 512)
    tm_big = _tile(S, 1024)
    tm_wide = _tile(S, 512)
    tm_down = _tile(S, 256)
    tq = _tile(S, 512)
    tn_merge = _tile(D, 512)
    tn_ffn = _tile(d_ff, 512)

    sizes = (3 * FOX_WIDTH, FOX_HEADS, 3 * CONV_WIDTH, 2 * SGU_WIDTH, 3 * DIL_WIDTH, N_BRANCH * D)
    offs = [0]
    for n in sizes:
        offs.append(offs[-1] + n)

    bf = lambda w: w.astype(BF16)
    b_f = jnp.pad(fox_forget_b.astype(F32), ((0, 0), (0, LANES - FOX_HEADS)))[:, None, :]
    moves, seg_block, pos = [], {}, 0
    for name, seg, block in (("gate", 5, tn_merge), ("sgu", 3, sizes[3]), ("conv", 2, sizes[2]), ("dil", 4, sizes[4])):
        pos += -pos % block
        seg_block[name] = pos // block
        moves.append((offs[seg], sizes[seg], pos))
        pos += sizes[seg]
    w_fox, w_mix = _repack_w_in(w_in, offs[1] + LANES, tuple(moves), pos)
    w_brs = [bf(w_br_fox), bf(w_br_conv), bf(w_br_sgu), bf(w_br_dil)]
    w_out_b, w_down_b = bf(w_out), bf(w_down)
    w_pg_b, w_pp_b = bf(w_ple_gate), bf(w_ple_proj)
    sgu_bias = jnp.repeat(jnp.swapaxes(sgu_b, 1, 2), SGU_GROUP_DIM, axis=2)
    sgu_norm = sgu_norm_g[:, None, :]
    p_tok = p.reshape(depth, T, p.shape[-1])
    row_vec = lambda g: g[None, :]

    tables = _rope_tables(positions, tm)
    xf = x.reshape(T, D)
    h = _rmsnorm(xf, row_vec(norm_mix_g[0]), tm)
    out = None
    for i in range(depth):
        (qa, ka, va), qkv = _attn_proj(h, w_fox, b_f, w_mix, seg_block["dil"], tables, i, B, S, tm_attn)
        o_a = _fox_attention(qa.reshape(B, S, -1), ka.reshape(B, S, -1), va.reshape(B, S, -1),
                             B, S, tq).reshape(T, FOX_WIDTH)
        o_b, o_c = _conv_sgu_branches(h, w_mix, seg_block["conv"], seg_block["sgu"], shortconv_w, sgu_norm, sgu_w,
                                      sgu_bias, i, S, tm)
        outs, lses = [], []
        for g, (window, dil) in enumerate(DIL_PATTERNS):
            assert window // dil == DIL_SPAN
            o_g, l_g = _dil_attention(*qkv[3 * g:3 * g + 3], dil)
            outs.append(o_g)
            lses.append(l_g)
        o_d = _dil_merge(outs, lses, B, S, tm)

        merged = _gated_merge(h, (o_a, o_b, o_c, o_d), w_mix, seg_block["gate"], w_brs, i, tm_big, tn_merge)
        xf, h = _residual_proj(merged, w_out_b, xf, row_vec(norm_ffn_g[i]), i, tm_wide)

        act = _ffn_up(h, w_up, ffn_conv_w, i, tm_big, tn_ffn, S // tm_big)
        xf, h = _residual_proj(act, w_down_b, xf, row_vec(norm_ple_g[i]), i, tm_down)

        final = i == depth - 1
        g_next = final_norm_g if final else norm_mix_g[i + 1]
        res = _ple(h, w_pg_b, p_tok, w_pp_b, xf, row_vec(g_next), i, tm_wide, final)
        if final:
            out = res[0]
        else:
            xf, h = res
    return out.reshape(B, S, D)
```

```python
import functools

import jax
import jax.numpy as jnp
from jax import lax
from jax.experimental import pallas as pl
from jax.experimental.pallas import tpu as pltpu

F32 = jnp.float32
BF16 = jnp.bfloat16

HEAD_DIM = 64
EPS = 1e-6
FOX_HEADS = 8
FOX_WIDTH = FOX_HEADS * HEAD_DIM
CONV_WIDTH = 512
CONV_TAPS = 3
SGU_GROUPS = 4
SGU_GROUP_DIM = 128
SGU_WIDTH = SGU_GROUPS * SGU_GROUP_DIM
SGU_CHUNK = 128
DIL_PATTERNS = ((128, 1), (512, 4), (2048, 16))
DIL_HEADS_PER_GROUP = 4
DIL_HEADS = DIL_HEADS_PER_GROUP * len(DIL_PATTERNS)
DIL_WIDTH = DIL_HEADS * HEAD_DIM
DIL_OUT = DIL_HEADS_PER_GROUP * HEAD_DIM
DIL_SPAN = 128
ROPE_THETA = 500000.0
ROPE_DIM = HEAD_DIM // 4
N_BRANCH = 4

LANES = 128
SUBLANES = 8
BF16_ROWS = 16
HEADS_PER_SLAB = 4
SLAB = HEADS_PER_SLAB * HEAD_DIM
VMEM_LIMIT = 56 * 1024 * 1024
NEG = -1e30
QK_SCALE = HEAD_DIM ** -0.5


def _params(n_axes):
    return pltpu.CompilerParams(dimension_semantics=("arbitrary",) * n_axes, vmem_limit_bytes=VMEM_LIMIT)


def _resident(shape, index=None):
    index = (0,) * len(shape) if index is None else index
    return pl.BlockSpec(shape, lambda *_: index, pipeline_mode=pl.Buffered(1))


def _layer_weight(layer, rows, cols):
    return _resident((None, rows, cols), (layer, 0, 0))


def _tile(n, pref):
    t = min(n, pref)
    assert n % t == 0, (n, t)
    return t


def _rms(x, g):
    var = jnp.mean(x * x, axis=-1, keepdims=True)
    return x * lax.rsqrt(var + EPS) * g


def _dot(a, b):
    return jnp.dot(a, b, preferred_element_type=F32)


def _dot_nt(a, b):
    return lax.dot_general(a, b, (((1,), (1,)), ((), ())), preferred_element_type=F32)


def _causal_conv3(z, w, prev):
    y = w[0:1] * pltpu.roll(z, 2, 0) + w[1:2] * pltpu.roll(z, 1, 0) + w[2:3] * z
    zh = z[0:BF16_ROWS]
    row = lax.broadcasted_iota(jnp.int32, zh.shape, 0)
    p1 = prev[SUBLANES - 1:SUBLANES]
    p2 = prev[SUBLANES - 2:SUBLANES - 1]
    z1 = jnp.where(row == 0, p1, pltpu.roll(zh, 1, 0))
    z2 = jnp.where(row == 0, p2, jnp.where(row == 1, p1, pltpu.roll(zh, 2, 0)))
    yh = w[0:1] * z2 + w[1:2] * z1 + w[2:3] * zh
    return jnp.concatenate([yh, y[BF16_ROWS:]], axis=0)


def _rmsnorm_kernel(x_ref, g_ref, h_ref):
    h_ref[...] = _rms(x_ref[...], g_ref[...]).astype(h_ref.dtype)


def _rmsnorm(x, g, tm):
    T, D = x.shape
    return pl.pallas_call(
        _rmsnorm_kernel,
        out_shape=jax.ShapeDtypeStruct((T, D), BF16),
        grid=(T // tm,),
        in_specs=[pl.BlockSpec((tm, D), lambda i: (i, 0)), _resident((1, D))],
        out_specs=pl.BlockSpec((tm, D), lambda i: (i, 0)),
        compiler_params=_params(1),
        name="rmsnorm",
    )(x, g)


def _rope_table_kernel(pos_ref, inv_ref, c_ref, s1_ref, s2_ref):
    ang = pos_ref[...] * inv_ref[...]
    cos = jnp.cos(ang)
    sin = jnp.sin(ang)
    lane = lax.broadcasted_iota(jnp.int32, ang.shape, 1) % HEAD_DIM
    half = ROPE_DIM // 2
    c_ref[...] = jnp.where(lane < ROPE_DIM, cos, 1.0)
    s1_ref[...] = jnp.where((lane >= half) & (lane < ROPE_DIM), sin, 0.0)
    s2_ref[...] = jnp.where(lane < half, -sin, 0.0)


def _rope_tables(positions, tm):
    T = positions.size
    half = ROPE_DIM // 2
    inv = ROPE_THETA ** (-jnp.arange(half, dtype=F32) * (2.0 / ROPE_DIM))
    lane = jnp.arange(LANES) % HEAD_DIM
    inv_lane = jnp.where(lane < ROPE_DIM, inv[lane % half], 0.0).astype(F32)[None, :]
    posf = jnp.broadcast_to(positions.reshape(T, 1).astype(F32), (T, LANES))
    spec = pl.BlockSpec((tm, LANES), lambda i: (i, 0))
    return pl.pallas_call(
        _rope_table_kernel,
        out_shape=[jax.ShapeDtypeStruct((T, LANES), F32)] * 3,
        grid=(T // tm,),
        in_specs=[spec, _resident((1, LANES))],
        out_specs=[spec] * 3,
        compiler_params=_params(1),
        name="rope_tables",
    )(posf, inv_lane)


def _repack_kernel(wt_ref, fox_ref, mix_ref, *, fox_width, moves, mix_width):
    def rows_of(src, width):
        return wt_ref[src:src + width, :].T.astype(BF16)

    fox_ref[...] = rows_of(0, fox_width)
    end = 0
    for src, width, dst in moves:
        if dst > end:
            mix_ref[:, end:dst] = jnp.zeros((mix_ref.shape[0], dst - end), BF16)
        mix_ref[:, dst:dst + width] = rows_of(src, width)
        end = dst + width
    assert end == mix_width


def _repack_w_in(w_in, fox_width, moves, mix_width):
    depth, D, d_in = w_in.shape
    assert all(src % SUBLANES == 0 for src, _, _ in moves)
    tr = _tile(D, LANES)
    return pl.pallas_call(
        functools.partial(_repack_kernel, fox_width=fox_width, moves=moves, mix_width=mix_width),
        out_shape=[jax.ShapeDtypeStruct((depth, D, fox_width), BF16),
                   jax.ShapeDtypeStruct((depth, D, mix_width), BF16)],
        grid=(depth, D // tr),
        in_specs=[pl.BlockSpec((None, d_in, tr), lambda l, r: (l, 0, r))],
        out_specs=[pl.BlockSpec((None, tr, fox_width), lambda l, r: (l, r, 0)),
                   pl.BlockSpec((None, tr, mix_width), lambda l, r: (l, r, 0))],
        compiler_params=_params(2),
        name="repack_w_in",
    )(jnp.swapaxes(w_in, 1, 2))


def _fox_proj_kernel(h_ref, w_ref, wf_ref, bf_ref, q_ref, k_ref, v_ref, carry_ref, *, tiles_per_seq):
    @pl.when(pl.program_id(0) % tiles_per_seq == 0)
    def _():
        carry_ref[...] = jnp.zeros_like(carry_ref)

    h = h_ref[...]
    W = FOX_WIDTH
    q_ref[...] = (_dot(h, w_ref[:, 0:W]) * QK_SCALE).astype(BF16)
    k = _dot(h, w_ref[:, W:2 * W]).astype(BF16)
    v = _dot(h, w_ref[:, 2 * W:3 * W]).astype(BF16)

    x = jax.nn.log_sigmoid(_dot(h, wf_ref[...]) + bf_ref[...])
    tm = x.shape[0]
    row = lax.broadcasted_iota(jnp.int32, x.shape, 0)
    lane = lax.broadcasted_iota(jnp.int32, x.shape, 1)
    s = 1
    while s < tm:
        x = x + jnp.where(row >= s, pltpu.roll(x, s, 0), 0.0)
        s *= 2
    x = x + carry_ref[0:1, :]
    carry_ref[...] = jnp.broadcast_to(x[tm - 1:tm, :], carry_ref.shape)

    x = jnp.where(lane < FOX_HEADS, x, 0.0)
    hi = x.astype(BF16).astype(F32)
    mid = (x - hi).astype(BF16).astype(F32)
    lo = (x - hi - mid).astype(BF16).astype(F32)
    f_lanes = (hi + pltpu.roll(mid, FOX_HEADS, 1) + pltpu.roll(lo, 2 * FOX_HEADS, 1)).astype(BF16)
    one_lane = jnp.where(lane == 0, 1.0, 0.0).astype(BF16)
    for j in range(FOX_WIDTH // LANES):
        k_ref[:, j * SLAB:j * SLAB + LANES] = k[:, j * LANES:(j + 1) * LANES]
        k_ref[:, j * SLAB + LANES:(j + 1) * SLAB] = f_lanes
        v_ref[:, j * SLAB:j * SLAB + LANES] = v[:, j * LANES:(j + 1) * LANES]
        v_ref[:, j * SLAB + LANES:(j + 1) * SLAB] = one_lane


def _conv_branch_kernel(h_ref, w_ref, cw_ref, o_ref, prev_ref, *, tiles_per_seq):
    @pl.when(pl.program_id(0) % tiles_per_seq == 0)
    def _():
        prev_ref[...] = jnp.zeros_like(prev_ref)

    h = h_ref[...]
    W = CONV_WIDTH
    tm = h.shape[0]
    xb = _dot(h, w_ref[:, 0:W])
    gate_b = _dot(h, w_ref[:, W:2 * W])
    gate_c = _dot(h, w_ref[:, 2 * W:3 * W])
    z = gate_c * xb
    y = _causal_conv3(z, cw_ref[...], prev_ref[...])
    prev_ref[...] = z[tm - SUBLANES:, :]
    o_ref[...] = (gate_b * y).astype(BF16)


def _sgu_branch_kernel(h_ref, w_ref, ng_ref, ws_ref, bias_ref, o_ref):
    h = h_ref[...]
    W = SGU_WIDTH
    C = SGU_CHUNK
    G = SGU_GROUP_DIM
    u = jax.nn.gelu(_dot(h, w_ref[:, 0:W]))
    v = _rms(jax.nn.gelu(_dot(h, w_ref[:, W:2 * W])), ng_ref[...]).astype(BF16)
    row = lax.broadcasted_iota(jnp.int32, (C, C), 0)
    col = lax.broadcasted_iota(jnp.int32, (C, C), 1)
    for g in range(SGU_GROUPS):
        w_tril = jnp.where(col <= row, ws_ref[g], 0.0).astype(BF16)
        cols = slice(g * G, (g + 1) * G)
        for c in range(h.shape[0] // C):
            rows = slice(c * C, (c + 1) * C)
            mixed = _dot(w_tril, v[rows, cols]) + bias_ref[:, cols]
            o_ref[rows, cols] = (u[rows, cols] * mixed).astype(BF16)


def _dil_proj_kernel(h_ref, w_ref, c_ref, s1_ref, s2_ref, *refs):
    n_groups = len(DIL_PATTERNS)
    out_refs = refs[:3 * n_groups]
    stage_ref = refs[3 * n_groups]
    h = h_ref[...]
    tm = h.shape[0]
    cos = c_ref[...]
    s1 = s1_ref[...]
    s2 = s2_ref[...]
    shift = ROPE_DIM // 2
    slot = 0
    for which, rope, scale in ((0, True, QK_SCALE), (1, True, None), (2, False, None)):
        for g, (_, d) in enumerate(DIL_PATTERNS):
            col = which * DIL_WIDTH + g * SLAB
            x = _dot(h, w_ref[:, col:col + SLAB])
            parts = []
            for j in range(SLAB // LANES):
                xj = x[:, j * LANES:(j + 1) * LANES]
                if rope:
                    xj = xj * cos + pltpu.roll(xj, shift, 1) * s1 + pltpu.roll(xj, LANES - shift, 1) * s2
                if scale is not None:
                    xj = xj * scale
                parts.append(xj)
            out = out_refs[g * 3 + which]
            for j, xj in enumerate(parts):
                lanes = slice(j * LANES, (j + 1) * LANES)
                if d == 1:
                    out[0, 0, :, lanes] = xj.astype(BF16)
                else:
                    stage_ref[slot] = xj
                    for r in range(d):
                        out[0, r, :, lanes] = stage_ref[slot, pl.ds(r, tm // d, stride=d), :].astype(BF16)
                    slot += 1


N_FOX_PROJ_IN, N_FOX_PROJ_OUT = 3, 3
N_CONV_IN, N_SGU_IN = 2, 4


def _attn_proj_kernel(h_ref, *refs, tiles_per_seq):
    n_dil_out = 3 * len(DIL_PATTERNS)
    fox_in, refs = refs[:N_FOX_PROJ_IN], refs[N_FOX_PROJ_IN:]
    dil_in, refs = refs[:4], refs[4:]
    fox_out, refs = refs[:N_FOX_PROJ_OUT], refs[N_FOX_PROJ_OUT:]
    dil_out, (carry_ref, stage_ref) = refs[:n_dil_out], refs[n_dil_out:]
    _fox_proj_kernel(h_ref, *fox_in, *fox_out, carry_ref, tiles_per_seq=tiles_per_seq)
    _dil_proj_kernel(h_ref, *dil_in, *dil_out, stage_ref)


def _attn_proj(h, w_fox, b_f, w_mix, dil_block, tables, layer, B, S, tm):
    T, D = h.shape
    W = FOX_WIDTH
    n_pairs = FOX_WIDTH // LANES
    tiles_per_seq = S // tm
    row = lambda n: pl.BlockSpec((tm, n), lambda i: (i, 0))
    out_shape = [jax.ShapeDtypeStruct((T, W), BF16)] + [jax.ShapeDtypeStruct((T, n_pairs * SLAB), BF16)] * 2
    out_specs = [row(W), row(n_pairs * SLAB), row(n_pairs * SLAB)]
    n_strided = 0
    for _, d in DIL_PATTERNS:
        assert (tm // d) % BF16_ROWS == 0
        n_strided += 3 * (SLAB // LANES) * (d > 1)
        for _ in range(3):
            out_shape.append(jax.ShapeDtypeStruct((B, d, S // d, SLAB), BF16))
            out_specs.append(pl.BlockSpec((1, d, tm // d, SLAB),
                                          lambda i: (i // tiles_per_seq, 0, i % tiles_per_seq, 0)))
    res = pl.pallas_call(
        functools.partial(_attn_proj_kernel, tiles_per_seq=tiles_per_seq),
        out_shape=out_shape,
        grid=(T // tm,),
        in_specs=[row(D), _resident((None, D, 3 * W), (layer, 0, 0)),
                  _resident((None, D, LANES), (layer, 0, 3 * W // LANES)), _layer_weight(layer, 1, LANES),
                  _resident((None, D, 3 * DIL_WIDTH), (layer, 0, dil_block)), row(LANES), row(LANES), row(LANES)],
        out_specs=out_specs,
        scratch_shapes=[pltpu.VMEM((SUBLANES, LANES), F32), pltpu.VMEM((n_strided, tm, LANES), F32)],
        compiler_params=_params(1),
        name="attn_proj",
    )(h, w_fox, w_fox, b_f, w_mix, *tables)
    return res[:3], res[3:]


def _conv_sgu_kernel(h_ref, *refs, tiles_per_seq):
    conv_in, refs = refs[:N_CONV_IN], refs[N_CONV_IN:]
    sgu_in, (ob_ref, oc_ref, prev_ref) = refs[:N_SGU_IN], refs[N_SGU_IN:]
    _conv_branch_kernel(h_ref, *conv_in, ob_ref, prev_ref, tiles_per_seq=tiles_per_seq)
    _sgu_branch_kernel(h_ref, *sgu_in, oc_ref)


def _conv_sgu_branches(h, w_mix, conv_block, sgu_block, conv_w, norm_g, w_s, bias_full, layer, S, tm):
    T, D = h.shape
    row = lambda n: pl.BlockSpec((tm, n), lambda i: (i, 0))
    return pl.pallas_call(
        functools.partial(_conv_sgu_kernel, tiles_per_seq=S // tm),
        out_shape=[jax.ShapeDtypeStruct((T, CONV_WIDTH), BF16), jax.ShapeDtypeStruct((T, SGU_WIDTH), BF16)],
        grid=(T // tm,),
        in_specs=[row(D), _resident((None, D, 3 * CONV_WIDTH), (layer, 0, conv_block)),
                  _layer_weight(layer, CONV_TAPS, CONV_WIDTH),
                  _resident((None, D, 2 * SGU_WIDTH), (layer, 0, sgu_block)), _layer_weight(layer, 1, SGU_WIDTH),
                  _resident((None, SGU_GROUPS, SGU_CHUNK, SGU_CHUNK), (layer, 0, 0, 0)),
                  _layer_weight(layer, SGU_CHUNK, SGU_WIDTH)],
        out_specs=[row(CONV_WIDTH), row(SGU_WIDTH)],
        scratch_shapes=[pltpu.VMEM((SUBLANES, CONV_WIDTH), F32)],
        compiler_params=_params(1),
        name="conv_sgu_branches",
    )(h, w_mix, conv_w, w_mix, norm_g, w_s, bias_full)


def _slab_head_masks(n_rows):
    lane = lax.broadcasted_iota(jnp.int32, (n_rows, SLAB), 1)
    return [(lane >= h * HEAD_DIM) & (lane < (h + 1) * HEAD_DIM) for h in range(HEADS_PER_SLAB)]


FOX_PAIRS_PER_STEP = 2


def _fox_attn_kernel(q_ref, k_ref, v_ref, o_ref, *, tq):
    group = pl.program_id(1)
    i = pl.program_id(2)
    q0 = pl.multiple_of(i * tq, tq)
    lane = lax.broadcasted_iota(jnp.int32, (tq, LANES), 1)
    rowi = lax.broadcasted_iota(jnp.int32, (2 * tq, tq), 0) % tq
    coli = lax.broadcasted_iota(jnp.int32, (2 * tq, tq), 1)

    def stacked_queries(c):
        q = q_ref[0, :, c * LANES:(c + 1) * LANES]
        zero = jnp.zeros_like(q)
        halves = []
        for e in range(2):
            head = 2 * (FOX_PAIRS_PER_STEP * group + c) + e
            f_pick = (lane == head) | (lane == head + FOX_HEADS) | (lane == head + 2 * FOX_HEADS)
            halves.append(jnp.concatenate(
                [jnp.where(lane // HEAD_DIM == e, q, zero), jnp.where(f_pick, -1.0, 0.0).astype(BF16)], axis=1))
        return jnp.concatenate(halves, axis=0)

    qs = [stacked_queries(c) for c in range(FOX_PAIRS_PER_STEP)]

    def block(k0, carry, diagonal):
        slabs = [slice(c * SLAB, (c + 1) * SLAB) for c in range(FOX_PAIRS_PER_STEP)]
        logits = [_dot_nt(qs[c], k_ref[0, pl.ds(k0, tq), slabs[c]]) for c in range(FOX_PAIRS_PER_STEP)]
        stats = []
        for (m, _), s in zip(carry, logits):
            if diagonal:
                s = jnp.where(coli <= rowi, s, NEG)
            m_new = jnp.maximum(m, jnp.max(s, axis=-1, keepdims=True))
            stats.append((m_new, jnp.exp(m - m_new), jnp.exp(s - m_new).astype(BF16)))
        return tuple((m_new, alpha * acc + _dot(p, v_ref[0, pl.ds(k0, tq), slabs[c]]))
                     for c, ((_, acc), (m_new, alpha, p)) in enumerate(zip(carry, stats)))

    init = tuple((jnp.full((2 * tq, 1), NEG, F32), jnp.zeros((2 * tq, SLAB), F32))
                 for _ in range(FOX_PAIRS_PER_STEP))
    carry = lax.fori_loop(0, i, lambda kb, c: block(pl.multiple_of(kb * tq, tq), c, False), init)
    for c, (_, acc) in enumerate(block(q0, carry, True)):
        out = acc[:, 0:LANES] / acc[:, LANES:LANES + 1]
        o_ref[0, :, c * LANES:(c + 1) * LANES] = jnp.where(lane // HEAD_DIM == 0, out[0:tq],
                                                           out[tq:2 * tq]).astype(BF16)


def _fox_attention(q, k_slabs, v_slabs, B, S, tq):
    n_groups = FOX_WIDTH // LANES // FOX_PAIRS_PER_STEP
    kv_spec = pl.BlockSpec((1, S, FOX_PAIRS_PER_STEP * SLAB), lambda b, g, i: (b, 0, g))
    q_spec = pl.BlockSpec((1, tq, FOX_PAIRS_PER_STEP * LANES), lambda b, g, i: (b, i, g))
    return pl.pallas_call(
        functools.partial(_fox_attn_kernel, tq=tq),
        out_shape=jax.ShapeDtypeStruct((B, S, FOX_WIDTH), BF16),
        grid=(B, n_groups, S // tq),
        in_specs=[q_spec, kv_spec, kv_spec],
        out_specs=q_spec,
        compiler_params=_params(3),
        name="fox_attention",
    )(q, k_slabs, v_slabs)


DIL_BLOCKS_PER_STEP = 8


def _dil_attn_kernel(q_ref, k_ref, v_ref, o_ref, lse_ref, *, n_blocks):
    P = DIL_SPAN
    H = HEADS_PER_SLAB
    masks = _slab_head_masks(P)

    def attend(r, q_rows, k_rows, valid):
        q = q_ref[r, q_rows, :]
        zero = jnp.zeros_like(q)
        qs = jnp.concatenate([jnp.where(masks[h], q, zero) for h in range(H)], axis=0)
        s = jnp.where(valid, _dot_nt(qs, k_ref[r, k_rows, :]), NEG)
        m = jnp.max(s, axis=-1, keepdims=True)
        p = jnp.exp(s - m)
        l = jnp.sum(p, axis=-1, keepdims=True)
        pv = _dot(p.astype(BF16), v_ref[r, k_rows, :]) / l
        lse = jnp.broadcast_to(m + jnp.log(l), (H * P, SLAB))
        o, ls = pv[0:P], lse[0:P]
        for h in range(1, H):
            o = jnp.where(masks[h], pv[h * P:(h + 1) * P], o)
            ls = jnp.where(masks[h], lse[h * P:(h + 1) * P], ls)
        o_ref[r, q_rows, :] = o
        lse_ref[r, q_rows, :] = ls

    qi = lax.broadcasted_iota(jnp.int32, (H * P, P), 0) % P
    ki = lax.broadcasted_iota(jnp.int32, (H * P, P), 1)
    qi2 = lax.broadcasted_iota(jnp.int32, (H * P, 2 * P), 0) % P
    ki2 = lax.broadcasted_iota(jnp.int32, (H * P, 2 * P), 1)
    valid2 = (ki2 >= qi2) & (ki2 <= qi2 + P)

    for r in range(q_ref.shape[0]):
        attend(r, pl.ds(0, P), pl.ds(0, P), ki <= qi)

        def body(n, _, r=r):
            start = pl.multiple_of(n * P, P)
            attend(r, pl.ds(start, P), pl.ds(pl.multiple_of(start - P, P), 2 * P), valid2)
            return 0

        lax.fori_loop(1, n_blocks, body, 0, unroll=True if n_blocks <= 8 else 2)


def _dil_attention(q, k, v, dilation):
    B, d, L, _ = q.shape
    assert d == dilation
    n_blocks = L // DIL_SPAN
    per_step = max(1, min(d, DIL_BLOCKS_PER_STEP // n_blocks))
    spec = pl.BlockSpec((None, per_step, L, SLAB), lambda b, r: (b, r, 0, 0))
    return pl.pallas_call(
        functools.partial(_dil_attn_kernel, n_blocks=n_blocks),
        out_shape=[jax.ShapeDtypeStruct((B, d, L, SLAB), F32)] * 2,
        grid=(B, d // per_step),
        in_specs=[spec] * 3,
        out_specs=[spec] * 2,
        compiler_params=_params(2),
        name=f"dil_attention_d{dilation}",
    )(q, k, v)


def _dil_merge_kernel(*refs):
    n_groups = len(DIL_PATTERNS)
    o_refs, l_refs = refs[:n_groups], refs[n_groups:2 * n_groups]
    out_ref, stage_ref = refs[2 * n_groups], refs[2 * n_groups + 1]
    tm = out_ref.shape[0]
    slot = 0
    outs, lses = [], []
    for src, dst in ((o_refs, outs), (l_refs, lses)):
        for (_, d), ref in zip(DIL_PATTERNS, src):
            if d == 1:
                dst.append(ref[0, 0])
                continue
            halves = []
            for j in range(SLAB // LANES):
                for r in range(d):
                    stage_ref[slot, pl.ds(r, tm // d, stride=d), :] = ref[0, r, :, j * LANES:(j + 1) * LANES]
                halves.append(stage_ref[slot])
                slot += 1
            dst.append(jnp.concatenate(halves, axis=1))
    m = functools.reduce(jnp.maximum, lses)
    es = [jnp.exp(l - m) for l in lses]
    num = es[0] * outs[0] + es[1] * outs[1] + es[2] * outs[2]
    out_ref[...] = (num / (es[0] + es[1] + es[2])).astype(BF16)


def _dil_merge(outs, lses, B, S, tm):
    T = B * S
    tiles_per_seq = S // tm
    specs = [pl.BlockSpec((1, d, tm // d, SLAB), lambda i: (i // tiles_per_seq, 0, i % tiles_per_seq, 0))
             for _, d in DIL_PATTERNS]
    n_strided = 2 * (SLAB // LANES) * sum(d > 1 for _, d in DIL_PATTERNS)
    return pl.pallas_call(
        _dil_merge_kernel,
        out_shape=jax.ShapeDtypeStruct((T, SLAB), BF16),
        grid=(T // tm,),
        in_specs=specs * 2,
        out_specs=pl.BlockSpec((tm, SLAB), lambda i: (i, 0)),
        scratch_shapes=[pltpu.VMEM((n_strided, tm, LANES), F32)],
        compiler_params=_params(1),
        name="dil_merge",
    )(*outs, *lses)


def _gated_merge_kernel(h_ref, oa_ref, ob_ref, oc_ref, od_ref, g0, g1, g2, g3, wa, wb, wc, wd, out_ref):
    h = h_ref[...]
    acc = None
    for o_ref, g_ref, w_ref in ((oa_ref, g0, wa), (ob_ref, g1, wb), (oc_ref, g2, wc), (od_ref, g3, wd)):
        term = jax.nn.sigmoid(_dot(h, g_ref[...])) * _dot(o_ref[...], w_ref[...])
        acc = term if acc is None else acc + term
    out_ref[...] = acc.astype(BF16)


def _gated_merge(h, branches, w_mix, gate_block, w_branches, layer, tm, tn):
    T, D = h.shape
    n_col = D // tn
    row = lambda n: pl.BlockSpec((tm, n), lambda j, i: (i, 0))
    gate_specs = [pl.BlockSpec((None, D, tn), functools.partial(lambda j, i, br: (layer, 0, gate_block + br * n_col + j), br=br))
                  for br in range(N_BRANCH)]
    w_specs = [pl.BlockSpec((None, w.shape[1], tn), lambda j, i: (layer, 0, j)) for w in w_branches]
    return pl.pallas_call(
        _gated_merge_kernel,
        out_shape=jax.ShapeDtypeStruct((T, D), BF16),
        grid=(n_col, T // tm),
        in_specs=[row(D)] + [row(o.shape[1]) for o in branches] + gate_specs + w_specs,
        out_specs=pl.BlockSpec((tm, tn), lambda j, i: (i, j)),
        compiler_params=_params(2),
        name="gated_merge",
    )(h, *branches, w_mix, w_mix, w_mix, w_mix, *w_branches)


def _residual_proj_kernel(a_ref, w_ref, x_ref, g_ref, xo_ref, h_ref):
    x = x_ref[...] + _dot(a_ref[...], w_ref[...])
    xo_ref[...] = x
    h_ref[...] = _rms(x, g_ref[...]).astype(h_ref.dtype)


def _residual_proj(a, w, x, g_next, layer, tm):
    T, D = x.shape
    K = a.shape[1]
    row = lambda n: pl.BlockSpec((tm, n), lambda i: (i, 0))
    return pl.pallas_call(
        _residual_proj_kernel,
        out_shape=[jax.ShapeDtypeStruct((T, D), F32), jax.ShapeDtypeStruct((T, D), BF16)],
        grid=(T // tm,),
        in_specs=[row(K), _layer_weight(layer, K, D), row(D), _resident((1, D))],
        out_specs=[row(D), row(D)],
        compiler_params=_params(1),
        name=f"residual_proj_k{K}",
    )(a, w, x, g_next)


FFN_CHUNK = SLAB
SHIFT_PAD = 16


def _shifted_rows(z, s_ref):
    tm = z.shape[0]
    half = tm // 2
    P = SHIFT_PAD
    s_ref[P - 2:P - 1, :] = s_ref[P + tm - 1:P + tm, :]
    s_ref[P - 4:P - 3, :] = s_ref[P + tm - 3:P + tm - 2, :]
    s_ref[pl.ds(P, half, stride=2), :] = z[:half]
    s_ref[pl.ds(P + 1, half, stride=2), :] = z[half:]
    s_ref[P - 1:P, :] = z[half - 1:half]
    s_ref[P - 3:P - 2, :] = z[half - 2:half - 1]
    z1 = jnp.concatenate([s_ref[pl.ds(P - 2, half, stride=2), :], s_ref[pl.ds(P - 1, half, stride=2), :]], axis=0)
    z2 = jnp.concatenate([s_ref[pl.ds(P - 4, half, stride=2), :], s_ref[pl.ds(P - 3, half, stride=2), :]], axis=0)
    return z1, z2


def _ffn_up_kernel(h_ref, wg_ref, wv_ref, cg_ref, cv_ref, act_ref, wg_bf, wv_bf, shift_ref, *, tiles_per_seq):
    i = pl.program_id(1)
    tm = h_ref.shape[0]

    @pl.when(i == 0)
    def _():
        wg_bf[...] = wg_ref[...].astype(BF16)
        wv_bf[...] = wv_ref[...].astype(BF16)

    @pl.when(i % tiles_per_seq == 0)
    def _():
        shift_ref[:, SHIFT_PAD + tm - SUBLANES:, :] = jnp.zeros((shift_ref.shape[0], SUBLANES, LANES), F32)

    h = h_ref[...]

    def conv(z, w_ref, col, buf):
        parts = []
        for j in range(FFN_CHUNK // LANES):
            zj = z[:, j * LANES:(j + 1) * LANES]
            z1, z2 = _shifted_rows(zj, shift_ref.at[buf + j])
            w = w_ref[:, col + j * LANES:col + (j + 1) * LANES]
            parts.append(w[0:1] * z2 + w[1:2] * z1 + w[2:3] * zj)
        return jnp.concatenate(parts, axis=1)

    lane_blocks = act_ref.shape[1] // LANES
    for c in range(act_ref.shape[1] // FFN_CHUNK):
        col = c * FFN_CHUNK
        cols = slice(col, col + FFN_CHUNK)
        up_gate = conv(_dot(h, wg_bf[:, cols]), cg_ref, col, col // LANES)
        up_val = conv(_dot(h, wv_bf[:, cols]), cv_ref, col, lane_blocks + col // LANES)
        act_ref[:, cols] = (jax.nn.silu(up_gate) * up_val).astype(BF16)


def _ffn_up(h, w_up, conv_w, layer, tm, tn, tiles_per_seq):
    T, D = h.shape
    d_ff = w_up.shape[2] // 2
    n_col = d_ff // tn
    return pl.pallas_call(
        functools.partial(_ffn_up_kernel, tiles_per_seq=tiles_per_seq),
        out_shape=jax.ShapeDtypeStruct((T, d_ff), BF16),
        grid=(n_col, T // tm),
        in_specs=[pl.BlockSpec((tm, D), lambda j, i: (i, 0)),
                  pl.BlockSpec((None, D, tn), lambda j, i: (layer, 0, j)),
                  pl.BlockSpec((None, D, tn), lambda j, i: (layer, 0, n_col + j)),
                  pl.BlockSpec((None, CONV_TAPS, tn), lambda j, i: (layer, 0, j)),
                  pl.BlockSpec((None, CONV_TAPS, tn), lambda j, i: (layer, 0, n_col + j))],
        out_specs=pl.BlockSpec((tm, tn), lambda j, i: (i, j)),
        scratch_shapes=[pltpu.VMEM((D, tn), BF16)] * 2 + [pltpu.VMEM((2 * tn // LANES, SHIFT_PAD + tm, LANES), F32)],
        compiler_params=_params(2),
        name="ffn_up",
    )(h, w_up, w_up, conv_w, conv_w)


def _ffn_down_ple_kernel(a_ref, wd_ref, x_ref, gp_ref, wg_ref, p_ref, wp_ref, gn_ref, *out_refs, final):
    x = x_ref[...] + _dot(a_ref[...], wd_ref[...])
    h = _rms(x, gp_ref[...]).astype(BF16)
    gate = jax.nn.sigmoid(_dot(h, wg_ref[...]))
    x = x + gate * _dot(p_ref[...].astype(BF16), wp_ref[...])
    if not final:
        out_refs[0][...] = x
    out_refs[-1][...] = _rms(x, gn_ref[...]).astype(out_refs[-1].dtype)


def _ffn_down_ple(act, w_down, x, g_ple, w_gate, p, w_proj, g_next, layer, tm, final):
    T, D = x.shape
    K = act.shape[1]
    P = p.shape[2]
    row = lambda n: pl.BlockSpec((tm, n), lambda i: (i, 0))
    if final:
        out_shape = [jax.ShapeDtypeStruct((T, D), F32)]
    else:
        out_shape = [jax.ShapeDtypeStruct((T, D), F32), jax.ShapeDtypeStruct((T, D), BF16)]
    return pl.pallas_call(
        functools.partial(_ffn_down_ple_kernel, final=final),
        out_shape=out_shape,
        grid=(T // tm,),
        in_specs=[row(K), _layer_weight(layer, K, D), row(D), _resident((1, D)), _layer_weight(layer, D, D),
                  pl.BlockSpec((None, tm, P), lambda i: (layer, i, 0)), _layer_weight(layer, P, D),
                  _resident((1, D))],
        out_specs=[row(D)] * len(out_shape),
        compiler_params=_params(1),
        name="ffn_down_ple_final" if final else "ffn_down_ple",
    )(act, w_down, x, g_ple, w_gate, p, w_proj, g_next)


def kernel(x, p, positions, norm_mix_g, w_in, fox_forget_b, shortconv_w, sgu_norm_g, sgu_w, sgu_b, w_br_fox, w_br_conv, w_br_sgu, w_br_dil, w_out, norm_ffn_g, w_up, ffn_conv_w, w_down, norm_ple_g, w_ple_gate, w_ple_proj, final_norm_g):
    B, S, D = x.shape
    depth = w_in.shape[0]
    T = B * S
    d_ff = w_down.shape[1]

    tm = _tile(S, 1024)
    tm_attn = _tile(S, 512)
    tm_big = _tile(S, 1024)
    tm_wide = _tile(S, 512)
    tm_down = _tile(S, 256)
    tq = _tile(S, 512)
    tn_merge = _tile(D, 512)
    tn_ffn = _tile(d_ff, 512)

    sizes = (3 * FOX_WIDTH, FOX_HEADS, 3 * CONV_WIDTH, 2 * SGU_WIDTH, 3 * DIL_WIDTH, N_BRANCH * D)
    offs = [0]
    for n in sizes:
        offs.append(offs[-1] + n)

    bf = lambda w: w.astype(BF16)
    b_f = jnp.pad(fox_forget_b.astype(F32), ((0, 0), (0, LANES - FOX_HEADS)))[:, None, :]
    moves, seg_block, pos = [], {}, 0
    for name, seg, block in (("gate", 5, tn_merge), ("sgu", 3, sizes[3]), ("conv", 2, sizes[2]), ("dil", 4, sizes[4])):
        pos += -pos % block
        seg_block[name] = pos // block
        moves.append((offs[seg], sizes[seg], pos))
        pos += sizes[seg]
    w_fox, w_mix = _repack_w_in(w_in, offs[1] + LANES, tuple(moves), pos)
    w_brs = [bf(w_br_fox), bf(w_br_conv), bf(w_br_sgu), bf(w_br_dil)]
    w_out_b, w_down_b = bf(w_out), bf(w_down)
    w_pg_b, w_pp_b = bf(w_ple_gate), bf(w_ple_proj)
    sgu_bias = jnp.repeat(jnp.swapaxes(sgu_b, 1, 2), SGU_GROUP_DIM, axis=2)
    sgu_norm = sgu_norm_g[:, None, :]
    p_tok = p.reshape(depth, T, p.shape[-1])
    row_vec = lambda g: g[None, :]

    tables = _rope_tables(positions, tm)
    xf = x.reshape(T, D)
    h = _rmsnorm(xf, row_vec(norm_mix_g[0]), tm)
    out = None
    for i in range(depth):
        (qa, ka, va), qkv = _attn_proj(h, w_fox, b_f, w_mix, seg_block["dil"], tables, i, B, S, tm_attn)
        o_a = _fox_attention(qa.reshape(B, S, -1), ka.reshape(B, S, -1), va.reshape(B, S, -1),
                             B, S, tq).reshape(T, FOX_WIDTH)
        o_b, o_c = _conv_sgu_branches(h, w_mix, seg_block["conv"], seg_block["sgu"], shortconv_w, sgu_norm, sgu_w,
                                      sgu_bias, i, S, tm)
        outs, lses = [], []
        for g, (window, dil) in enumerate(DIL_PATTERNS):
            assert window // dil == DIL_SPAN
            o_g, l_g = _dil_attention(*qkv[3 * g:3 * g + 3], dil)
            outs.append(o_g)
            lses.append(l_g)
        o_d = _dil_merge(outs, lses, B, S, tm)

        merged = _gated_merge(h, (o_a, o_b, o_c, o_d), w_mix, seg_block["gate"], w_brs, i, tm_big, tn_merge)
        xf, h = _residual_proj(merged, w_out_b, xf, row_vec(norm_ffn_g[i]), i, tm_wide)

        act = _ffn_up(h, w_up, ffn_conv_w, i, tm_big, tn_ffn, S // tm_big)
        final = i == depth - 1
        g_next = final_norm_g if final else norm_mix_g[i + 1]
        res = _ffn_down_ple(act, w_down_b, xf, row_vec(norm_ple_g[i]), w_pg_b, p_tok, w_pp_b, row_vec(g_next), i,
                            tm_down, final)
        if final:
            out = res[0]
        else:
            xf, h = res
    return out.reshape(B, S, D)
```

```python
import functools

import jax
import jax.numpy as jnp
from jax import lax
from jax.experimental import pallas as pl
from jax.experimental.pallas import tpu as pltpu

F32 = jnp.float32
BF16 = jnp.bfloat16

HEAD_DIM = 64
EPS = 1e-6
FOX_HEADS = 8
FOX_WIDTH = FOX_HEADS * HEAD_DIM
CONV_WIDTH = 512
CONV_TAPS = 3
SGU_GROUPS = 4
SGU_GROUP_DIM = 128
SGU_WIDTH = SGU_GROUPS * SGU_GROUP_DIM
SGU_CHUNK = 128
DIL_PATTERNS = ((128, 1), (512, 4), (2048, 16))
DIL_HEADS_PER_GROUP = 4
DIL_HEADS = DIL_HEADS_PER_GROUP * len(DIL_PATTERNS)
DIL_WIDTH = DIL_HEADS * HEAD_DIM
DIL_OUT = DIL_HEADS_PER_GROUP * HEAD_DIM
DIL_SPAN = 128
ROPE_THETA = 500000.0
ROPE_DIM = HEAD_DIM // 4
N_BRANCH = 4

LANES = 128
SUBLANES = 8
BF16_ROWS = 16
HEADS_PER_SLAB = 4
SLAB = HEADS_PER_SLAB * HEAD_DIM
VMEM_LIMIT = 56 * 1024 * 1024
NEG = -1e30
QK_SCALE = HEAD_DIM ** -0.5


def _params(n_axes):
    return pltpu.CompilerParams(dimension_semantics=("arbitrary",) * n_axes, vmem_limit_bytes=VMEM_LIMIT)


def _resident(shape, index=None):
    index = (0,) * len(shape) if index is None else index
    return pl.BlockSpec(shape, lambda *_: index, pipeline_mode=pl.Buffered(1))


def _layer_weight(layer, rows, cols):
    return _resident((None, rows, cols), (layer, 0, 0))


def _tile(n, pref):
    t = min(n, pref)
    assert n % t == 0, (n, t)
    return t


def _rms(x, g):
    var = jnp.mean(x * x, axis=-1, keepdims=True)
    return x * lax.rsqrt(var + EPS) * g


def _dot(a, b):
    return jnp.dot(a, b, preferred_element_type=F32)


def _dot_nt(a, b):
    return lax.dot_general(a, b, (((1,), (1,)), ((), ())), preferred_element_type=F32)


def _causal_conv3(z, w, prev):
    y = w[0:1] * pltpu.roll(z, 2, 0) + w[1:2] * pltpu.roll(z, 1, 0) + w[2:3] * z
    zh = z[0:BF16_ROWS]
    row = lax.broadcasted_iota(jnp.int32, zh.shape, 0)
    p1 = prev[SUBLANES - 1:SUBLANES]
    p2 = prev[SUBLANES - 2:SUBLANES - 1]
    z1 = jnp.where(row == 0, p1, pltpu.roll(zh, 1, 0))
    z2 = jnp.where(row == 0, p2, jnp.where(row == 1, p1, pltpu.roll(zh, 2, 0)))
    yh = w[0:1] * z2 + w[1:2] * z1 + w[2:3] * zh
    return jnp.concatenate([yh, y[BF16_ROWS:]], axis=0)


def _rmsnorm_kernel(x_ref, g_ref, h_ref):
    h_ref[...] = _rms(x_ref[...], g_ref[...]).astype(h_ref.dtype)


def _rmsnorm(x, g, tm):
    T, D = x.shape
    return pl.pallas_call(
        _rmsnorm_kernel,
        out_shape=jax.ShapeDtypeStruct((T, D), BF16),
        grid=(T // tm,),
        in_specs=[pl.BlockSpec((tm, D), lambda i: (i, 0)), _resident((1, D))],
        out_specs=pl.BlockSpec((tm, D), lambda i: (i, 0)),
        compiler_params=_params(1),
        name="rmsnorm",
    )(x, g)


def _rope_table_kernel(pos_ref, inv_ref, c_ref, s1_ref, s2_ref):
    ang = pos_ref[...] * inv_ref[...]
    cos = jnp.cos(ang)
    sin = jnp.sin(ang)
    lane = lax.broadcasted_iota(jnp.int32, ang.shape, 1) % HEAD_DIM
    half = ROPE_DIM // 2
    c_ref[...] = jnp.where(lane < ROPE_DIM, cos, 1.0)
    s1_ref[...] = jnp.where((lane >= half) & (lane < ROPE_DIM), sin, 0.0)
    s2_ref[...] = jnp.where(lane < half, -sin, 0.0)


def _rope_tables(positions, tm):
    T = positions.size
    half = ROPE_DIM // 2
    inv = ROPE_THETA ** (-jnp.arange(half, dtype=F32) * (2.0 / ROPE_DIM))
    lane = jnp.arange(LANES) % HEAD_DIM
    inv_lane = jnp.where(lane < ROPE_DIM, inv[lane % half], 0.0).astype(F32)[None, :]
    posf = jnp.broadcast_to(positions.reshape(T, 1).astype(F32), (T, LANES))
    spec = pl.BlockSpec((tm, LANES), lambda i: (i, 0))
    return pl.pallas_call(
        _rope_table_kernel,
        out_shape=[jax.ShapeDtypeStruct((T, LANES), F32)] * 3,
        grid=(T // tm,),
        in_specs=[spec, _resident((1, LANES))],
        out_specs=[spec] * 3,
        compiler_params=_params(1),
        name="rope_tables",
    )(posf, inv_lane)


def _repack_kernel(wt_ref, fox_ref, mix_ref, *, fox_width, moves, mix_width):
    def rows_of(src, width):
        return wt_ref[src:src + width, :].T.astype(BF16)

    fox_ref[...] = rows_of(0, fox_width)
    end = 0
    for src, width, dst in moves:
        if dst > end:
            mix_ref[:, end:dst] = jnp.zeros((mix_ref.shape[0], dst - end), BF16)
        mix_ref[:, dst:dst + width] = rows_of(src, width)
        end = dst + width
    assert end == mix_width


def _repack_w_in(w_in, fox_width, moves, mix_width):
    depth, D, d_in = w_in.shape
    assert all(src % SUBLANES == 0 for src, _, _ in moves)
    tr = _tile(D, LANES)
    return pl.pallas_call(
        functools.partial(_repack_kernel, fox_width=fox_width, moves=moves, mix_width=mix_width),
        out_shape=[jax.ShapeDtypeStruct((depth, D, fox_width), BF16),
                   jax.ShapeDtypeStruct((depth, D, mix_width), BF16)],
        grid=(depth, D // tr),
        in_specs=[pl.BlockSpec((None, d_in, tr), lambda l, r: (l, 0, r))],
        out_specs=[pl.BlockSpec((None, tr, fox_width), lambda l, r: (l, r, 0)),
                   pl.BlockSpec((None, tr, mix_width), lambda l, r: (l, r, 0))],
        compiler_params=_params(2),
        name="repack_w_in",
    )(jnp.swapaxes(w_in, 1, 2))


def _fox_proj_kernel(h_ref, w_ref, wf_ref, bf_ref, q_ref, k_ref, v_ref, carry_ref, *, tiles_per_seq):
    @pl.when(pl.program_id(0) % tiles_per_seq == 0)
    def _():
        carry_ref[...] = jnp.zeros_like(carry_ref)

    h = h_ref[...]
    W = FOX_WIDTH
    q_ref[...] = (_dot(h, w_ref[:, 0:W]) * QK_SCALE).astype(BF16)
    k = _dot(h, w_ref[:, W:2 * W]).astype(BF16)
    v = _dot(h, w_ref[:, 2 * W:3 * W]).astype(BF16)

    x = jax.nn.log_sigmoid(_dot(h, wf_ref[...]) + bf_ref[...])
    tm = x.shape[0]
    row = lax.broadcasted_iota(jnp.int32, x.shape, 0)
    lane = lax.broadcasted_iota(jnp.int32, x.shape, 1)
    s = 1
    while s < tm:
        x = x + jnp.where(row >= s, pltpu.roll(x, s, 0), 0.0)
        s *= 2
    x = x + carry_ref[0:1, :]
    carry_ref[...] = jnp.broadcast_to(x[tm - 1:tm, :], carry_ref.shape)

    x = jnp.where(lane < FOX_HEADS, x, 0.0)
    hi = x.astype(BF16).astype(F32)
    mid = (x - hi).astype(BF16).astype(F32)
    lo = (x - hi - mid).astype(BF16).astype(F32)
    f_lanes = (hi + pltpu.roll(mid, FOX_HEADS, 1) + pltpu.roll(lo, 2 * FOX_HEADS, 1)).astype(BF16)
    one_lane = jnp.where(lane == 0, 1.0, 0.0).astype(BF16)
    for j in range(FOX_WIDTH // LANES):
        k_ref[:, j * SLAB:j * SLAB + LANES] = k[:, j * LANES:(j + 1) * LANES]
        k_ref[:, j * SLAB + LANES:(j + 1) * SLAB] = f_lanes
        v_ref[:, j * SLAB:j * SLAB + LANES] = v[:, j * LANES:(j + 1) * LANES]
        v_ref[:, j * SLAB + LANES:(j + 1) * SLAB] = one_lane


def _conv_branch_kernel(h_ref, w_ref, cw_ref, o_ref, prev_ref, *, tiles_per_seq):
    @pl.when(pl.program_id(0) % tiles_per_seq == 0)
    def _():
        prev_ref[...] = jnp.zeros_like(prev_ref)

    h = h_ref[...]
    W = CONV_WIDTH
    tm = h.shape[0]
    xb = _dot(h, w_ref[:, 0:W])
    gate_b = _dot(h, w_ref[:, W:2 * W])
    gate_c = _dot(h, w_ref[:, 2 * W:3 * W])
    z = gate_c * xb
    y = _causal_conv3(z, cw_ref[...], prev_ref[...])
    prev_ref[...] = z[tm - SUBLANES:, :]
    o_ref[...] = (gate_b * y).astype(BF16)


def _sgu_branch_kernel(h_ref, w_ref, ng_ref, ws_ref, bias_ref, o_ref):
    h = h_ref[...]
    W = SGU_WIDTH
    C = SGU_CHUNK
    G = SGU_GROUP_DIM
    u = jax.nn.gelu(_dot(h, w_ref[:, 0:W]))
    v = _rms(jax.nn.gelu(_dot(h, w_ref[:, W:2 * W])), ng_ref[...]).astype(BF16)
    row = lax.broadcasted_iota(jnp.int32, (C, C), 0)
    col = lax.broadcasted_iota(jnp.int32, (C, C), 1)
    for g in range(SGU_GROUPS):
        w_tril = jnp.where(col <= row, ws_ref[g], 0.0).astype(BF16)
        cols = slice(g * G, (g + 1) * G)
        for c in range(h.shape[0] // C):
            rows = slice(c * C, (c + 1) * C)
            mixed = _dot(w_tril, v[rows, cols]) + bias_ref[:, cols]
            o_ref[rows, cols] = (u[rows, cols] * mixed).astype(BF16)


def _dil_proj_kernel(h_ref, w_ref, c_ref, s1_ref, s2_ref, *refs):
    n_groups = len(DIL_PATTERNS)
    out_refs = refs[:3 * n_groups]
    stage_ref = refs[3 * n_groups]
    h = h_ref[...]
    tm = h.shape[0]
    cos = c_ref[...]
    s1 = s1_ref[...]
    s2 = s2_ref[...]
    shift = ROPE_DIM // 2
    slot = 0
    for which, rope, scale in ((0, True, QK_SCALE), (1, True, None), (2, False, None)):
        for g, (_, d) in enumerate(DIL_PATTERNS):
            col = which * DIL_WIDTH + g * SLAB
            x = _dot(h, w_ref[:, col:col + SLAB])
            parts = []
            for j in range(SLAB // LANES):
                xj = x[:, j * LANES:(j + 1) * LANES]
                if rope:
                    xj = xj * cos + pltpu.roll(xj, shift, 1) * s1 + pltpu.roll(xj, LANES - shift, 1) * s2
                if scale is not None:
                    xj = xj * scale
                parts.append(xj)
            out = out_refs[g * 3 + which]
            for j, xj in enumerate(parts):
                lanes = slice(j * LANES, (j + 1) * LANES)
                if d == 1:
                    out[0, 0, :, lanes] = xj.astype(BF16)
                else:
                    stage_ref[slot] = xj
                    for r in range(d):
                        out[0, r, :, lanes] = stage_ref[slot, pl.ds(r, tm // d, stride=d), :].astype(BF16)
                    slot += 1


N_FOX_PROJ_IN, N_FOX_PROJ_OUT = 3, 3
N_CONV_IN, N_SGU_IN = 2, 4


def _attn_proj_kernel(h_ref, *refs, tiles_per_seq):
    n_dil_out = 3 * len(DIL_PATTERNS)
    fox_in, refs = refs[:N_FOX_PROJ_IN], refs[N_FOX_PROJ_IN:]
    dil_in, refs = refs[:4], refs[4:]
    fox_out, refs = refs[:N_FOX_PROJ_OUT], refs[N_FOX_PROJ_OUT:]
    dil_out, (carry_ref, stage_ref) = refs[:n_dil_out], refs[n_dil_out:]
    _fox_proj_kernel(h_ref, *fox_in, *fox_out, carry_ref, tiles_per_seq=tiles_per_seq)
    _dil_proj_kernel(h_ref, *dil_in, *dil_out, stage_ref)


def _attn_proj(h, w_fox, b_f, w_mix, dil_block, tables, layer, B, S, tm):
    T, D = h.shape
    W = FOX_WIDTH
    n_pairs = FOX_WIDTH // LANES
    tiles_per_seq = S // tm
    row = lambda n: pl.BlockSpec((tm, n), lambda i: (i, 0))
    out_shape = [jax.ShapeDtypeStruct((T, W), BF16)] + [jax.ShapeDtypeStruct((T, n_pairs * SLAB), BF16)] * 2
    out_specs = [row(W), row(n_pairs * SLAB), row(n_pairs * SLAB)]
    n_strided = 0
    for _, d in DIL_PATTERNS:
        assert (tm // d) % BF16_ROWS == 0
        n_strided += 3 * (SLAB // LANES) * (d > 1)
        for _ in range(3):
            out_shape.append(jax.ShapeDtypeStruct((B, d, S // d, SLAB), BF16))
            out_specs.append(pl.BlockSpec((1, d, tm // d, SLAB),
                                          lambda i: (i // tiles_per_seq, 0, i % tiles_per_seq, 0)))
    res = pl.pallas_call(
        functools.partial(_attn_proj_kernel, tiles_per_seq=tiles_per_seq),
        out_shape=out_shape,
        grid=(T // tm,),
        in_specs=[row(D), _resident((None, D, 3 * W), (layer, 0, 0)),
                  _resident((None, D, LANES), (layer, 0, 3 * W // LANES)), _layer_weight(layer, 1, LANES),
                  _resident((None, D, 3 * DIL_WIDTH), (layer, 0, dil_block)), row(LANES), row(LANES), row(LANES)],
        out_specs=out_specs,
        scratch_shapes=[pltpu.VMEM((SUBLANES, LANES), F32), pltpu.VMEM((n_strided, tm, LANES), F32)],
        compiler_params=_params(1),
        name="attn_proj",
    )(h, w_fox, w_fox, b_f, w_mix, *tables)
    return res[:3], res[3:]


def _conv_sgu_kernel(h_ref, *refs, tiles_per_seq):
    conv_in, refs = refs[:N_CONV_IN], refs[N_CONV_IN:]
    sgu_in, (ob_ref, oc_ref, prev_ref) = refs[:N_SGU_IN], refs[N_SGU_IN:]
    _conv_branch_kernel(h_ref, *conv_in, ob_ref, prev_ref, tiles_per_seq=tiles_per_seq)
    _sgu_branch_kernel(h_ref, *sgu_in, oc_ref)


def _conv_sgu_branches(h, w_mix, conv_block, sgu_block, conv_w, norm_g, w_s, bias_full, layer, S, tm):
    T, D = h.shape
    row = lambda n: pl.BlockSpec((tm, n), lambda i: (i, 0))
    return pl.pallas_call(
        functools.partial(_conv_sgu_kernel, tiles_per_seq=S // tm),
        out_shape=[jax.ShapeDtypeStruct((T, CONV_WIDTH), BF16), jax.ShapeDtypeStruct((T, SGU_WIDTH), BF16)],
        grid=(T // tm,),
        in_specs=[row(D), _resident((None, D, 3 * CONV_WIDTH), (layer, 0, conv_block)),
                  _layer_weight(layer, CONV_TAPS, CONV_WIDTH),
                  _resident((None, D, 2 * SGU_WIDTH), (layer, 0, sgu_block)), _layer_weight(layer, 1, SGU_WIDTH),
                  _resident((None, SGU_GROUPS, SGU_CHUNK, SGU_CHUNK), (layer, 0, 0, 0)),
                  _layer_weight(layer, SGU_CHUNK, SGU_WIDTH)],
        out_specs=[row(CONV_WIDTH), row(SGU_WIDTH)],
        scratch_shapes=[pltpu.VMEM((SUBLANES, CONV_WIDTH), F32)],
        compiler_params=_params(1),
        name="conv_sgu_branches",
    )(h, w_mix, conv_w, w_mix, norm_g, w_s, bias_full)


def _slab_head_masks(n_rows):
    lane = lax.broadcasted_iota(jnp.int32, (n_rows, SLAB), 1)
    return [(lane >= h * HEAD_DIM) & (lane < (h + 1) * HEAD_DIM) for h in range(HEADS_PER_SLAB)]


FOX_BLOCKS_PER_ITER = 4
FOX_PAIRS_PER_STEP = 2


def _fox_attn_kernel(q_ref, k_ref, v_ref, o_ref, *, tq):
    group = pl.program_id(1)
    i = pl.program_id(2)
    q0 = pl.multiple_of(i * tq, tq)
    lane = lax.broadcasted_iota(jnp.int32, (tq, LANES), 1)
    rowi = lax.broadcasted_iota(jnp.int32, (2 * tq, tq), 0) % tq
    coli = lax.broadcasted_iota(jnp.int32, (2 * tq, tq), 1)

    def stacked_queries(c):
        q = q_ref[0, :, c * LANES:(c + 1) * LANES]
        zero = jnp.zeros_like(q)
        halves = []
        for e in range(2):
            head = 2 * (FOX_PAIRS_PER_STEP * group + c) + e
            f_pick = (lane == head) | (lane == head + FOX_HEADS) | (lane == head + 2 * FOX_HEADS)
            halves.append(jnp.concatenate(
                [jnp.where(lane // HEAD_DIM == e, q, zero), jnp.where(f_pick, -1.0, 0.0).astype(BF16)], axis=1))
        return jnp.concatenate(halves, axis=0)

    qs = [stacked_queries(c) for c in range(FOX_PAIRS_PER_STEP)]

    def block(k0, carry, diagonal):
        slabs = [slice(c * SLAB, (c + 1) * SLAB) for c in range(FOX_PAIRS_PER_STEP)]
        logits = [_dot_nt(qs[c], k_ref[0, pl.ds(k0, tq), slabs[c]]) for c in range(FOX_PAIRS_PER_STEP)]
        stats = []
        for (m, _), s in zip(carry, logits):
            if diagonal:
                s = jnp.where(coli <= rowi, s, NEG)
            m_new = jnp.maximum(m, jnp.max(s, axis=-1, keepdims=True))
            stats.append((m_new, jnp.exp(m - m_new), jnp.exp(s - m_new).astype(BF16)))
        return tuple((m_new, alpha * acc + _dot(p, v_ref[0, pl.ds(k0, tq), slabs[c]]))
                     for c, ((_, acc), (m_new, alpha, p)) in enumerate(zip(carry, stats)))

    init = tuple((jnp.full((2 * tq, 1), NEG, F32), jnp.zeros((2 * tq, SLAB), F32))
                 for _ in range(FOX_PAIRS_PER_STEP))
    def run_blocks(first_block, n, carry):
        for j in range(n):
            carry = block(pl.multiple_of((first_block + j) * tq, tq), carry, False)
        return carry

    def finish(carry):
        for c, (_, acc) in enumerate(block(q0, carry, True)):
            out = acc[:, 0:LANES] / acc[:, LANES:LANES + 1]
            o_ref[0, :, c * LANES:(c + 1) * LANES] = jnp.where(lane // HEAD_DIM == 0, out[0:tq],
                                                               out[tq:2 * tq]).astype(BF16)

    U = FOX_BLOCKS_PER_ITER
    carry = lax.fori_loop(0, i // U, lambda g, c: run_blocks(g * U, U, c), init)
    for rem in range(U):
        @pl.when(i % U == rem)
        def _(rem=rem):
            finish(run_blocks(i - rem, rem, carry))


def _fox_attention(q, k_slabs, v_slabs, B, S, tq):
    n_groups = FOX_WIDTH // LANES // FOX_PAIRS_PER_STEP
    kv_spec = pl.BlockSpec((1, S, FOX_PAIRS_PER_STEP * SLAB), lambda b, g, i: (b, 0, g))
    q_spec = pl.BlockSpec((1, tq, FOX_PAIRS_PER_STEP * LANES), lambda b, g, i: (b, i, g))
    return pl.pallas_call(
        functools.partial(_fox_attn_kernel, tq=tq),
        out_shape=jax.ShapeDtypeStruct((B, S, FOX_WIDTH), BF16),
        grid=(B, n_groups, S // tq),
        in_specs=[q_spec, kv_spec, kv_spec],
        out_specs=q_spec,
        compiler_params=_params(3),
        name="fox_attention",
    )(q, k_slabs, v_slabs)


DIL_BLOCKS_PER_STEP = 8


def _dil_attn_kernel(q_ref, k_ref, v_ref, o_ref, lse_ref, *, n_blocks):
    P = DIL_SPAN
    H = HEADS_PER_SLAB
    masks = _slab_head_masks(P)

    def attend(r, q_rows, k_rows, valid):
        q = q_ref[r, q_rows, :]
        zero = jnp.zeros_like(q)
        qs = jnp.concatenate([jnp.where(masks[h], q, zero) for h in range(H)], axis=0)
        s = jnp.where(valid, _dot_nt(qs, k_ref[r, k_rows, :]), NEG)
        m = jnp.max(s, axis=-1, keepdims=True)
        p = jnp.exp(s - m)
        l = jnp.sum(p, axis=-1, keepdims=True)
        pv = _dot(p.astype(BF16), v_ref[r, k_rows, :]) / l
        lse = jnp.broadcast_to(m + jnp.log(l), (H * P, SLAB))
        o, ls = pv[0:P], lse[0:P]
        for h in range(1, H):
            o = jnp.where(masks[h], pv[h * P:(h + 1) * P], o)
            ls = jnp.where(masks[h], lse[h * P:(h + 1) * P], ls)
        o_ref[r, q_rows, :] = o
        lse_ref[r, q_rows, :] = ls

    qi = lax.broadcasted_iota(jnp.int32, (H * P, P), 0) % P
    ki = lax.broadcasted_iota(jnp.int32, (H * P, P), 1)
    qi2 = lax.broadcasted_iota(jnp.int32, (H * P, 2 * P), 0) % P
    ki2 = lax.broadcasted_iota(jnp.int32, (H * P, 2 * P), 1)
    valid2 = (ki2 >= qi2) & (ki2 <= qi2 + P)

    for r in range(q_ref.shape[0]):
        attend(r, pl.ds(0, P), pl.ds(0, P), ki <= qi)

        def body(n, _, r=r):
            start = pl.multiple_of(n * P, P)
            attend(r, pl.ds(start, P), pl.ds(pl.multiple_of(start - P, P), 2 * P), valid2)
            return 0

        lax.fori_loop(1, n_blocks, body, 0, unroll=min(n_blocks - 1, DIL_BLOCKS_PER_STEP))


def _dil_attention(q, k, v, dilation):
    B, d, L, _ = q.shape
    assert d == dilation
    n_blocks = L // DIL_SPAN
    per_step = max(1, min(d, DIL_BLOCKS_PER_STEP // n_blocks))
    spec = pl.BlockSpec((None, per_step, L, SLAB), lambda b, r: (b, r, 0, 0))
    return pl.pallas_call(
        functools.partial(_dil_attn_kernel, n_blocks=n_blocks),
        out_shape=[jax.ShapeDtypeStruct((B, d, L, SLAB), F32)] * 2,
        grid=(B, d // per_step),
        in_specs=[spec] * 3,
        out_specs=[spec] * 2,
        compiler_params=_params(2),
        name=f"dil_attention_d{dilation}",
    )(q, k, v)


def _dil_merge_kernel(*refs):
    n_groups = len(DIL_PATTERNS)
    o_refs, l_refs = refs[:n_groups], refs[n_groups:2 * n_groups]
    out_ref, stage_ref = refs[2 * n_groups], refs[2 * n_groups + 1]
    tm = out_ref.shape[0]
    slot = 0
    outs, lses = [], []
    for src, dst in ((o_refs, outs), (l_refs, lses)):
        for (_, d), ref in zip(DIL_PATTERNS, src):
            if d == 1:
                dst.append(ref[0, 0])
                continue
            halves = []
            for j in range(SLAB // LANES):
                for r in range(d):
                    stage_ref[slot, pl.ds(r, tm // d, stride=d), :] = ref[0, r, :, j * LANES:(j + 1) * LANES]
                halves.append(stage_ref[slot])
                slot += 1
            dst.append(jnp.concatenate(halves, axis=1))
    m = functools.reduce(jnp.maximum, lses)
    es = [jnp.exp(l - m) for l in lses]
    num = es[0] * outs[0] + es[1] * outs[1] + es[2] * outs[2]
    out_ref[...] = (num / (es[0] + es[1] + es[2])).astype(BF16)


def _dil_merge(outs, lses, B, S, tm):
    T = B * S
    tiles_per_seq = S // tm
    specs = [pl.BlockSpec((1, d, tm // d, SLAB), lambda i: (i // tiles_per_seq, 0, i % tiles_per_seq, 0))
             for _, d in DIL_PATTERNS]
    n_strided = 2 * (SLAB // LANES) * sum(d > 1 for _, d in DIL_PATTERNS)
    return pl.pallas_call(
        _dil_merge_kernel,
        out_shape=jax.ShapeDtypeStruct((T, SLAB), BF16),
        grid=(T // tm,),
        in_specs=specs * 2,
        out_specs=pl.BlockSpec((tm, SLAB), lambda i: (i, 0)),
        scratch_shapes=[pltpu.VMEM((n_strided, tm, LANES), F32)],
        compiler_params=_params(1),
        name="dil_merge",
    )(*outs, *lses)


def _gated_merge_kernel(h_ref, oa_ref, ob_ref, oc_ref, od_ref, g0, g1, g2, g3, wa, wb, wc, wd, out_ref):
    h = h_ref[...]
    acc = None
    for o_ref, g_ref, w_ref in ((oa_ref, g0, wa), (ob_ref, g1, wb), (oc_ref, g2, wc), (od_ref, g3, wd)):
        term = jax.nn.sigmoid(_dot(h, g_ref[...])) * _dot(o_ref[...], w_ref[...])
        acc = term if acc is None else acc + term
    out_ref[...] = acc.astype(BF16)


def _gated_merge(h, branches, w_mix, gate_block, w_branches, layer, tm, tn):
    T, D = h.shape
    n_col = D // tn
    row = lambda n: pl.BlockSpec((tm, n), lambda j, i: (i, 0))
    gate_specs = [pl.BlockSpec((None, D, tn), functools.partial(lambda j, i, br: (layer, 0, gate_block + br * n_col + j), br=br))
                  for br in range(N_BRANCH)]
    w_specs = [pl.BlockSpec((None, w.shape[1], tn), lambda j, i: (layer, 0, j)) for w in w_branches]
    return pl.pallas_call(
        _gated_merge_kernel,
        out_shape=jax.ShapeDtypeStruct((T, D), BF16),
        grid=(n_col, T // tm),
        in_specs=[row(D)] + [row(o.shape[1]) for o in branches] + gate_specs + w_specs,
        out_specs=pl.BlockSpec((tm, tn), lambda j, i: (i, j)),
        compiler_params=_params(2),
        name="gated_merge",
    )(h, *branches, w_mix, w_mix, w_mix, w_mix, *w_branches)


def _residual_proj_kernel(a_ref, w_ref, x_ref, g_ref, xo_ref, h_ref):
    x = x_ref[...] + _dot(a_ref[...], w_ref[...])
    xo_ref[...] = x
    h_ref[...] = _rms(x, g_ref[...]).astype(h_ref.dtype)


def _residual_proj(a, w, x, g_next, layer, tm):
    T, D = x.shape
    K = a.shape[1]
    row = lambda n: pl.BlockSpec((tm, n), lambda i: (i, 0))
    return pl.pallas_call(
        _residual_proj_kernel,
        out_shape=[jax.ShapeDtypeStruct((T, D), F32), jax.ShapeDtypeStruct((T, D), BF16)],
        grid=(T // tm,),
        in_specs=[row(K), _layer_weight(layer, K, D), row(D), _resident((1, D))],
        out_specs=[row(D), row(D)],
        compiler_params=_params(1),
        name=f"residual_proj_k{K}",
    )(a, w, x, g_next)


FFN_CHUNK = SLAB
SHIFT_PAD = 16


def _shifted_rows(z, s_ref):
    tm = z.shape[0]
    half = tm // 2
    P = SHIFT_PAD
    s_ref[P - 2:P - 1, :] = s_ref[P + tm - 1:P + tm, :]
    s_ref[P - 4:P - 3, :] = s_ref[P + tm - 3:P + tm - 2, :]
    s_ref[pl.ds(P, half, stride=2), :] = z[:half]
    s_ref[pl.ds(P + 1, half, stride=2), :] = z[half:]
    s_ref[P - 1:P, :] = z[half - 1:half]
    s_ref[P - 3:P - 2, :] = z[half - 2:half - 1]
    z1 = jnp.concatenate([s_ref[pl.ds(P - 2, half, stride=2), :], s_ref[pl.ds(P - 1, half, stride=2), :]], axis=0)
    z2 = jnp.concatenate([s_ref[pl.ds(P - 4, half, stride=2), :], s_ref[pl.ds(P - 3, half, stride=2), :]], axis=0)
    return z1, z2


def _ffn_up_kernel(h_ref, wg_ref, wv_ref, cg_ref, cv_ref, act_ref, wg_bf, wv_bf, shift_ref, *, tiles_per_seq):
    i = pl.program_id(1)
    tm = h_ref.shape[0]

    @pl.when(i == 0)
    def _():
        wg_bf[...] = wg_ref[...].astype(BF16)
        wv_bf[...] = wv_ref[...].astype(BF16)

    @pl.when(i % tiles_per_seq == 0)
    def _():
        shift_ref[:, SHIFT_PAD + tm - SUBLANES:, :] = jnp.zeros((shift_ref.shape[0], SUBLANES, LANES), F32)

    h = h_ref[...]

    def conv(z, w_ref, col, buf):
        parts = []
        for j in range(FFN_CHUNK // LANES):
            zj = z[:, j * LANES:(j + 1) * LANES]
            z1, z2 = _shifted_rows(zj, shift_ref.at[buf + j])
            w = w_ref[:, col + j * LANES:col + (j + 1) * LANES]
            parts.append(w[0:1] * z2 + w[1:2] * z1 + w[2:3] * zj)
        return jnp.concatenate(parts, axis=1)

    lane_blocks = act_ref.shape[1] // LANES
    for c in range(act_ref.shape[1] // FFN_CHUNK):
        col = c * FFN_CHUNK
        cols = slice(col, col + FFN_CHUNK)
        up_gate = conv(_dot(h, wg_bf[:, cols]), cg_ref, col, col // LANES)
        up_val = conv(_dot(h, wv_bf[:, cols]), cv_ref, col, lane_blocks + col // LANES)
        act_ref[:, cols] = (jax.nn.silu(up_gate) * up_val).astype(BF16)


def _ffn_up(h, w_up, conv_w, layer, tm, tn, tiles_per_seq):
    T, D = h.shape
    d_ff = w_up.shape[2] // 2
    n_col = d_ff // tn
    return pl.pallas_call(
        functools.partial(_ffn_up_kernel, tiles_per_seq=tiles_per_seq),
        out_shape=jax.ShapeDtypeStruct((T, d_ff), BF16),
        grid=(n_col, T // tm),
        in_specs=[pl.BlockSpec((tm, D), lambda j, i: (i, 0)),
                  pl.BlockSpec((None, D, tn), lambda j, i: (layer, 0, j)),
                  pl.BlockSpec((None, D, tn), lambda j, i: (layer, 0, n_col + j)),
                  pl.BlockSpec((None, CONV_TAPS, tn), lambda j, i: (layer, 0, j)),
                  pl.BlockSpec((None, CONV_TAPS, tn), lambda j, i: (layer, 0, n_col + j))],
        out_specs=pl.BlockSpec((tm, tn), lambda j, i: (i, j)),
        scratch_shapes=[pltpu.VMEM((D, tn), BF16)] * 2 + [pltpu.VMEM((2 * tn // LANES, SHIFT_PAD + tm, LANES), F32)],
        compiler_params=_params(2),
        name="ffn_up",
    )(h, w_up, w_up, conv_w, conv_w)


def _ffn_down_ple_kernel(a_ref, wd_ref, x_ref, gp_ref, wg_ref, p_ref, wp_ref, gn_ref, *out_refs, final):
    x = x_ref[...] + _dot(a_ref[...], wd_ref[...])
    h = _rms(x, gp_ref[...]).astype(BF16)
    gate = jax.nn.sigmoid(_dot(h, wg_ref[...]))
    x = x + gate * _dot(p_ref[...].astype(BF16), wp_ref[...])
    if not final:
        out_refs[0][...] = x
    out_refs[-1][...] = _rms(x, gn_ref[...]).astype(out_refs[-1].dtype)


def _ffn_down_ple(act, w_down, x, g_ple, w_gate, p, w_proj, g_next, layer, tm, final):
    T, D = x.shape
    K = act.shape[1]
    P = p.shape[2]
    row = lambda n: pl.BlockSpec((tm, n), lambda i: (i, 0))
    if final:
        out_shape = [jax.ShapeDtypeStruct((T, D), F32)]
    else:
        out_shape = [jax.ShapeDtypeStruct((T, D), F32), jax.ShapeDtypeStruct((T, D), BF16)]
    return pl.pallas_call(
        functools.partial(_ffn_down_ple_kernel, final=final),
        out_shape=out_shape,
        grid=(T // tm,),
        in_specs=[row(K), _layer_weight(layer, K, D), row(D), _resident((1, D)), _layer_weight(layer, D, D),
                  pl.BlockSpec((None, tm, P), lambda i: (layer, i, 0)), _layer_weight(layer, P, D),
                  _resident((1, D))],
        out_specs=[row(D)] * len(out_shape),
        compiler_params=_params(1),
        name="ffn_down_ple_final" if final else "ffn_down_ple",
    )(act, w_down, x, g_ple, w_gate, p, w_proj, g_next)


def kernel(x, p, positions, norm_mix_g, w_in, fox_forget_b, shortconv_w, sgu_norm_g, sgu_w, sgu_b, w_br_fox, w_br_conv, w_br_sgu, w_br_dil, w_out, norm_ffn_g, w_up, ffn_conv_w, w_down, norm_ple_g, w_ple_gate, w_ple_proj, final_norm_g):
    B, S, D = x.shape
    depth = w_in.shape[0]
    T = B * S
    d_ff = w_down.shape[1]

    tm = _tile(S, 1024)
    tm_attn = _tile(S, 512)
    tm_big = _tile(S, 1024)
    tm_wide = _tile(S, 512)
    tm_down = _tile(S, 256)
    tq = _tile(S, 512)
    tn_merge = _tile(D, 512)
    tn_ffn = _tile(d_ff, 512)

    sizes = (3 * FOX_WIDTH, FOX_HEADS, 3 * CONV_WIDTH, 2 * SGU_WIDTH, 3 * DIL_WIDTH, N_BRANCH * D)
    offs = [0]
    for n in sizes:
        offs.append(offs[-1] + n)

    bf = lambda w: w.astype(BF16)
    b_f = jnp.pad(fox_forget_b.astype(F32), ((0, 0), (0, LANES - FOX_HEADS)))[:, None, :]
    moves, seg_block, pos = [], {}, 0
    for name, seg, block in (("gate", 5, tn_merge), ("sgu", 3, sizes[3]), ("conv", 2, sizes[2]), ("dil", 4, sizes[4])):
        pos += -pos % block
        seg_block[name] = pos // block
        moves.append((offs[seg], sizes[seg], pos))
        pos += sizes[seg]
    w_fox, w_mix = _repack_w_in(w_in, offs[1] + LANES, tuple(moves), pos)
    w_brs = [bf(w_br_fox), bf(w_br_conv), bf(w_br_sgu), bf(w_br_dil)]
    w_out_b, w_down_b = bf(w_out), bf(w_down)
    w_pg_b, w_pp_b = bf(w_ple_gate), bf(w_ple_proj)
    sgu_bias = jnp.repeat(jnp.swapaxes(sgu_b, 1, 2), SGU_GROUP_DIM, axis=2)
    sgu_norm = sgu_norm_g[:, None, :]
    p_tok = p.reshape(depth, T, p.shape[-1])
    row_vec = lambda g: g[None, :]

    tables = _rope_tables(positions, tm)
    xf = x.reshape(T, D)
    h = _rmsnorm(xf, row_vec(norm_mix_g[0]), tm)
    out = None
    for i in range(depth):
        (qa, ka, va), qkv = _attn_proj(h, w_fox, b_f, w_mix, seg_block["dil"], tables, i, B, S, tm_attn)
        o_a = _fox_attention(qa.reshape(B, S, -1), ka.reshape(B, S, -1), va.reshape(B, S, -1),
                             B, S, tq).reshape(T, FOX_WIDTH)
        o_b, o_c = _conv_sgu_branches(h, w_mix, seg_block["conv"], seg_block["sgu"], shortconv_w, sgu_norm, sgu_w,
                                      sgu_bias, i, S, tm)
        outs, lses = [], []
        for g, (window, dil) in enumerate(DIL_PATTERNS):
            assert window // dil == DIL_SPAN
            o_g, l_g = _dil_attention(*qkv[3 * g:3 * g + 3], dil)
            outs.append(o_g)
            lses.append(l_g)
        o_d = _dil_merge(outs, lses, B, S, tm)

        merged = _gated_merge(h, (o_a, o_b, o_c, o_d), w_mix, seg_block["gate"], w_brs, i, tm_big, tn_merge)
        xf, h = _residual_proj(merged, w_out_b, xf, row_vec(norm_ffn_g[i]), i, tm_wide)

        act = _ffn_up(h, w_up, ffn_conv_w, i, tm_big, tn_ffn, S // tm_big)
        final = i == depth - 1
        g_next = final_norm_g if final else norm_mix_g[i + 1]
        res = _ffn_down_ple(act, w_down_b, xf, row_vec(norm_ple_g[i]), w_pg_b, p_tok, w_pp_b, row_vec(g_next), i,
                            tm_down, final)
        if final:
            out = res[0]
        else:
            xf, h = res
    return out.reshape(B, S, D)
```

```python
import functools

import jax
import jax.numpy as jnp
from jax import lax
from jax.experimental import pallas as pl
from jax.experimental.pallas import tpu as pltpu

F32 = jnp.float32
BF16 = jnp.bfloat16

HEAD_DIM = 64
EPS = 1e-6
FOX_HEADS = 8
FOX_WIDTH = FOX_HEADS * HEAD_DIM
CONV_WIDTH = 512
CONV_TAPS = 3
SGU_GROUPS = 4
SGU_GROUP_DIM = 128
SGU_WIDTH = SGU_GROUPS * SGU_GROUP_DIM
SGU_CHUNK = 128
DIL_PATTERNS = ((128, 1), (512, 4), (2048, 16))
DIL_HEADS_PER_GROUP = 4
DIL_HEADS = DIL_HEADS_PER_GROUP * len(DIL_PATTERNS)
DIL_WIDTH = DIL_HEADS * HEAD_DIM
DIL_OUT = DIL_HEADS_PER_GROUP * HEAD_DIM
DIL_SPAN = 128
ROPE_THETA = 500000.0
ROPE_DIM = HEAD_DIM // 4
N_BRANCH = 4

LANES = 128
SUBLANES = 8
BF16_ROWS = 16
HEADS_PER_SLAB = 4
SLAB = HEADS_PER_SLAB * HEAD_DIM
VMEM_LIMIT = 56 * 1024 * 1024
NEG = -1e30
QK_SCALE = HEAD_DIM ** -0.5


def _params(n_axes):
    return pltpu.CompilerParams(dimension_semantics=("arbitrary",) * n_axes, vmem_limit_bytes=VMEM_LIMIT)


def _resident(shape, index=None):
    index = (0,) * len(shape) if index is None else index
    return pl.BlockSpec(shape, lambda *_: index, pipeline_mode=pl.Buffered(1))


def _layer_weight(layer, rows, cols):
    return _resident((None, rows, cols), (layer, 0, 0))


def _tile(n, pref):
    t = min(n, pref)
    assert n % t == 0, (n, t)
    return t


def _rms(x, g):
    var = jnp.mean(x * x, axis=-1, keepdims=True)
    return x * lax.rsqrt(var + EPS) * g


def _dot(a, b):
    return jnp.dot(a, b, preferred_element_type=F32)


def _dot_nt(a, b):
    return lax.dot_general(a, b, (((1,), (1,)), ((), ())), preferred_element_type=F32)


def _causal_conv3(z, w, prev):
    y = w[0:1] * pltpu.roll(z, 2, 0) + w[1:2] * pltpu.roll(z, 1, 0) + w[2:3] * z
    zh = z[0:BF16_ROWS]
    row = lax.broadcasted_iota(jnp.int32, zh.shape, 0)
    p1 = prev[SUBLANES - 1:SUBLANES]
    p2 = prev[SUBLANES - 2:SUBLANES - 1]
    z1 = jnp.where(row == 0, p1, pltpu.roll(zh, 1, 0))
    z2 = jnp.where(row == 0, p2, jnp.where(row == 1, p1, pltpu.roll(zh, 2, 0)))
    yh = w[0:1] * z2 + w[1:2] * z1 + w[2:3] * zh
    return jnp.concatenate([yh, y[BF16_ROWS:]], axis=0)


def _rmsnorm_kernel(x_ref, g_ref, h_ref):
    h_ref[...] = _rms(x_ref[...], g_ref[...]).astype(h_ref.dtype)


def _rmsnorm(x, g, tm):
    T, D = x.shape
    return pl.pallas_call(
        _rmsnorm_kernel,
        out_shape=jax.ShapeDtypeStruct((T, D), BF16),
        grid=(T // tm,),
        in_specs=[pl.BlockSpec((tm, D), lambda i: (i, 0)), _resident((1, D))],
        out_specs=pl.BlockSpec((tm, D), lambda i: (i, 0)),
        compiler_params=_params(1),
        name="rmsnorm",
    )(x, g)


def _rope_table_kernel(pos_ref, inv_ref, c_ref, s1_ref, s2_ref):
    ang = pos_ref[...] * inv_ref[...]
    cos = jnp.cos(ang)
    sin = jnp.sin(ang)
    lane = lax.broadcasted_iota(jnp.int32, ang.shape, 1) % HEAD_DIM
    half = ROPE_DIM // 2
    c_ref[...] = jnp.where(lane < ROPE_DIM, cos, 1.0)
    s1_ref[...] = jnp.where((lane >= half) & (lane < ROPE_DIM), sin, 0.0)
    s2_ref[...] = jnp.where(lane < half, -sin, 0.0)


def _rope_tables(positions, tm):
    T = positions.size
    half = ROPE_DIM // 2
    inv = ROPE_THETA ** (-jnp.arange(half, dtype=F32) * (2.0 / ROPE_DIM))
    lane = jnp.arange(LANES) % HEAD_DIM
    inv_lane = jnp.where(lane < ROPE_DIM, inv[lane % half], 0.0).astype(F32)[None, :]
    posf = jnp.broadcast_to(positions.reshape(T, 1).astype(F32), (T, LANES))
    spec = pl.BlockSpec((tm, LANES), lambda i: (i, 0))
    return pl.pallas_call(
        _rope_table_kernel,
        out_shape=[jax.ShapeDtypeStruct((T, LANES), F32)] * 3,
        grid=(T // tm,),
        in_specs=[spec, _resident((1, LANES))],
        out_specs=[spec] * 3,
        compiler_params=_params(1),
        name="rope_tables",
    )(posf, inv_lane)


def _repack_kernel(wt_ref, fox_ref, mix_ref, *, fox_width, moves, mix_width):
    def rows_of(src, width):
        return wt_ref[src:src + width, :].T.astype(BF16)

    fox_ref[...] = rows_of(0, fox_width)
    end = 0
    for src, width, dst in moves:
        if dst > end:
            mix_ref[:, end:dst] = jnp.zeros((mix_ref.shape[0], dst - end), BF16)
        mix_ref[:, dst:dst + width] = rows_of(src, width)
        end = dst + width
    assert end == mix_width


def _repack_w_in(w_in, fox_width, moves, mix_width):
    depth, D, d_in = w_in.shape
    assert all(src % SUBLANES == 0 for src, _, _ in moves)
    tr = _tile(D, LANES)
    return pl.pallas_call(
        functools.partial(_repack_kernel, fox_width=fox_width, moves=moves, mix_width=mix_width),
        out_shape=[jax.ShapeDtypeStruct((depth, D, fox_width), BF16),
                   jax.ShapeDtypeStruct((depth, D, mix_width), BF16)],
        grid=(depth, D // tr),
        in_specs=[pl.BlockSpec((None, d_in, tr), lambda l, r: (l, 0, r))],
        out_specs=[pl.BlockSpec((None, tr, fox_width), lambda l, r: (l, r, 0)),
                   pl.BlockSpec((None, tr, mix_width), lambda l, r: (l, r, 0))],
        compiler_params=_params(2),
        name="repack_w_in",
    )(jnp.swapaxes(w_in, 1, 2))


def _fox_proj_kernel(h_ref, w_ref, wf_ref, bf_ref, q_ref, k_ref, v_ref, carry_ref, *, tiles_per_seq):
    @pl.when(pl.program_id(0) % tiles_per_seq == 0)
    def _():
        carry_ref[...] = jnp.zeros_like(carry_ref)

    h = h_ref[...]
    W = FOX_WIDTH
    q_ref[...] = (_dot(h, w_ref[:, 0:W]) * QK_SCALE).astype(BF16)
    k = _dot(h, w_ref[:, W:2 * W]).astype(BF16)
    v = _dot(h, w_ref[:, 2 * W:3 * W]).astype(BF16)

    x = jax.nn.log_sigmoid(_dot(h, wf_ref[...]) + bf_ref[...])
    tm = x.shape[0]
    row = lax.broadcasted_iota(jnp.int32, x.shape, 0)
    lane = lax.broadcasted_iota(jnp.int32, x.shape, 1)
    s = 1
    while s < tm:
        x = x + jnp.where(row >= s, pltpu.roll(x, s, 0), 0.0)
        s *= 2
    x = x + carry_ref[0:1, :]
    carry_ref[...] = jnp.broadcast_to(x[tm - 1:tm, :], carry_ref.shape)

    x = jnp.where(lane < FOX_HEADS, x, 0.0)
    hi = x.astype(BF16).astype(F32)
    mid = (x - hi).astype(BF16).astype(F32)
    lo = (x - hi - mid).astype(BF16).astype(F32)
    f_lanes = (hi + pltpu.roll(mid, FOX_HEADS, 1) + pltpu.roll(lo, 2 * FOX_HEADS, 1)).astype(BF16)
    one_lane = jnp.where(lane == 0, 1.0, 0.0).astype(BF16)
    for j in range(FOX_WIDTH // LANES):
        k_ref[:, j * SLAB:j * SLAB + LANES] = k[:, j * LANES:(j + 1) * LANES]
        k_ref[:, j * SLAB + LANES:(j + 1) * SLAB] = f_lanes
        v_ref[:, j * SLAB:j * SLAB + LANES] = v[:, j * LANES:(j + 1) * LANES]
        v_ref[:, j * SLAB + LANES:(j + 1) * SLAB] = one_lane


def _conv_branch_kernel(h_ref, w_ref, cw_ref, o_ref, prev_ref, *, tiles_per_seq):
    @pl.when(pl.program_id(0) % tiles_per_seq == 0)
    def _():
        prev_ref[...] = jnp.zeros_like(prev_ref)

    h = h_ref[...]
    W = CONV_WIDTH
    tm = h.shape[0]
    xb = _dot(h, w_ref[:, 0:W])
    gate_b = _dot(h, w_ref[:, W:2 * W])
    gate_c = _dot(h, w_ref[:, 2 * W:3 * W])
    z = gate_c * xb
    y = _causal_conv3(z, cw_ref[...], prev_ref[...])
    prev_ref[...] = z[tm - SUBLANES:, :]
    o_ref[...] = (gate_b * y).astype(BF16)


def _sgu_branch_kernel(h_ref, w_ref, ng_ref, ws_ref, bias_ref, o_ref):
    h = h_ref[...]
    W = SGU_WIDTH
    C = SGU_CHUNK
    G = SGU_GROUP_DIM
    u = jax.nn.gelu(_dot(h, w_ref[:, 0:W]))
    v = _rms(jax.nn.gelu(_dot(h, w_ref[:, W:2 * W])), ng_ref[...]).astype(BF16)
    row = lax.broadcasted_iota(jnp.int32, (C, C), 0)
    col = lax.broadcasted_iota(jnp.int32, (C, C), 1)
    for g in range(SGU_GROUPS):
        w_tril = jnp.where(col <= row, ws_ref[g], 0.0).astype(BF16)
        cols = slice(g * G, (g + 1) * G)
        for c in range(h.shape[0] // C):
            rows = slice(c * C, (c + 1) * C)
            mixed = _dot(w_tril, v[rows, cols]) + bias_ref[:, cols]
            o_ref[rows, cols] = (u[rows, cols] * mixed).astype(BF16)


def _dil_proj_kernel(h_ref, w_ref, c_ref, s1_ref, s2_ref, *refs):
    n_groups = len(DIL_PATTERNS)
    out_refs = refs[:3 * n_groups]
    stage_ref = refs[3 * n_groups]
    h = h_ref[...]
    tm = h.shape[0]
    cos = c_ref[...]
    s1 = s1_ref[...]
    s2 = s2_ref[...]
    shift = ROPE_DIM // 2
    slot = 0
    kinds = ((0, True, QK_SCALE), (1, True, None), (2, False, None))
    wide = [_dot(h, w_ref[:, which * DIL_WIDTH:(which + 1) * DIL_WIDTH]) for which, _, _ in kinds]
    for (which, rope, scale), x_all in zip(kinds, wide):
        for g, (_, d) in enumerate(DIL_PATTERNS):
            parts = []
            for j in range(SLAB // LANES):
                xj = x_all[:, g * SLAB + j * LANES:g * SLAB + (j + 1) * LANES]
                if rope:
                    xj = xj * cos + pltpu.roll(xj, shift, 1) * s1 + pltpu.roll(xj, LANES - shift, 1) * s2
                if scale is not None:
                    xj = xj * scale
                parts.append(xj)
            out = out_refs[g * 3 + which]
            for j, xj in enumerate(parts):
                lanes = slice(j * LANES, (j + 1) * LANES)
                if d == 1:
                    out[0, 0, :, lanes] = xj.astype(BF16)
                else:
                    stage_ref[slot] = xj
                    for r in range(d):
                        out[0, r, :, lanes] = stage_ref[slot, pl.ds(r, tm // d, stride=d), :].astype(BF16)
                    slot += 1


N_FOX_PROJ_IN, N_FOX_PROJ_OUT = 3, 3
N_CONV_IN, N_SGU_IN = 2, 4


def _attn_proj_kernel(h_ref, *refs, tiles_per_seq):
    n_dil_out = 3 * len(DIL_PATTERNS)
    fox_in, refs = refs[:N_FOX_PROJ_IN], refs[N_FOX_PROJ_IN:]
    dil_in, refs = refs[:4], refs[4:]
    fox_out, refs = refs[:N_FOX_PROJ_OUT], refs[N_FOX_PROJ_OUT:]
    dil_out, (carry_ref, stage_ref) = refs[:n_dil_out], refs[n_dil_out:]
    _fox_proj_kernel(h_ref, *fox_in, *fox_out, carry_ref, tiles_per_seq=tiles_per_seq)
    _dil_proj_kernel(h_ref, *dil_in, *dil_out, stage_ref)


def _attn_proj(h, w_fox, b_f, w_mix, dil_block, tables, layer, B, S, tm):
    T, D = h.shape
    W = FOX_WIDTH
    n_pairs = FOX_WIDTH // LANES
    tiles_per_seq = S // tm
    row = lambda n: pl.BlockSpec((tm, n), lambda i: (i, 0))
    out_shape = [jax.ShapeDtypeStruct((T, W), BF16)] + [jax.ShapeDtypeStruct((T, n_pairs * SLAB), BF16)] * 2
    out_specs = [row(W), row(n_pairs * SLAB), row(n_pairs * SLAB)]
    n_strided = 0
    for _, d in DIL_PATTERNS:
        assert (tm // d) % BF16_ROWS == 0
        n_strided += 3 * (SLAB // LANES) * (d > 1)
        for _ in range(3):
            out_shape.append(jax.ShapeDtypeStruct((B, d, S // d, SLAB), BF16))
            out_specs.append(pl.BlockSpec((1, d, tm // d, SLAB),
                                          lambda i: (i // tiles_per_seq, 0, i % tiles_per_seq, 0)))
    res = pl.pallas_call(
        functools.partial(_attn_proj_kernel, tiles_per_seq=tiles_per_seq),
        out_shape=out_shape,
        grid=(T // tm,),
        in_specs=[row(D), _resident((None, D, 3 * W), (layer, 0, 0)),
                  _resident((None, D, LANES), (layer, 0, 3 * W // LANES)), _layer_weight(layer, 1, LANES),
                  _resident((None, D, 3 * DIL_WIDTH), (layer, 0, dil_block)), row(LANES), row(LANES), row(LANES)],
        out_specs=out_specs,
        scratch_shapes=[pltpu.VMEM((SUBLANES, LANES), F32), pltpu.VMEM((n_strided, tm, LANES), F32)],
        compiler_params=_params(1),
        name="attn_proj",
    )(h, w_fox, w_fox, b_f, w_mix, *tables)
    return res[:3], res[3:]


def _conv_sgu_kernel(h_ref, *refs, tiles_per_seq):
    conv_in, refs = refs[:N_CONV_IN], refs[N_CONV_IN:]
    sgu_in, (ob_ref, oc_ref, prev_ref) = refs[:N_SGU_IN], refs[N_SGU_IN:]
    _conv_branch_kernel(h_ref, *conv_in, ob_ref, prev_ref, tiles_per_seq=tiles_per_seq)
    _sgu_branch_kernel(h_ref, *sgu_in, oc_ref)


def _conv_sgu_branches(h, w_mix, conv_block, sgu_block, conv_w, norm_g, w_s, bias_full, layer, S, tm):
    T, D = h.shape
    row = lambda n: pl.BlockSpec((tm, n), lambda i: (i, 0))
    return pl.pallas_call(
        functools.partial(_conv_sgu_kernel, tiles_per_seq=S // tm),
        out_shape=[jax.ShapeDtypeStruct((T, CONV_WIDTH), BF16), jax.ShapeDtypeStruct((T, SGU_WIDTH), BF16)],
        grid=(T // tm,),
        in_specs=[row(D), _resident((None, D, 3 * CONV_WIDTH), (layer, 0, conv_block)),
                  _layer_weight(layer, CONV_TAPS, CONV_WIDTH),
                  _resident((None, D, 2 * SGU_WIDTH), (layer, 0, sgu_block)), _layer_weight(layer, 1, SGU_WIDTH),
                  _resident((None, SGU_GROUPS, SGU_CHUNK, SGU_CHUNK), (layer, 0, 0, 0)),
                  _layer_weight(layer, SGU_CHUNK, SGU_WIDTH)],
        out_specs=[row(CONV_WIDTH), row(SGU_WIDTH)],
        scratch_shapes=[pltpu.VMEM((SUBLANES, CONV_WIDTH), F32)],
        compiler_params=_params(1),
        name="conv_sgu_branches",
    )(h, w_mix, conv_w, w_mix, norm_g, w_s, bias_full)


def _slab_head_masks(n_rows):
    lane = lax.broadcasted_iota(jnp.int32, (n_rows, SLAB), 1)
    return [(lane >= h * HEAD_DIM) & (lane < (h + 1) * HEAD_DIM) for h in range(HEADS_PER_SLAB)]


FOX_BLOCKS_PER_ITER = 4
FOX_PAIRS_PER_STEP = 2


def _fox_attn_kernel(q_ref, k_ref, v_ref, o_ref, *, tq):
    group = pl.program_id(1)
    i = pl.program_id(2)
    q0 = pl.multiple_of(i * tq, tq)
    lane = lax.broadcasted_iota(jnp.int32, (tq, LANES), 1)
    rowi = lax.broadcasted_iota(jnp.int32, (2 * tq, tq), 0) % tq
    coli = lax.broadcasted_iota(jnp.int32, (2 * tq, tq), 1)

    def stacked_queries(c):
        q = q_ref[0, :, c * LANES:(c + 1) * LANES]
        zero = jnp.zeros_like(q)
        halves = []
        for e in range(2):
            head = 2 * (FOX_PAIRS_PER_STEP * group + c) + e
            f_pick = (lane == head) | (lane == head + FOX_HEADS) | (lane == head + 2 * FOX_HEADS)
            halves.append(jnp.concatenate(
                [jnp.where(lane // HEAD_DIM == e, q, zero), jnp.where(f_pick, -1.0, 0.0).astype(BF16)], axis=1))
        return jnp.concatenate(halves, axis=0)

    qs = [stacked_queries(c) for c in range(FOX_PAIRS_PER_STEP)]

    def block(k0, carry, diagonal):
        slabs = [slice(c * SLAB, (c + 1) * SLAB) for c in range(FOX_PAIRS_PER_STEP)]
        logits = [_dot_nt(qs[c], k_ref[0, pl.ds(k0, tq), slabs[c]]) for c in range(FOX_PAIRS_PER_STEP)]
        stats = []
        for (m, _), s in zip(carry, logits):
            if diagonal:
                s = jnp.where(coli <= rowi, s, NEG)
            m_new = jnp.maximum(m, jnp.max(s, axis=-1, keepdims=True))
            stats.append((m_new, jnp.exp(m - m_new), jnp.exp(s - m_new).astype(BF16)))
        return tuple((m_new, alpha * acc + _dot(p, v_ref[0, pl.ds(k0, tq), slabs[c]]))
                     for c, ((_, acc), (m_new, alpha, p)) in enumerate(zip(carry, stats)))

    init = tuple((jnp.full((2 * tq, 1), NEG, F32), jnp.zeros((2 * tq, SLAB), F32))
                 for _ in range(FOX_PAIRS_PER_STEP))
    def run_blocks(first_block, n, carry):
        for j in range(n):
            carry = block(pl.multiple_of((first_block + j) * tq, tq), carry, False)
        return carry

    def finish(carry):
        for c, (_, acc) in enumerate(block(q0, carry, True)):
            out = acc[:, 0:LANES] / acc[:, LANES:LANES + 1]
            o_ref[0, :, c * LANES:(c + 1) * LANES] = jnp.where(lane // HEAD_DIM == 0, out[0:tq],
                                                               out[tq:2 * tq]).astype(BF16)

    U = FOX_BLOCKS_PER_ITER
    carry = lax.fori_loop(0, i // U, lambda g, c: run_blocks(g * U, U, c), init)
    for rem in range(U):
        @pl.when(i % U == rem)
        def _(rem=rem):
            finish(run_blocks(i - rem, rem, carry))


def _fox_attention(q, k_slabs, v_slabs, B, S, tq):
    n_groups = FOX_WIDTH // LANES // FOX_PAIRS_PER_STEP
    kv_spec = pl.BlockSpec((1, S, FOX_PAIRS_PER_STEP * SLAB), lambda b, g, i: (b, 0, g))
    q_spec = pl.BlockSpec((1, tq, FOX_PAIRS_PER_STEP * LANES), lambda b, g, i: (b, i, g))
    return pl.pallas_call(
        functools.partial(_fox_attn_kernel, tq=tq),
        out_shape=jax.ShapeDtypeStruct((B, S, FOX_WIDTH), BF16),
        grid=(B, n_groups, S // tq),
        in_specs=[q_spec, kv_spec, kv_spec],
        out_specs=q_spec,
        compiler_params=_params(3),
        name="fox_attention",
    )(q, k_slabs, v_slabs)


DIL_BLOCKS_PER_STEP = 8


def _dil_attn_kernel(q_ref, k_ref, v_ref, o_ref, lse_ref, *, n_blocks):
    P = DIL_SPAN
    H = HEADS_PER_SLAB
    masks = _slab_head_masks(P)

    def attend(r, q_rows, k_rows, valid):
        q = q_ref[r, q_rows, :]
        zero = jnp.zeros_like(q)
        qs = jnp.concatenate([jnp.where(masks[h], q, zero) for h in range(H)], axis=0)
        s = jnp.where(valid, _dot_nt(qs, k_ref[r, k_rows, :]), NEG)
        m = jnp.max(s, axis=-1, keepdims=True)
        p = jnp.exp(s - m)
        l = jnp.sum(p, axis=-1, keepdims=True)
        pv = _dot(p.astype(BF16), v_ref[r, k_rows, :]) / l
        lse = jnp.broadcast_to(m + jnp.log(l), (H * P, SLAB))
        o, ls = pv[0:P], lse[0:P]
        for h in range(1, H):
            o = jnp.where(masks[h], pv[h * P:(h + 1) * P], o)
            ls = jnp.where(masks[h], lse[h * P:(h + 1) * P], ls)
        o_ref[r, q_rows, :] = o
        lse_ref[r, q_rows, :] = ls

    qi = lax.broadcasted_iota(jnp.int32, (H * P, P), 0) % P
    ki = lax.broadcasted_iota(jnp.int32, (H * P, P), 1)
    qi2 = lax.broadcasted_iota(jnp.int32, (H * P, 2 * P), 0) % P
    ki2 = lax.broadcasted_iota(jnp.int32, (H * P, 2 * P), 1)
    valid2 = (ki2 >= qi2) & (ki2 <= qi2 + P)

    for r in range(q_ref.shape[0]):
        attend(r, pl.ds(0, P), pl.ds(0, P), ki <= qi)

        def body(n, _, r=r):
            start = pl.multiple_of(n * P, P)
            attend(r, pl.ds(start, P), pl.ds(pl.multiple_of(start - P, P), 2 * P), valid2)
            return 0

        lax.fori_loop(1, n_blocks, body, 0, unroll=min(n_blocks - 1, DIL_BLOCKS_PER_STEP))


def _dil_attention(q, k, v, dilation):
    B, d, L, _ = q.shape
    assert d == dilation
    n_blocks = L // DIL_SPAN
    per_step = max(1, min(d, DIL_BLOCKS_PER_STEP // n_blocks))
    spec = pl.BlockSpec((None, per_step, L, SLAB), lambda b, r: (b, r, 0, 0))
    return pl.pallas_call(
        functools.partial(_dil_attn_kernel, n_blocks=n_blocks),
        out_shape=[jax.ShapeDtypeStruct((B, d, L, SLAB), F32)] * 2,
        grid=(B, d // per_step),
        in_specs=[spec] * 3,
        out_specs=[spec] * 2,
        compiler_params=_params(2),
        name=f"dil_attention_d{dilation}",
    )(q, k, v)


def _dil_merge_kernel(*refs):
    n_groups = len(DIL_PATTERNS)
    o_refs, l_refs = refs[:n_groups], refs[n_groups:2 * n_groups]
    out_ref, stage_ref = refs[2 * n_groups], refs[2 * n_groups + 1]
    tm = out_ref.shape[0]
    slot = 0
    outs, lses = [], []
    for src, dst in ((o_refs, outs), (l_refs, lses)):
        for (_, d), ref in zip(DIL_PATTERNS, src):
            if d == 1:
                dst.append(ref[0, 0])
                continue
            halves = []
            for j in range(SLAB // LANES):
                for r in range(d):
                    stage_ref[slot, pl.ds(r, tm // d, stride=d), :] = ref[0, r, :, j * LANES:(j + 1) * LANES]
                halves.append(stage_ref[slot])
                slot += 1
            dst.append(jnp.concatenate(halves, axis=1))
    m = functools.reduce(jnp.maximum, lses)
    es = [jnp.exp(l - m) for l in lses]
    num = es[0] * outs[0] + es[1] * outs[1] + es[2] * outs[2]
    out_ref[...] = (num / (es[0] + es[1] + es[2])).astype(BF16)


def _dil_merge(outs, lses, B, S, tm):
    T = B * S
    tiles_per_seq = S // tm
    specs = [pl.BlockSpec((1, d, tm // d, SLAB), lambda i: (i // tiles_per_seq, 0, i % tiles_per_seq, 0))
             for _, d in DIL_PATTERNS]
    n_strided = 2 * (SLAB // LANES) * sum(d > 1 for _, d in DIL_PATTERNS)
    return pl.pallas_call(
        _dil_merge_kernel,
        out_shape=jax.ShapeDtypeStruct((T, SLAB), BF16),
        grid=(T // tm,),
        in_specs=specs * 2,
        out_specs=pl.BlockSpec((tm, SLAB), lambda i: (i, 0)),
        scratch_shapes=[pltpu.VMEM((n_strided, tm, LANES), F32)],
        compiler_params=_params(1),
        name="dil_merge",
    )(*outs, *lses)


def _gated_merge_kernel(h_ref, oa_ref, ob_ref, oc_ref, od_ref, g0, g1, g2, g3, wa, wb, wc, wd, out_ref):
    h = h_ref[...]
    acc = None
    for o_ref, g_ref, w_ref in ((oa_ref, g0, wa), (ob_ref, g1, wb), (oc_ref, g2, wc), (od_ref, g3, wd)):
        term = jax.nn.sigmoid(_dot(h, g_ref[...])) * _dot(o_ref[...], w_ref[...])
        acc = term if acc is None else acc + term
    out_ref[...] = acc.astype(BF16)


def _gated_merge(h, branches, w_mix, gate_block, w_branches, layer, tm, tn):
    T, D = h.shape
    n_col = D // tn
    row = lambda n: pl.BlockSpec((tm, n), lambda j, i: (i, 0))
    gate_specs = [pl.BlockSpec((None, D, tn), functools.partial(lambda j, i, br: (layer, 0, gate_block + br * n_col + j), br=br))
                  for br in range(N_BRANCH)]
    w_specs = [pl.BlockSpec((None, w.shape[1], tn), lambda j, i: (layer, 0, j)) for w in w_branches]
    return pl.pallas_call(
        _gated_merge_kernel,
        out_shape=jax.ShapeDtypeStruct((T, D), BF16),
        grid=(n_col, T // tm),
        in_specs=[row(D)] + [row(o.shape[1]) for o in branches] + gate_specs + w_specs,
        out_specs=pl.BlockSpec((tm, tn), lambda j, i: (i, j)),
        compiler_params=_params(2),
        name="gated_merge",
    )(h, *branches, w_mix, w_mix, w_mix, w_mix, *w_branches)


def _residual_proj_kernel(a_ref, w_ref, x_ref, g_ref, xo_ref, h_ref):
    x = x_ref[...] + _dot(a_ref[...], w_ref[...])
    xo_ref[...] = x
    h_ref[...] = _rms(x, g_ref[...]).astype(h_ref.dtype)


def _residual_proj(a, w, x, g_next, layer, tm):
    T, D = x.shape
    K = a.shape[1]
    row = lambda n: pl.BlockSpec((tm, n), lambda i: (i, 0))
    return pl.pallas_call(
        _residual_proj_kernel,
        out_shape=[jax.ShapeDtypeStruct((T, D), F32), jax.ShapeDtypeStruct((T, D), BF16)],
        grid=(T // tm,),
        in_specs=[row(K), _layer_weight(layer, K, D), row(D), _resident((1, D))],
        out_specs=[row(D), row(D)],
        compiler_params=_params(1),
        name=f"residual_proj_k{K}",
    )(a, w, x, g_next)


FFN_CHUNK = 2 * SLAB
SHIFT_PAD = 16


def _shifted_rows(z, s_ref):
    tm = z.shape[0]
    half = tm // 2
    P = SHIFT_PAD
    s_ref[P - 2:P - 1, :] = s_ref[P + tm - 1:P + tm, :]
    s_ref[P - 4:P - 3, :] = s_ref[P + tm - 3:P + tm - 2, :]
    s_ref[pl.ds(P, half, stride=2), :] = z[:half]
    s_ref[pl.ds(P + 1, half, stride=2), :] = z[half:]
    s_ref[P - 1:P, :] = z[half - 1:half]
    s_ref[P - 3:P - 2, :] = z[half - 2:half - 1]
    z1 = jnp.concatenate([s_ref[pl.ds(P - 2, half, stride=2), :], s_ref[pl.ds(P - 1, half, stride=2), :]], axis=0)
    z2 = jnp.concatenate([s_ref[pl.ds(P - 4, half, stride=2), :], s_ref[pl.ds(P - 3, half, stride=2), :]], axis=0)
    return z1, z2


def _ffn_up_kernel(h_ref, wg_ref, wv_ref, cg_ref, cv_ref, act_ref, wg_bf, wv_bf, shift_ref, *, tiles_per_seq):
    i = pl.program_id(1)
    tm = h_ref.shape[0]

    @pl.when(i == 0)
    def _():
        wg_bf[...] = wg_ref[...].astype(BF16)
        wv_bf[...] = wv_ref[...].astype(BF16)

    @pl.when(i % tiles_per_seq == 0)
    def _():
        shift_ref[:, SHIFT_PAD + tm - SUBLANES:, :] = jnp.zeros((shift_ref.shape[0], SUBLANES, LANES), F32)

    h = h_ref[...]

    def conv(z, w_ref, col, buf):
        parts = []
        for j in range(FFN_CHUNK // LANES):
            zj = z[:, j * LANES:(j + 1) * LANES]
            z1, z2 = _shifted_rows(zj, shift_ref.at[buf + j])
            w = w_ref[:, col + j * LANES:col + (j + 1) * LANES]
            parts.append(w[0:1] * z2 + w[1:2] * z1 + w[2:3] * zj)
        return jnp.concatenate(parts, axis=1)

    lane_blocks = act_ref.shape[1] // LANES
    for c in range(act_ref.shape[1] // FFN_CHUNK):
        col = c * FFN_CHUNK
        cols = slice(col, col + FFN_CHUNK)
        up_gate = conv(_dot(h, wg_bf[:, cols]), cg_ref, col, col // LANES)
        up_val = conv(_dot(h, wv_bf[:, cols]), cv_ref, col, lane_blocks + col // LANES)
        act_ref[:, cols] = (jax.nn.silu(up_gate) * up_val).astype(BF16)


def _ffn_up(h, w_up, conv_w, layer, tm, tn, tiles_per_seq):
    T, D = h.shape
    d_ff = w_up.shape[2] // 2
    n_col = d_ff // tn
    return pl.pallas_call(
        functools.partial(_ffn_up_kernel, tiles_per_seq=tiles_per_seq),
        out_shape=jax.ShapeDtypeStruct((T, d_ff), BF16),
        grid=(n_col, T // tm),
        in_specs=[pl.BlockSpec((tm, D), lambda j, i: (i, 0)),
                  pl.BlockSpec((None, D, tn), lambda j, i: (layer, 0, j)),
                  pl.BlockSpec((None, D, tn), lambda j, i: (layer, 0, n_col + j)),
                  pl.BlockSpec((None, CONV_TAPS, tn), lambda j, i: (layer, 0, j)),
                  pl.BlockSpec((None, CONV_TAPS, tn), lambda j, i: (layer, 0, n_col + j))],
        out_specs=pl.BlockSpec((tm, tn), lambda j, i: (i, j)),
        scratch_shapes=[pltpu.VMEM((D, tn), BF16)] * 2 + [pltpu.VMEM((2 * tn // LANES, SHIFT_PAD + tm, LANES), F32)],
        compiler_params=_params(2),
        name="ffn_up",
    )(h, w_up, w_up, conv_w, conv_w)


def _ffn_down_ple_kernel(a_ref, wd_ref, x_ref, gp_ref, wg_ref, p_ref, wp_ref, gn_ref, *out_refs, final):
    x = x_ref[...] + _dot(a_ref[...], wd_ref[...])
    h = _rms(x, gp_ref[...]).astype(BF16)
    gate = jax.nn.sigmoid(_dot(h, wg_ref[...]))
    x = x + gate * _dot(p_ref[...].astype(BF16), wp_ref[...])
    if not final:
        out_refs[0][...] = x
    out_refs[-1][...] = _rms(x, gn_ref[...]).astype(out_refs[-1].dtype)


def _ffn_down_ple(act, w_down, x, g_ple, w_gate, p, w_proj, g_next, layer, tm, final):
    T, D = x.shape
    K = act.shape[1]
    P = p.shape[2]
    row = lambda n: pl.BlockSpec((tm, n), lambda i: (i, 0))
    if final:
        out_shape = [jax.ShapeDtypeStruct((T, D), F32)]
    else:
        out_shape = [jax.ShapeDtypeStruct((T, D), F32), jax.ShapeDtypeStruct((T, D), BF16)]
    return pl.pallas_call(
        functools.partial(_ffn_down_ple_kernel, final=final),
        out_shape=out_shape,
        grid=(T // tm,),
        in_specs=[row(K), _layer_weight(layer, K, D), row(D), _resident((1, D)), _layer_weight(layer, D, D),
                  pl.BlockSpec((None, tm, P), lambda i: (layer, i, 0)), _layer_weight(layer, P, D),
                  _resident((1, D))],
        out_specs=[row(D)] * len(out_shape),
        compiler_params=_params(1),
        name="ffn_down_ple_final" if final else "ffn_down_ple",
    )(act, w_down, x, g_ple, w_gate, p, w_proj, g_next)


def kernel(x, p, positions, norm_mix_g, w_in, fox_forget_b, shortconv_w, sgu_norm_g, sgu_w, sgu_b, w_br_fox, w_br_conv, w_br_sgu, w_br_dil, w_out, norm_ffn_g, w_up, ffn_conv_w, w_down, norm_ple_g, w_ple_gate, w_ple_proj, final_norm_g):
    B, S, D = x.shape
    depth = w_in.shape[0]
    T = B * S
    d_ff = w_down.shape[1]

    tm = _tile(S, 1024)
    tm_attn = _tile(S, 512)
    tm_big = _tile(S, 1024)
    tm_wide = _tile(S, 512)
    tm_down = _tile(S, 256)
    tq = _tile(S, 512)
    tn_merge = _tile(D, 512)
    tn_ffn = _tile(d_ff, 512)

    sizes = (3 * FOX_WIDTH, FOX_HEADS, 3 * CONV_WIDTH, 2 * SGU_WIDTH, 3 * DIL_WIDTH, N_BRANCH * D)
    offs = [0]
    for n in sizes:
        offs.append(offs[-1] + n)

    bf = lambda w: w.astype(BF16)
    b_f = jnp.pad(fox_forget_b.astype(F32), ((0, 0), (0, LANES - FOX_HEADS)))[:, None, :]
    moves, seg_block, pos = [], {}, 0
    for name, seg, block in (("gate", 5, tn_merge), ("sgu", 3, sizes[3]), ("conv", 2, sizes[2]), ("dil", 4, sizes[4])):
        pos += -pos % block
        seg_block[name] = pos // block
        moves.append((offs[seg], sizes[seg], pos))
        pos += sizes[seg]
    w_fox, w_mix = _repack_w_in(w_in, offs[1] + LANES, tuple(moves), pos)
    w_brs = [bf(w_br_fox), bf(w_br_conv), bf(w_br_sgu), bf(w_br_dil)]
    w_out_b, w_down_b = bf(w_out), bf(w_down)
    w_pg_b, w_pp_b = bf(w_ple_gate), bf(w_ple_proj)
    sgu_bias = jnp.repeat(jnp.swapaxes(sgu_b, 1, 2), SGU_GROUP_DIM, axis=2)
    sgu_norm = sgu_norm_g[:, None, :]
    p_tok = p.reshape(depth, T, p.shape[-1])
    row_vec = lambda g: g[None, :]

    tables = _rope_tables(positions, tm)
    xf = x.reshape(T, D)
    h = _rmsnorm(xf, row_vec(norm_mix_g[0]), tm)
    out = None
    for i in range(depth):
        (qa, ka, va), qkv = _attn_proj(h, w_fox, b_f, w_mix, seg_block["dil"], tables, i, B, S, tm_attn)
        o_a = _fox_attention(qa.reshape(B, S, -1), ka.reshape(B, S, -1), va.reshape(B, S, -1),
                             B, S, tq).reshape(T, FOX_WIDTH)
        o_b, o_c = _conv_sgu_branches(h, w_mix, seg_block["conv"], seg_block["sgu"], shortconv_w, sgu_norm, sgu_w,
                                      sgu_bias, i, S, tm)
        outs, lses = [], []
        for g, (window, dil) in enumerate(DIL_PATTERNS):
            assert window // dil == DIL_SPAN
            o_g, l_g = _dil_attention(*qkv[3 * g:3 * g + 3], dil)
            outs.append(o_g)
            lses.append(l_g)
        o_d = _dil_merge(outs, lses, B, S, tm)

        merged = _gated_merge(h, (o_a, o_b, o_c, o_d), w_mix, seg_block["gate"], w_brs, i, tm_big, tn_merge)
        xf, h = _residual_proj(merged, w_out_b, xf, row_vec(norm_ffn_g[i]), i, tm_wide)

        act = _ffn_up(h, w_up, ffn_conv_w, i, tm_big, tn_ffn, S // tm_big)
        final = i == depth - 1
        g_next = final_norm_g if final else norm_mix_g[i + 1]
        res = _ffn_down_ple(act, w_down_b, xf, row_vec(norm_ple_g[i]), w_pg_b, p_tok, w_pp_b, row_vec(g_next), i,
                            tm_down, final)
        if final:
            out = res[0]
        else:
            xf, h = res
    return out.reshape(B, S, D)
```

```python
import functools

import jax
import jax.numpy as jnp
from jax import lax
from jax.experimental import pallas as pl
from jax.experimental.pallas import tpu as pltpu

F32 = jnp.float32
BF16 = jnp.bfloat16

HEAD_DIM = 64
EPS = 1e-6
FOX_HEADS = 8
FOX_WIDTH = FOX_HEADS * HEAD_DIM
CONV_WIDTH = 512
CONV_TAPS = 3
SGU_GROUPS = 4
SGU_GROUP_DIM = 128
SGU_WIDTH = SGU_GROUPS * SGU_GROUP_DIM
SGU_CHUNK = 128
DIL_PATTERNS = ((128, 1), (512, 4), (2048, 16))
DIL_HEADS_PER_GROUP = 4
DIL_HEADS = DIL_HEADS_PER_GROUP * len(DIL_PATTERNS)
DIL_WIDTH = DIL_HEADS * HEAD_DIM
DIL_OUT = DIL_HEADS_PER_GROUP * HEAD_DIM
DIL_SPAN = 128
ROPE_THETA = 500000.0
ROPE_DIM = HEAD_DIM // 4
N_BRANCH = 4

LANES = 128
SUBLANES = 8
BF16_ROWS = 16
HEADS_PER_SLAB = 4
SLAB = HEADS_PER_SLAB * HEAD_DIM
VMEM_LIMIT = 56 * 1024 * 1024
NEG = -1e30
QK_SCALE = HEAD_DIM ** -0.5


def _params(n_axes):
    return pltpu.CompilerParams(dimension_semantics=("arbitrary",) * n_axes, vmem_limit_bytes=VMEM_LIMIT)


def _resident(shape, index=None):
    index = (0,) * len(shape) if index is None else index
    return pl.BlockSpec(shape, lambda *_: index, pipeline_mode=pl.Buffered(1))


def _layer_weight(layer, rows, cols):
    return _resident((None, rows, cols), (layer, 0, 0))


def _tile(n, pref):
    t = min(n, pref)
    assert n % t == 0, (n, t)
    return t


def _rms(x, g):
    var = jnp.mean(x * x, axis=-1, keepdims=True)
    return x * lax.rsqrt(var + EPS) * g


def _dot(a, b):
    return jnp.dot(a, b, preferred_element_type=F32)


def _dot_nt(a, b):
    return lax.dot_general(a, b, (((1,), (1,)), ((), ())), preferred_element_type=F32)


def _causal_conv3(z, w, prev):
    y = w[0:1] * pltpu.roll(z, 2, 0) + w[1:2] * pltpu.roll(z, 1, 0) + w[2:3] * z
    zh = z[0:BF16_ROWS]
    row = lax.broadcasted_iota(jnp.int32, zh.shape, 0)
    p1 = prev[SUBLANES - 1:SUBLANES]
    p2 = prev[SUBLANES - 2:SUBLANES - 1]
    z1 = jnp.where(row == 0, p1, pltpu.roll(zh, 1, 0))
    z2 = jnp.where(row == 0, p2, jnp.where(row == 1, p1, pltpu.roll(zh, 2, 0)))
    yh = w[0:1] * z2 + w[1:2] * z1 + w[2:3] * zh
    return jnp.concatenate([yh, y[BF16_ROWS:]], axis=0)


def _rmsnorm_kernel(x_ref, g_ref, h_ref):
    h_ref[...] = _rms(x_ref[...], g_ref[...]).astype(h_ref.dtype)


def _rmsnorm(x, g, tm):
    T, D = x.shape
    return pl.pallas_call(
        _rmsnorm_kernel,
        out_shape=jax.ShapeDtypeStruct((T, D), BF16),
        grid=(T // tm,),
        in_specs=[pl.BlockSpec((tm, D), lambda i: (i, 0)), _resident((1, D))],
        out_specs=pl.BlockSpec((tm, D), lambda i: (i, 0)),
        compiler_params=_params(1),
        name="rmsnorm",
    )(x, g)


def _rope_table_kernel(pos_ref, inv_ref, c_ref, s1_ref, s2_ref):
    ang = pos_ref[...] * inv_ref[...]
    cos = jnp.cos(ang)
    sin = jnp.sin(ang)
    lane = lax.broadcasted_iota(jnp.int32, ang.shape, 1) % HEAD_DIM
    half = ROPE_DIM // 2
    c_ref[...] = jnp.where(lane < ROPE_DIM, cos, 1.0)
    s1_ref[...] = jnp.where((lane >= half) & (lane < ROPE_DIM), sin, 0.0)
    s2_ref[...] = jnp.where(lane < half, -sin, 0.0)


def _rope_tables(positions, tm):
    T = positions.size
    half = ROPE_DIM // 2
    inv = ROPE_THETA ** (-jnp.arange(half, dtype=F32) * (2.0 / ROPE_DIM))
    lane = jnp.arange(LANES) % HEAD_DIM
    inv_lane = jnp.where(lane < ROPE_DIM, inv[lane % half], 0.0).astype(F32)[None, :]
    posf = jnp.broadcast_to(positions.reshape(T, 1).astype(F32), (T, LANES))
    spec = pl.BlockSpec((tm, LANES), lambda i: (i, 0))
    return pl.pallas_call(
        _rope_table_kernel,
        out_shape=[jax.ShapeDtypeStruct((T, LANES), F32)] * 3,
        grid=(T // tm,),
        in_specs=[spec, _resident((1, LANES))],
        out_specs=[spec] * 3,
        compiler_params=_params(1),
        name="rope_tables",
    )(posf, inv_lane)


def _repack_kernel(wt_ref, fox_ref, mix_ref, *, fox_width, moves, mix_width):
    def rows_of(src, width):
        return wt_ref[src:src + width, :].T.astype(BF16)

    fox_ref[...] = rows_of(0, fox_width)
    end = 0
    for src, width, dst in moves:
        if dst > end:
            mix_ref[:, end:dst] = jnp.zeros((mix_ref.shape[0], dst - end), BF16)
        mix_ref[:, dst:dst + width] = rows_of(src, width)
        end = dst + width
    assert end == mix_width


def _repack_w_in(w_in, fox_width, moves, mix_width):
    depth, D, d_in = w_in.shape
    assert all(src % SUBLANES == 0 for src, _, _ in moves)
    tr = _tile(D, LANES)
    return pl.pallas_call(
        functools.partial(_repack_kernel, fox_width=fox_width, moves=moves, mix_width=mix_width),
        out_shape=[jax.ShapeDtypeStruct((depth, D, fox_width), BF16),
                   jax.ShapeDtypeStruct((depth, D, mix_width), BF16)],
        grid=(depth, D // tr),
        in_specs=[pl.BlockSpec((None, d_in, tr), lambda l, r: (l, 0, r))],
        out_specs=[pl.BlockSpec((None, tr, fox_width), lambda l, r: (l, r, 0)),
                   pl.BlockSpec((None, tr, mix_width), lambda l, r: (l, r, 0))],
        compiler_params=_params(2),
        name="repack_w_in",
    )(jnp.swapaxes(w_in, 1, 2))


def _fox_proj_kernel(h_ref, w_ref, wf_ref, bf_ref, q_ref, k_ref, v_ref, carry_ref, *, tiles_per_seq):
    @pl.when(pl.program_id(0) % tiles_per_seq == 0)
    def _():
        carry_ref[...] = jnp.zeros_like(carry_ref)

    h = h_ref[...]
    W = FOX_WIDTH
    q_ref[...] = (_dot(h, w_ref[:, 0:W]) * QK_SCALE).astype(BF16)
    k = _dot(h, w_ref[:, W:2 * W]).astype(BF16)
    v = _dot(h, w_ref[:, 2 * W:3 * W]).astype(BF16)

    x = jax.nn.log_sigmoid(_dot(h, wf_ref[...]) + bf_ref[...])
    tm = x.shape[0]
    row = lax.broadcasted_iota(jnp.int32, x.shape, 0)
    lane = lax.broadcasted_iota(jnp.int32, x.shape, 1)
    s = 1
    while s < tm:
        x = x + jnp.where(row >= s, pltpu.roll(x, s, 0), 0.0)
        s *= 2
    x = x + carry_ref[0:1, :]
    carry_ref[...] = jnp.broadcast_to(x[tm - 1:tm, :], carry_ref.shape)

    x = jnp.where(lane < FOX_HEADS, x, 0.0)
    hi = x.astype(BF16).astype(F32)
    mid = (x - hi).astype(BF16).astype(F32)
    lo = (x - hi - mid).astype(BF16).astype(F32)
    f_lanes = (hi + pltpu.roll(mid, FOX_HEADS, 1) + pltpu.roll(lo, 2 * FOX_HEADS, 1)).astype(BF16)
    one_lane = jnp.where(lane == 0, 1.0, 0.0).astype(BF16)
    for j in range(FOX_WIDTH // LANES):
        k_ref[:, j * SLAB:j * SLAB + LANES] = k[:, j * LANES:(j + 1) * LANES]
        k_ref[:, j * SLAB + LANES:(j + 1) * SLAB] = f_lanes
        v_ref[:, j * SLAB:j * SLAB + LANES] = v[:, j * LANES:(j + 1) * LANES]
        v_ref[:, j * SLAB + LANES:(j + 1) * SLAB] = one_lane


def _conv_branch_finish(dots, cw_ref, o_ref, prev_ref):
    xb, gate_b, gate_c = dots
    tm = xb.shape[0]
    z = gate_c * xb
    y = _causal_conv3(z, cw_ref[...], prev_ref[...])
    prev_ref[...] = z[tm - SUBLANES:, :]
    o_ref[...] = (gate_b * y).astype(BF16)


def _sgu_branch_finish(dots, ng_ref, ws_ref, bias_ref, o_ref):
    C = SGU_CHUNK
    G = SGU_GROUP_DIM
    u = jax.nn.gelu(dots[0])
    v = _rms(jax.nn.gelu(dots[1]), ng_ref[...]).astype(BF16)
    row = lax.broadcasted_iota(jnp.int32, (C, C), 0)
    col = lax.broadcasted_iota(jnp.int32, (C, C), 1)
    for g in range(SGU_GROUPS):
        w_tril = jnp.where(col <= row, ws_ref[g], 0.0).astype(BF16)
        cols = slice(g * G, (g + 1) * G)
        for c in range(u.shape[0] // C):
            rows = slice(c * C, (c + 1) * C)
            mixed = _dot(w_tril, v[rows, cols]) + bias_ref[:, cols]
            o_ref[rows, cols] = (u[rows, cols] * mixed).astype(BF16)


def _column_dots(h, w_ref, width, n):
    return [_dot(h, w_ref[:, j * width:(j + 1) * width]) for j in range(n)]


def _dil_proj_kernel(h_ref, w_ref, c_ref, s1_ref, s2_ref, *refs):
    n_groups = len(DIL_PATTERNS)
    out_refs = refs[:3 * n_groups]
    stage_ref = refs[3 * n_groups]
    h = h_ref[...]
    tm = h.shape[0]
    cos = c_ref[...]
    s1 = s1_ref[...]
    s2 = s2_ref[...]
    shift = ROPE_DIM // 2
    slot = 0
    kinds = ((0, True, QK_SCALE), (1, True, None), (2, False, None))
    wide = [_dot(h, w_ref[:, which * DIL_WIDTH:(which + 1) * DIL_WIDTH]) for which, _, _ in kinds]
    for (which, rope, scale), x_all in zip(kinds, wide):
        for g, (_, d) in enumerate(DIL_PATTERNS):
            parts = []
            for j in range(SLAB // LANES):
                xj = x_all[:, g * SLAB + j * LANES:g * SLAB + (j + 1) * LANES]
                if rope:
                    xj = xj * cos + pltpu.roll(xj, shift, 1) * s1 + pltpu.roll(xj, LANES - shift, 1) * s2
                if scale is not None:
                    xj = xj * scale
                parts.append(xj)
            out = out_refs[g * 3 + which]
            for j, xj in enumerate(parts):
                lanes = slice(j * LANES, (j + 1) * LANES)
                if d == 1:
                    out[0, 0, :, lanes] = xj.astype(BF16)
                else:
                    stage_ref[slot] = xj
                    for r in range(d):
                        out[0, r, :, lanes] = stage_ref[slot, pl.ds(r, tm // d, stride=d), :].astype(BF16)
                    slot += 1


N_FOX_PROJ_IN, N_FOX_PROJ_OUT = 3, 3


def _attn_proj_kernel(h_ref, *refs, tiles_per_seq):
    n_dil_out = 3 * len(DIL_PATTERNS)
    fox_in, refs = refs[:N_FOX_PROJ_IN], refs[N_FOX_PROJ_IN:]
    dil_in, refs = refs[:4], refs[4:]
    fox_out, refs = refs[:N_FOX_PROJ_OUT], refs[N_FOX_PROJ_OUT:]
    dil_out, (carry_ref, stage_ref) = refs[:n_dil_out], refs[n_dil_out:]
    _fox_proj_kernel(h_ref, *fox_in, *fox_out, carry_ref, tiles_per_seq=tiles_per_seq)
    _dil_proj_kernel(h_ref, *dil_in, *dil_out, stage_ref)


def _attn_proj(h, w_fox, b_f, w_mix, dil_block, tables, layer, B, S, tm):
    T, D = h.shape
    W = FOX_WIDTH
    n_pairs = FOX_WIDTH // LANES
    tiles_per_seq = S // tm
    row = lambda n: pl.BlockSpec((tm, n), lambda i: (i, 0))
    out_shape = [jax.ShapeDtypeStruct((T, W), BF16)] + [jax.ShapeDtypeStruct((T, n_pairs * SLAB), BF16)] * 2
    out_specs = [row(W), row(n_pairs * SLAB), row(n_pairs * SLAB)]
    n_strided = 0
    for _, d in DIL_PATTERNS:
        assert (tm // d) % BF16_ROWS == 0
        n_strided += 3 * (SLAB // LANES) * (d > 1)
        for _ in range(3):
            out_shape.append(jax.ShapeDtypeStruct((B, d, S // d, SLAB), BF16))
            out_specs.append(pl.BlockSpec((1, d, tm // d, SLAB),
                                          lambda i: (i // tiles_per_seq, 0, i % tiles_per_seq, 0)))
    res = pl.pallas_call(
        functools.partial(_attn_proj_kernel, tiles_per_seq=tiles_per_seq),
        out_shape=out_shape,
        grid=(T // tm,),
        in_specs=[row(D), _resident((None, D, 3 * W), (layer, 0, 0)),
                  _resident((None, D, LANES), (layer, 0, 3 * W // LANES)), _layer_weight(layer, 1, LANES),
                  _resident((None, D, 3 * DIL_WIDTH), (layer, 0, dil_block)), row(LANES), row(LANES), row(LANES)],
        out_specs=out_specs,
        scratch_shapes=[pltpu.VMEM((SUBLANES, LANES), F32), pltpu.VMEM((n_strided, tm, LANES), F32)],
        compiler_params=_params(1),
        name="attn_proj",
    )(h, w_fox, w_fox, b_f, w_mix, *tables)
    return res[:3], res[3:]


def _conv_sgu_kernel(h_ref, wc_ref, cw_ref, ws_ref, ng_ref, wsp_ref, bias_ref, ob_ref, oc_ref, prev_ref, *,
                     tiles_per_seq):
    @pl.when(pl.program_id(0) % tiles_per_seq == 0)
    def _():
        prev_ref[...] = jnp.zeros_like(prev_ref)

    h = h_ref[...]
    conv_dots = _column_dots(h, wc_ref, CONV_WIDTH, 3)
    sgu_dots = _column_dots(h, ws_ref, SGU_WIDTH, 2)
    _conv_branch_finish(conv_dots, cw_ref, ob_ref, prev_ref)
    _sgu_branch_finish(sgu_dots, ng_ref, wsp_ref, bias_ref, oc_ref)


def _conv_sgu_branches(h, w_mix, conv_block, sgu_block, conv_w, norm_g, w_s, bias_full, layer, S, tm):
    T, D = h.shape
    row = lambda n: pl.BlockSpec((tm, n), lambda i: (i, 0))
    return pl.pallas_call(
        functools.partial(_conv_sgu_kernel, tiles_per_seq=S // tm),
        out_shape=[jax.ShapeDtypeStruct((T, CONV_WIDTH), BF16), jax.ShapeDtypeStruct((T, SGU_WIDTH), BF16)],
        grid=(T // tm,),
        in_specs=[row(D), _resident((None, D, 3 * CONV_WIDTH), (layer, 0, conv_block)),
                  _layer_weight(layer, CONV_TAPS, CONV_WIDTH),
                  _resident((None, D, 2 * SGU_WIDTH), (layer, 0, sgu_block)), _layer_weight(layer, 1, SGU_WIDTH),
                  _resident((None, SGU_GROUPS, SGU_CHUNK, SGU_CHUNK), (layer, 0, 0, 0)),
                  _layer_weight(layer, SGU_CHUNK, SGU_WIDTH)],
        out_specs=[row(CONV_WIDTH), row(SGU_WIDTH)],
        scratch_shapes=[pltpu.VMEM((SUBLANES, CONV_WIDTH), F32)],
        compiler_params=_params(1),
        name="conv_sgu_branches",
    )(h, w_mix, conv_w, w_mix, norm_g, w_s, bias_full)


def _slab_head_masks(n_rows):
    lane = lax.broadcasted_iota(jnp.int32, (n_rows, SLAB), 1)
    return [(lane >= h * HEAD_DIM) & (lane < (h + 1) * HEAD_DIM) for h in range(HEADS_PER_SLAB)]


FOX_BLOCKS_PER_ITER = 4
FOX_PAIRS_PER_STEP = 2


def _fox_attn_kernel(q_ref, k_ref, v_ref, o_ref, *, tq):
    group = pl.program_id(1)
    i = pl.program_id(2)
    q0 = pl.multiple_of(i * tq, tq)
    lane = lax.broadcasted_iota(jnp.int32, (tq, LANES), 1)
    rowi = lax.broadcasted_iota(jnp.int32, (2 * tq, tq), 0) % tq
    coli = lax.broadcasted_iota(jnp.int32, (2 * tq, tq), 1)

    def stacked_queries(c):
        q = q_ref[0, :, c * LANES:(c + 1) * LANES]
        zero = jnp.zeros_like(q)
        halves = []
        for e in range(2):
            head = 2 * (FOX_PAIRS_PER_STEP * group + c) + e
            f_pick = (lane == head) | (lane == head + FOX_HEADS) | (lane == head + 2 * FOX_HEADS)
            halves.append(jnp.concatenate(
                [jnp.where(lane // HEAD_DIM == e, q, zero), jnp.where(f_pick, -1.0, 0.0).astype(BF16)], axis=1))
        return jnp.concatenate(halves, axis=0)

    qs = [stacked_queries(c) for c in range(FOX_PAIRS_PER_STEP)]

    def block(k0, carry, diagonal):
        slabs = [slice(c * SLAB, (c + 1) * SLAB) for c in range(FOX_PAIRS_PER_STEP)]
        logits = [_dot_nt(qs[c], k_ref[0, pl.ds(k0, tq), slabs[c]]) for c in range(FOX_PAIRS_PER_STEP)]
        stats = []
        for (m, _), s in zip(carry, logits):
            if diagonal:
                s = jnp.where(coli <= rowi, s, NEG)
            m_new = jnp.maximum(m, jnp.max(s, axis=-1, keepdims=True))
            stats.append((m_new, jnp.exp(m - m_new), jnp.exp(s - m_new).astype(BF16)))
        return tuple((m_new, alpha * acc + _dot(p, v_ref[0, pl.ds(k0, tq), slabs[c]]))
                     for c, ((_, acc), (m_new, alpha, p)) in enumerate(zip(carry, stats)))

    init = tuple((jnp.full((2 * tq, 1), NEG, F32), jnp.zeros((2 * tq, SLAB), F32))
                 for _ in range(FOX_PAIRS_PER_STEP))
    def run_blocks(first_block, n, carry):
        for j in range(n):
            carry = block(pl.multiple_of((first_block + j) * tq, tq), carry, False)
        return carry

    def finish(carry):
        for c, (_, acc) in enumerate(block(q0, carry, True)):
            out = acc[:, 0:LANES] / acc[:, LANES:LANES + 1]
            o_ref[0, :, c * LANES:(c + 1) * LANES] = jnp.where(lane // HEAD_DIM == 0, out[0:tq],
                                                               out[tq:2 * tq]).astype(BF16)

    U = FOX_BLOCKS_PER_ITER
    carry = lax.fori_loop(0, i // U, lambda g, c: run_blocks(g * U, U, c), init)
    for rem in range(U):
        @pl.when(i % U == rem)
        def _(rem=rem):
            finish(run_blocks(i - rem, rem, carry))


def _fox_attention(q, k_slabs, v_slabs, B, S, tq):
    n_groups = FOX_WIDTH // LANES // FOX_PAIRS_PER_STEP
    kv_spec = pl.BlockSpec((1, S, FOX_PAIRS_PER_STEP * SLAB), lambda b, g, i: (b, 0, g))
    q_spec = pl.BlockSpec((1, tq, FOX_PAIRS_PER_STEP * LANES), lambda b, g, i: (b, i, g))
    return pl.pallas_call(
        functools.partial(_fox_attn_kernel, tq=tq),
        out_shape=jax.ShapeDtypeStruct((B, S, FOX_WIDTH), BF16),
        grid=(B, n_groups, S // tq),
        in_specs=[q_spec, kv_spec, kv_spec],
        out_specs=q_spec,
        compiler_params=_params(3),
        name="fox_attention",
    )(q, k_slabs, v_slabs)


DIL_BLOCKS_PER_STEP = 8


def _dil_attn_kernel(q_ref, k_ref, v_ref, o_ref, lse_ref, *, n_blocks):
    P = DIL_SPAN
    H = HEADS_PER_SLAB
    masks = _slab_head_masks(P)

    def attend(r, q_rows, k_rows, valid):
        q = q_ref[r, q_rows, :]
        zero = jnp.zeros_like(q)
        qs = jnp.concatenate([jnp.where(masks[h], q, zero) for h in range(H)], axis=0)
        s = jnp.where(valid, _dot_nt(qs, k_ref[r, k_rows, :]), NEG)
        m = jnp.max(s, axis=-1, keepdims=True)
        p = jnp.exp(s - m)
        l = jnp.sum(p, axis=-1, keepdims=True)
        pv = _dot(p.astype(BF16), v_ref[r, k_rows, :]) / l
        lse = jnp.broadcast_to(m + jnp.log(l), (H * P, SLAB))
        o, ls = pv[0:P], lse[0:P]
        for h in range(1, H):
            o = jnp.where(masks[h], pv[h * P:(h + 1) * P], o)
            ls = jnp.where(masks[h], lse[h * P:(h + 1) * P], ls)
        o_ref[r, q_rows, :] = o
        lse_ref[r, q_rows, :] = ls

    qi = lax.broadcasted_iota(jnp.int32, (H * P, P), 0) % P
    ki = lax.broadcasted_iota(jnp.int32, (H * P, P), 1)
    qi2 = lax.broadcasted_iota(jnp.int32, (H * P, 2 * P), 0) % P
    ki2 = lax.broadcasted_iota(jnp.int32, (H * P, 2 * P), 1)
    valid2 = (ki2 >= qi2) & (ki2 <= qi2 + P)

    for r in range(q_ref.shape[0]):
        attend(r, pl.ds(0, P), pl.ds(0, P), ki <= qi)

        def body(n, _, r=r):
            start = pl.multiple_of(n * P, P)
            attend(r, pl.ds(start, P), pl.ds(pl.multiple_of(start - P, P), 2 * P), valid2)
            return 0

        lax.fori_loop(1, n_blocks, body, 0, unroll=min(n_blocks - 1, DIL_BLOCKS_PER_STEP))


def _dil_attention(q, k, v, dilation):
    B, d, L, _ = q.shape
    assert d == dilation
    n_blocks = L // DIL_SPAN
    per_step = max(1, min(d, DIL_BLOCKS_PER_STEP // n_blocks))
    spec = pl.BlockSpec((None, per_step, L, SLAB), lambda b, r: (b, r, 0, 0))
    return pl.pallas_call(
        functools.partial(_dil_attn_kernel, n_blocks=n_blocks),
        out_shape=[jax.ShapeDtypeStruct((B, d, L, SLAB), F32)] * 2,
        grid=(B, d // per_step),
        in_specs=[spec] * 3,
        out_specs=[spec] * 2,
        compiler_params=_params(2),
        name=f"dil_attention_d{dilation}",
    )(q, k, v)


def _dil_merge_kernel(*refs):
    n_groups = len(DIL_PATTERNS)
    o_refs, l_refs = refs[:n_groups], refs[n_groups:2 * n_groups]
    out_ref, stage_ref = refs[2 * n_groups], refs[2 * n_groups + 1]
    tm = out_ref.shape[0]
    slot = 0
    outs, lses = [], []
    for src, dst in ((o_refs, outs), (l_refs, lses)):
        for (_, d), ref in zip(DIL_PATTERNS, src):
            if d == 1:
                dst.append(ref[0, 0])
                continue
            halves = []
            for j in range(SLAB // LANES):
                for r in range(d):
                    stage_ref[slot, pl.ds(r, tm // d, stride=d), :] = ref[0, r, :, j * LANES:(j + 1) * LANES]
                halves.append(stage_ref[slot])
                slot += 1
            dst.append(jnp.concatenate(halves, axis=1))
    m = functools.reduce(jnp.maximum, lses)
    es = [jnp.exp(l - m) for l in lses]
    num = es[0] * outs[0] + es[1] * outs[1] + es[2] * outs[2]
    out_ref[...] = (num / (es[0] + es[1] + es[2])).astype(BF16)


def _dil_merge(outs, lses, B, S, tm):
    T = B * S
    tiles_per_seq = S // tm
    specs = [pl.BlockSpec((1, d, tm // d, SLAB), lambda i: (i // tiles_per_seq, 0, i % tiles_per_seq, 0))
             for _, d in DIL_PATTERNS]
    n_strided = 2 * (SLAB // LANES) * sum(d > 1 for _, d in DIL_PATTERNS)
    return pl.pallas_call(
        _dil_merge_kernel,
        out_shape=jax.ShapeDtypeStruct((T, SLAB), BF16),
        grid=(T // tm,),
        in_specs=specs * 2,
        out_specs=pl.BlockSpec((tm, SLAB), lambda i: (i, 0)),
        scratch_shapes=[pltpu.VMEM((n_strided, tm, LANES), F32)],
        compiler_params=_params(1),
        name="dil_merge",
    )(*outs, *lses)


def _gated_merge_kernel(h_ref, oa_ref, ob_ref, oc_ref, od_ref, g0, g1, g2, g3, wa, wb, wc, wd, out_ref):
    h = h_ref[...]
    acc = None
    for o_ref, g_ref, w_ref in ((oa_ref, g0, wa), (ob_ref, g1, wb), (oc_ref, g2, wc), (od_ref, g3, wd)):
        term = jax.nn.sigmoid(_dot(h, g_ref[...])) * _dot(o_ref[...], w_ref[...])
        acc = term if acc is None else acc + term
    out_ref[...] = acc.astype(BF16)


def _gated_merge(h, branches, w_mix, gate_block, w_branches, layer, tm, tn):
    T, D = h.shape
    n_col = D // tn
    row = lambda n: pl.BlockSpec((tm, n), lambda j, i: (i, 0))
    gate_specs = [pl.BlockSpec((None, D, tn), functools.partial(lambda j, i, br: (layer, 0, gate_block + br * n_col + j), br=br))
                  for br in range(N_BRANCH)]
    w_specs = [pl.BlockSpec((None, w.shape[1], tn), lambda j, i: (layer, 0, j)) for w in w_branches]
    return pl.pallas_call(
        _gated_merge_kernel,
        out_shape=jax.ShapeDtypeStruct((T, D), BF16),
        grid=(n_col, T // tm),
        in_specs=[row(D)] + [row(o.shape[1]) for o in branches] + gate_specs + w_specs,
        out_specs=pl.BlockSpec((tm, tn), lambda j, i: (i, j)),
        compiler_params=_params(2),
        name="gated_merge",
    )(h, *branches, w_mix, w_mix, w_mix, w_mix, *w_branches)


RESIDUAL_ROW_UNITS = 2


def _residual_proj_kernel(a_ref, w_ref, x_ref, g_ref, xo_ref, h_ref):
    rows_per_unit = a_ref.shape[0] // RESIDUAL_ROW_UNITS
    units = [slice(u * rows_per_unit, (u + 1) * rows_per_unit) for u in range(RESIDUAL_ROW_UNITS)]
    prods = [_dot(a_ref[rows, :], w_ref[...]) for rows in units]
    for rows, prod in zip(units, prods):
        x = x_ref[rows, :] + prod
        xo_ref[rows, :] = x
        h_ref[rows, :] = _rms(x, g_ref[...]).astype(h_ref.dtype)


def _residual_proj(a, w, x, g_next, layer, tm):
    T, D = x.shape
    K = a.shape[1]
    row = lambda n: pl.BlockSpec((tm, n), lambda i: (i, 0))
    return pl.pallas_call(
        _residual_proj_kernel,
        out_shape=[jax.ShapeDtypeStruct((T, D), F32), jax.ShapeDtypeStruct((T, D), BF16)],
        grid=(T // tm,),
        in_specs=[row(K), _layer_weight(layer, K, D), row(D), _resident((1, D))],
        out_specs=[row(D), row(D)],
        compiler_params=_params(1),
        name=f"residual_proj_k{K}",
    )(a, w, x, g_next)


FFN_CHUNK = 2 * SLAB
SHIFT_PAD = 16


def _shifted_rows(z, s_ref):
    tm = z.shape[0]
    half = tm // 2
    P = SHIFT_PAD
    s_ref[P - 2:P - 1, :] = s_ref[P + tm - 1:P + tm, :]
    s_ref[P - 4:P - 3, :] = s_ref[P + tm - 3:P + tm - 2, :]
    s_ref[pl.ds(P, half, stride=2), :] = z[:half]
    s_ref[pl.ds(P + 1, half, stride=2), :] = z[half:]
    s_ref[P - 1:P, :] = z[half - 1:half]
    s_ref[P - 3:P - 2, :] = z[half - 2:half - 1]
    z1 = jnp.concatenate([s_ref[pl.ds(P - 2, half, stride=2), :], s_ref[pl.ds(P - 1, half, stride=2), :]], axis=0)
    z2 = jnp.concatenate([s_ref[pl.ds(P - 4, half, stride=2), :], s_ref[pl.ds(P - 3, half, stride=2), :]], axis=0)
    return z1, z2


def _ffn_up_kernel(h_ref, wg_ref, wv_ref, cg_ref, cv_ref, act_ref, wg_bf, wv_bf, shift_ref, *, tiles_per_seq):
    i = pl.program_id(1)
    tm = h_ref.shape[0]

    @pl.when(i == 0)
    def _():
        wg_bf[...] = wg_ref[...].astype(BF16)
        wv_bf[...] = wv_ref[...].astype(BF16)

    @pl.when(i % tiles_per_seq == 0)
    def _():
        shift_ref[:, SHIFT_PAD + tm - SUBLANES:, :] = jnp.zeros((shift_ref.shape[0], SUBLANES, LANES), F32)

    h = h_ref[...]

    def conv(z, w_ref, col, buf):
        parts = []
        for j in range(FFN_CHUNK // LANES):
            zj = z[:, j * LANES:(j + 1) * LANES]
            z1, z2 = _shifted_rows(zj, shift_ref.at[buf + j])
            w = w_ref[:, col + j * LANES:col + (j + 1) * LANES]
            parts.append(w[0:1] * z2 + w[1:2] * z1 + w[2:3] * zj)
        return jnp.concatenate(parts, axis=1)

    lane_blocks = act_ref.shape[1] // LANES
    for c in range(act_ref.shape[1] // FFN_CHUNK):
        col = c * FFN_CHUNK
        cols = slice(col, col + FFN_CHUNK)
        up_gate = conv(_dot(h, wg_bf[:, cols]), cg_ref, col, col // LANES)
        up_val = conv(_dot(h, wv_bf[:, cols]), cv_ref, col, lane_blocks + col // LANES)
        act_ref[:, cols] = (jax.nn.silu(up_gate) * up_val).astype(BF16)


def _ffn_up(h, w_up, conv_w, layer, tm, tn, tiles_per_seq):
    T, D = h.shape
    d_ff = w_up.shape[2] // 2
    n_col = d_ff // tn
    return pl.pallas_call(
        functools.partial(_ffn_up_kernel, tiles_per_seq=tiles_per_seq),
        out_shape=jax.ShapeDtypeStruct((T, d_ff), BF16),
        grid=(n_col, T // tm),
        in_specs=[pl.BlockSpec((tm, D), lambda j, i: (i, 0)),
                  pl.BlockSpec((None, D, tn), lambda j, i: (layer, 0, j)),
                  pl.BlockSpec((None, D, tn), lambda j, i: (layer, 0, n_col + j)),
                  pl.BlockSpec((None, CONV_TAPS, tn), lambda j, i: (layer, 0, j)),
                  pl.BlockSpec((None, CONV_TAPS, tn), lambda j, i: (layer, 0, n_col + j))],
        out_specs=pl.BlockSpec((tm, tn), lambda j, i: (i, j)),
        scratch_shapes=[pltpu.VMEM((D, tn), BF16)] * 2 + [pltpu.VMEM((2 * tn // LANES, SHIFT_PAD + tm, LANES), F32)],
        compiler_params=_params(2),
        name="ffn_up",
    )(h, w_up, w_up, conv_w, conv_w)


def _ffn_down_ple_kernel(a_ref, wd_ref, x_ref, gp_ref, wg_ref, p_ref, wp_ref, gn_ref, *out_refs, final):
    x = x_ref[...] + _dot(a_ref[...], wd_ref[...])
    h = _rms(x, gp_ref[...]).astype(BF16)
    gate = jax.nn.sigmoid(_dot(h, wg_ref[...]))
    x = x + gate * _dot(p_ref[...].astype(BF16), wp_ref[...])
    if not final:
        out_refs[0][...] = x
    out_refs[-1][...] = _rms(x, gn_ref[...]).astype(out_refs[-1].dtype)


def _ffn_down_ple(act, w_down, x, g_ple, w_gate, p, w_proj, g_next, layer, tm, final):
    T, D = x.shape
    K = act.shape[1]
    P = p.shape[2]
    row = lambda n: pl.BlockSpec((tm, n), lambda i: (i, 0))
    if final:
        out_shape = [jax.ShapeDtypeStruct((T, D), F32)]
    else:
        out_shape = [jax.ShapeDtypeStruct((T, D), F32), jax.ShapeDtypeStruct((T, D), BF16)]
    return pl.pallas_call(
        functools.partial(_ffn_down_ple_kernel, final=final),
        out_shape=out_shape,
        grid=(T // tm,),
        in_specs=[row(K), _layer_weight(layer, K, D), row(D), _resident((1, D)), _layer_weight(layer, D, D),
                  pl.BlockSpec((None, tm, P), lambda i: (layer, i, 0)), _layer_weight(layer, P, D),
                  _resident((1, D))],
        out_specs=[row(D)] * len(out_shape),
        compiler_params=_params(1),
        name="ffn_down_ple_final" if final else "ffn_down_ple",
    )(act, w_down, x, g_ple, w_gate, p, w_proj, g_next)


def kernel(x, p, positions, norm_mix_g, w_in, fox_forget_b, shortconv_w, sgu_norm_g, sgu_w, sgu_b, w_br_fox, w_br_conv, w_br_sgu, w_br_dil, w_out, norm_ffn_g, w_up, ffn_conv_w, w_down, norm_ple_g, w_ple_gate, w_ple_proj, final_norm_g):
    B, S, D = x.shape
    depth = w_in.shape[0]
    T = B * S
    d_ff = w_down.shape[1]

    tm = _tile(S, 1024)
    tm_attn = _tile(S, 512)
    tm_big = _tile(S, 1024)
    tm_wide = _tile(S, 512)
    tm_down = _tile(S, 256)
    tq = _tile(S, 512)
    tn_merge = _tile(D, 512)
    tn_ffn = _tile(d_ff, 512)

    sizes = (3 * FOX_WIDTH, FOX_HEADS, 3 * CONV_WIDTH, 2 * SGU_WIDTH, 3 * DIL_WIDTH, N_BRANCH * D)
    offs = [0]
    for n in sizes:
        offs.append(offs[-1] + n)

    bf = lambda w: w.astype(BF16)
    b_f = jnp.pad(fox_forget_b.astype(F32), ((0, 0), (0, LANES - FOX_HEADS)))[:, None, :]
    moves, seg_block, pos = [], {}, 0
    for name, seg, block in (("gate", 5, tn_merge), ("sgu", 3, sizes[3]), ("conv", 2, sizes[2]), ("dil", 4, sizes[4])):
        pos += -pos % block
        seg_block[name] = pos // block
        moves.append((offs[seg], sizes[seg], pos))
        pos += sizes[seg]
    w_fox, w_mix = _repack_w_in(w_in, offs[1] + LANES, tuple(moves), pos)
    w_brs = [bf(w_br_fox), bf(w_br_conv), bf(w_br_sgu), bf(w_br_dil)]
    w_out_b, w_down_b = bf(w_out), bf(w_down)
    w_pg_b, w_pp_b = bf(w_ple_gate), bf(w_ple_proj)
    sgu_bias = jnp.repeat(jnp.swapaxes(sgu_b, 1, 2), SGU_GROUP_DIM, axis=2)
    sgu_norm = sgu_norm_g[:, None, :]
    p_tok = p.reshape(depth, T, p.shape[-1])
    row_vec = lambda g: g[None, :]

    tables = _rope_tables(positions, tm)
    xf = x.reshape(T, D)
    h = _rmsnorm(xf, row_vec(norm_mix_g[0]), tm)
    out = None
    for i in range(depth):
        (qa, ka, va), qkv = _attn_proj(h, w_fox, b_f, w_mix, seg_block["dil"], tables, i, B, S, tm_attn)
        o_a = _fox_attention(qa.reshape(B, S, -1), ka.reshape(B, S, -1), va.reshape(B, S, -1),
                             B, S, tq).reshape(T, FOX_WIDTH)
        o_b, o_c = _conv_sgu_branches(h, w_mix, seg_block["conv"], seg_block["sgu"], shortconv_w, sgu_norm, sgu_w,
                                      sgu_bias, i, S, tm)
        outs, lses = [], []
        for g, (window, dil) in enumerate(DIL_PATTERNS):
            assert window // dil == DIL_SPAN
            o_g, l_g = _dil_attention(*qkv[3 * g:3 * g + 3], dil)
            outs.append(o_g)
            lses.append(l_g)
        o_d = _dil_merge(outs, lses, B, S, tm)

        merged = _gated_merge(h, (o_a, o_b, o_c, o_d), w_mix, seg_block["gate"], w_brs, i, tm_big, tn_merge)
        xf, h = _residual_proj(merged, w_out_b, xf, row_vec(norm_ffn_g[i]), i, tm_wide)

        act = _ffn_up(h, w_up, ffn_conv_w, i, tm_big, tn_ffn, S // tm_big)
        final = i == depth - 1
        g_next = final_norm_g if final else norm_mix_g[i + 1]
        res = _ffn_down_ple(act, w_down_b, xf, row_vec(norm_ple_g[i]), w_pg_b, p_tok, w_pp_b, row_vec(g_next), i,
                            tm_down, final)
        if final:
            out = res[0]
        else:
            xf, h = res
    return out.reshape(B, S, D)
```

```python
import functools

import jax
import jax.numpy as jnp
from jax import lax
from jax.experimental import pallas as pl
from jax.experimental.pallas import tpu as pltpu

F32 = jnp.float32
BF16 = jnp.bfloat16

HEAD_DIM = 64
EPS = 1e-6
FOX_HEADS = 8
FOX_WIDTH = FOX_HEADS * HEAD_DIM
CONV_WIDTH = 512
CONV_TAPS = 3
SGU_GROUPS = 4
SGU_GROUP_DIM = 128
SGU_WIDTH = SGU_GROUPS * SGU_GROUP_DIM
SGU_CHUNK = 128
DIL_PATTERNS = ((128, 1), (512, 4), (2048, 16))
DIL_HEADS_PER_GROUP = 4
DIL_HEADS = DIL_HEADS_PER_GROUP * len(DIL_PATTERNS)
DIL_WIDTH = DIL_HEADS * HEAD_DIM
DIL_OUT = DIL_HEADS_PER_GROUP * HEAD_DIM
DIL_SPAN = 128
ROPE_THETA = 500000.0
ROPE_DIM = HEAD_DIM // 4
N_BRANCH = 4

LANES = 128
SUBLANES = 8
BF16_ROWS = 16
HEADS_PER_SLAB = 4
SLAB = HEADS_PER_SLAB * HEAD_DIM
VMEM_LIMIT = 56 * 1024 * 1024
NEG = -1e30
QK_SCALE = HEAD_DIM ** -0.5


def _params(n_axes):
    return pltpu.CompilerParams(dimension_semantics=("arbitrary",) * n_axes, vmem_limit_bytes=VMEM_LIMIT)


def _resident(shape, index=None):
    index = (0,) * len(shape) if index is None else index
    return pl.BlockSpec(shape, lambda *_: index, pipeline_mode=pl.Buffered(1))


def _layer_weight(layer, rows, cols):
    return _resident((None, rows, cols), (layer, 0, 0))


def _tile(n, pref):
    t = min(n, pref)
    assert n % t == 0, (n, t)
    return t


def _rms(x, g):
    var = jnp.mean(x * x, axis=-1, keepdims=True)
    return x * lax.rsqrt(var + EPS) * g


def _dot(a, b):
    return jnp.dot(a, b, preferred_element_type=F32)


def _dot_nt(a, b):
    return lax.dot_general(a, b, (((1,), (1,)), ((), ())), preferred_element_type=F32)


def _causal_conv3(z, w, prev):
    y = w[0:1] * pltpu.roll(z, 2, 0) + w[1:2] * pltpu.roll(z, 1, 0) + w[2:3] * z
    zh = z[0:BF16_ROWS]
    row = lax.broadcasted_iota(jnp.int32, zh.shape, 0)
    p1 = prev[SUBLANES - 1:SUBLANES]
    p2 = prev[SUBLANES - 2:SUBLANES - 1]
    z1 = jnp.where(row == 0, p1, pltpu.roll(zh, 1, 0))
    z2 = jnp.where(row == 0, p2, jnp.where(row == 1, p1, pltpu.roll(zh, 2, 0)))
    yh = w[0:1] * z2 + w[1:2] * z1 + w[2:3] * zh
    return jnp.concatenate([yh, y[BF16_ROWS:]], axis=0)


def _rmsnorm_kernel(x_ref, g_ref, h_ref):
    h_ref[...] = _rms(x_ref[...], g_ref[...]).astype(h_ref.dtype)


def _rmsnorm(x, g, tm):
    T, D = x.shape
    return pl.pallas_call(
        _rmsnorm_kernel,
        out_shape=jax.ShapeDtypeStruct((T, D), BF16),
        grid=(T // tm,),
        in_specs=[pl.BlockSpec((tm, D), lambda i: (i, 0)), _resident((1, D))],
        out_specs=pl.BlockSpec((tm, D), lambda i: (i, 0)),
        compiler_params=_params(1),
        name="rmsnorm",
    )(x, g)


def _rope_table_kernel(pos_ref, inv_ref, c_ref, s1_ref, s2_ref):
    ang = pos_ref[...] * inv_ref[...]
    cos = jnp.cos(ang)
    sin = jnp.sin(ang)
    lane = lax.broadcasted_iota(jnp.int32, ang.shape, 1) % HEAD_DIM
    half = ROPE_DIM // 2
    c_ref[...] = jnp.where(lane < ROPE_DIM, cos, 1.0)
    s1_ref[...] = jnp.where((lane >= half) & (lane < ROPE_DIM), sin, 0.0)
    s2_ref[...] = jnp.where(lane < half, -sin, 0.0)


def _rope_tables(positions, tm):
    T = positions.size
    half = ROPE_DIM // 2
    inv = ROPE_THETA ** (-jnp.arange(half, dtype=F32) * (2.0 / ROPE_DIM))
    lane = jnp.arange(LANES) % HEAD_DIM
    inv_lane = jnp.where(lane < ROPE_DIM, inv[lane % half], 0.0).astype(F32)[None, :]
    posf = jnp.broadcast_to(positions.reshape(T, 1).astype(F32), (T, LANES))
    spec = pl.BlockSpec((tm, LANES), lambda i: (i, 0))
    return pl.pallas_call(
        _rope_table_kernel,
        out_shape=[jax.ShapeDtypeStruct((T, LANES), F32)] * 3,
        grid=(T // tm,),
        in_specs=[spec, _resident((1, LANES))],
        out_specs=[spec] * 3,
        compiler_params=_params(1),
        name="rope_tables",
    )(posf, inv_lane)


def _repack_kernel(wt_ref, fox_ref, mix_ref, *, fox_width, moves, mix_width):
    def rows_of(src, width):
        return wt_ref[src:src + width, :].T.astype(BF16)

    fox_ref[...] = rows_of(0, fox_width)
    end = 0
    for src, width, dst in moves:
        if dst > end:
            mix_ref[:, end:dst] = jnp.zeros((mix_ref.shape[0], dst - end), BF16)
        mix_ref[:, dst:dst + width] = rows_of(src, width)
        end = dst + width
    assert end == mix_width


def _repack_w_in(w_in, fox_width, moves, mix_width):
    depth, D, d_in = w_in.shape
    assert all(src % SUBLANES == 0 for src, _, _ in moves)
    tr = _tile(D, LANES)
    return pl.pallas_call(
        functools.partial(_repack_kernel, fox_width=fox_width, moves=moves, mix_width=mix_width),
        out_shape=[jax.ShapeDtypeStruct((depth, D, fox_width), BF16),
                   jax.ShapeDtypeStruct((depth, D, mix_width), BF16)],
        grid=(depth, D // tr),
        in_specs=[pl.BlockSpec((None, d_in, tr), lambda l, r: (l, 0, r))],
        out_specs=[pl.BlockSpec((None, tr, fox_width), lambda l, r: (l, r, 0)),
                   pl.BlockSpec((None, tr, mix_width), lambda l, r: (l, r, 0))],
        compiler_params=_params(2),
        name="repack_w_in",
    )(jnp.swapaxes(w_in, 1, 2))


def _fox_proj_kernel(h_ref, w_ref, wf_ref, bf_ref, q_ref, k_ref, v_ref, carry_ref, *, tiles_per_seq):
    @pl.when(pl.program_id(0) % tiles_per_seq == 0)
    def _():
        carry_ref[...] = jnp.zeros_like(carry_ref)

    h = h_ref[...]
    W = FOX_WIDTH
    q_ref[...] = (_dot(h, w_ref[:, 0:W]) * QK_SCALE).astype(BF16)
    k = _dot(h, w_ref[:, W:2 * W]).astype(BF16)
    v = _dot(h, w_ref[:, 2 * W:3 * W]).astype(BF16)

    x = jax.nn.log_sigmoid(_dot(h, wf_ref[...]) + bf_ref[...])
    tm = x.shape[0]
    row = lax.broadcasted_iota(jnp.int32, x.shape, 0)
    lane = lax.broadcasted_iota(jnp.int32, x.shape, 1)
    s = 1
    while s < tm:
        x = x + jnp.where(row >= s, pltpu.roll(x, s, 0), 0.0)
        s *= 2
    x = x + carry_ref[0:1, :]
    carry_ref[...] = jnp.broadcast_to(x[tm - 1:tm, :], carry_ref.shape)

    x = jnp.where(lane < FOX_HEADS, x, 0.0)
    hi = x.astype(BF16).astype(F32)
    mid = (x - hi).astype(BF16).astype(F32)
    lo = (x - hi - mid).astype(BF16).astype(F32)
    f_lanes = (hi + pltpu.roll(mid, FOX_HEADS, 1) + pltpu.roll(lo, 2 * FOX_HEADS, 1)).astype(BF16)
    one_lane = jnp.where(lane == 0, 1.0, 0.0).astype(BF16)
    for j in range(FOX_WIDTH // LANES):
        k_ref[:, j * SLAB:j * SLAB + LANES] = k[:, j * LANES:(j + 1) * LANES]
        k_ref[:, j * SLAB + LANES:(j + 1) * SLAB] = f_lanes
        v_ref[:, j * SLAB:j * SLAB + LANES] = v[:, j * LANES:(j + 1) * LANES]
        v_ref[:, j * SLAB + LANES:(j + 1) * SLAB] = one_lane


def _conv_branch_finish(dots, cw_ref, o_ref, prev_ref):
    xb, gate_b, gate_c = dots
    tm = xb.shape[0]
    z = gate_c * xb
    y = _causal_conv3(z, cw_ref[...], prev_ref[...])
    prev_ref[...] = z[tm - SUBLANES:, :]
    o_ref[...] = (gate_b * y).astype(BF16)


def _sgu_branch_finish(dots, ng_ref, ws_ref, bias_ref, o_ref):
    C = SGU_CHUNK
    G = SGU_GROUP_DIM
    u = jax.nn.gelu(dots[0])
    v = _rms(jax.nn.gelu(dots[1]), ng_ref[...]).astype(BF16)
    row = lax.broadcasted_iota(jnp.int32, (C, C), 0)
    col = lax.broadcasted_iota(jnp.int32, (C, C), 1)
    for g in range(SGU_GROUPS):
        w_tril = jnp.where(col <= row, ws_ref[g], 0.0).astype(BF16)
        cols = slice(g * G, (g + 1) * G)
        for c in range(u.shape[0] // C):
            rows = slice(c * C, (c + 1) * C)
            mixed = _dot(w_tril, v[rows, cols]) + bias_ref[:, cols]
            o_ref[rows, cols] = (u[rows, cols] * mixed).astype(BF16)


def _column_dots(h, w_ref, width, n):
    return [_dot(h, w_ref[:, j * width:(j + 1) * width]) for j in range(n)]


def _dil_proj_kernel(h_ref, w_ref, c_ref, s1_ref, s2_ref, *refs):
    n_groups = len(DIL_PATTERNS)
    out_refs = refs[:3 * n_groups]
    stage_ref = refs[3 * n_groups]
    h = h_ref[...]
    tm = h.shape[0]
    cos = c_ref[...]
    s1 = s1_ref[...]
    s2 = s2_ref[...]
    shift = ROPE_DIM // 2
    slot = 0
    kinds = ((0, True, QK_SCALE), (1, True, None), (2, False, None))
    wide = [_dot(h, w_ref[:, which * DIL_WIDTH:(which + 1) * DIL_WIDTH]) for which, _, _ in kinds]
    for (which, rope, scale), x_all in zip(kinds, wide):
        for g, (_, d) in enumerate(DIL_PATTERNS):
            parts = []
            for j in range(SLAB // LANES):
                xj = x_all[:, g * SLAB + j * LANES:g * SLAB + (j + 1) * LANES]
                if rope:
                    xj = xj * cos + pltpu.roll(xj, shift, 1) * s1 + pltpu.roll(xj, LANES - shift, 1) * s2
                if scale is not None:
                    xj = xj * scale
                parts.append(xj)
            out = out_refs[g * 3 + which]
            for j, xj in enumerate(parts):
                lanes = slice(j * LANES, (j + 1) * LANES)
                if d == 1:
                    out[0, 0, :, lanes] = xj.astype(BF16)
                else:
                    stage_ref[slot] = xj
                    for r in range(d):
                        out[0, r, :, lanes] = stage_ref[slot, pl.ds(r, tm // d, stride=d), :].astype(BF16)
                    slot += 1


N_FOX_PROJ_IN, N_FOX_PROJ_OUT = 3, 3


def _attn_proj_kernel(h_ref, *refs, tiles_per_seq):
    n_dil_out = 3 * len(DIL_PATTERNS)
    fox_in, refs = refs[:N_FOX_PROJ_IN], refs[N_FOX_PROJ_IN:]
    dil_in, refs = refs[:4], refs[4:]
    fox_out, refs = refs[:N_FOX_PROJ_OUT], refs[N_FOX_PROJ_OUT:]
    dil_out, (carry_ref, stage_ref) = refs[:n_dil_out], refs[n_dil_out:]
    _fox_proj_kernel(h_ref, *fox_in, *fox_out, carry_ref, tiles_per_seq=tiles_per_seq)
    _dil_proj_kernel(h_ref, *dil_in, *dil_out, stage_ref)


def _attn_proj(h, w_fox, b_f, w_mix, dil_block, tables, layer, B, S, tm):
    T, D = h.shape
    W = FOX_WIDTH
    n_pairs = FOX_WIDTH // LANES
    tiles_per_seq = S // tm
    row = lambda n: pl.BlockSpec((tm, n), lambda i: (i, 0))
    out_shape = [jax.ShapeDtypeStruct((T, W), BF16)] + [jax.ShapeDtypeStruct((T, n_pairs * SLAB), BF16)] * 2
    out_specs = [row(W), row(n_pairs * SLAB), row(n_pairs * SLAB)]
    n_strided = 0
    for _, d in DIL_PATTERNS:
        assert (tm // d) % BF16_ROWS == 0
        n_strided += 3 * (SLAB // LANES) * (d > 1)
        for _ in range(3):
            out_shape.append(jax.ShapeDtypeStruct((B, d, S // d, SLAB), BF16))
            out_specs.append(pl.BlockSpec((1, d, tm // d, SLAB),
                                          lambda i: (i // tiles_per_seq, 0, i % tiles_per_seq, 0)))
    res = pl.pallas_call(
        functools.partial(_attn_proj_kernel, tiles_per_seq=tiles_per_seq),
        out_shape=out_shape,
        grid=(T // tm,),
        in_specs=[row(D), _resident((None, D, 3 * W), (layer, 0, 0)),
                  _resident((None, D, LANES), (layer, 0, 3 * W // LANES)), _layer_weight(layer, 1, LANES),
                  _resident((None, D, 3 * DIL_WIDTH), (layer, 0, dil_block)), row(LANES), row(LANES), row(LANES)],
        out_specs=out_specs,
        scratch_shapes=[pltpu.VMEM((SUBLANES, LANES), F32), pltpu.VMEM((n_strided, tm, LANES), F32)],
        compiler_params=_params(1),
        name="attn_proj",
    )(h, w_fox, w_fox, b_f, w_mix, *tables)
    return res[:3], res[3:]


def _conv_sgu_kernel(h_ref, wc_ref, cw_ref, ws_ref, ng_ref, wsp_ref, bias_ref, ob_ref, oc_ref, prev_ref, *,
                     tiles_per_seq):
    @pl.when(pl.program_id(0) % tiles_per_seq == 0)
    def _():
        prev_ref[...] = jnp.zeros_like(prev_ref)

    h = h_ref[...]
    conv_dots = _column_dots(h, wc_ref, CONV_WIDTH, 3)
    sgu_dots = _column_dots(h, ws_ref, SGU_WIDTH, 2)
    _conv_branch_finish(conv_dots, cw_ref, ob_ref, prev_ref)
    _sgu_branch_finish(sgu_dots, ng_ref, wsp_ref, bias_ref, oc_ref)


def _conv_sgu_branches(h, w_mix, conv_block, sgu_block, conv_w, norm_g, w_s, bias_full, layer, S, tm):
    T, D = h.shape
    row = lambda n: pl.BlockSpec((tm, n), lambda i: (i, 0))
    return pl.pallas_call(
        functools.partial(_conv_sgu_kernel, tiles_per_seq=S // tm),
        out_shape=[jax.ShapeDtypeStruct((T, CONV_WIDTH), BF16), jax.ShapeDtypeStruct((T, SGU_WIDTH), BF16)],
        grid=(T // tm,),
        in_specs=[row(D), _resident((None, D, 3 * CONV_WIDTH), (layer, 0, conv_block)),
                  _layer_weight(layer, CONV_TAPS, CONV_WIDTH),
                  _resident((None, D, 2 * SGU_WIDTH), (layer, 0, sgu_block)), _layer_weight(layer, 1, SGU_WIDTH),
                  _resident((None, SGU_GROUPS, SGU_CHUNK, SGU_CHUNK), (layer, 0, 0, 0)),
                  _layer_weight(layer, SGU_CHUNK, SGU_WIDTH)],
        out_specs=[row(CONV_WIDTH), row(SGU_WIDTH)],
        scratch_shapes=[pltpu.VMEM((SUBLANES, CONV_WIDTH), F32)],
        compiler_params=_params(1),
        name="conv_sgu_branches",
    )(h, w_mix, conv_w, w_mix, norm_g, w_s, bias_full)


def _slab_head_masks(n_rows):
    lane = lax.broadcasted_iota(jnp.int32, (n_rows, SLAB), 1)
    return [(lane >= h * HEAD_DIM) & (lane < (h + 1) * HEAD_DIM) for h in range(HEADS_PER_SLAB)]


FOX_BLOCKS_PER_ITER = 4
FOX_PAIRS_PER_STEP = 2


def _fox_attn_kernel(q_ref, k_ref, v_ref, o_ref, *, tq):
    group = pl.program_id(1)
    i = pl.program_id(2)
    q0 = pl.multiple_of(i * tq, tq)
    lane = lax.broadcasted_iota(jnp.int32, (tq, LANES), 1)
    rowi = lax.broadcasted_iota(jnp.int32, (2 * tq, tq), 0) % tq
    coli = lax.broadcasted_iota(jnp.int32, (2 * tq, tq), 1)

    def stacked_queries(c):
        q = q_ref[0, :, c * LANES:(c + 1) * LANES]
        zero = jnp.zeros_like(q)
        halves = []
        for e in range(2):
            head = 2 * (FOX_PAIRS_PER_STEP * group + c) + e
            f_pick = (lane == head) | (lane == head + FOX_HEADS) | (lane == head + 2 * FOX_HEADS)
            halves.append(jnp.concatenate(
                [jnp.where(lane // HEAD_DIM == e, q, zero), jnp.where(f_pick, -1.0, 0.0).astype(BF16)], axis=1))
        return jnp.concatenate(halves, axis=0)

    qs = [stacked_queries(c) for c in range(FOX_PAIRS_PER_STEP)]

    def block(k0, carry, diagonal):
        slabs = [slice(c * SLAB, (c + 1) * SLAB) for c in range(FOX_PAIRS_PER_STEP)]
        logits = [_dot_nt(qs[c], k_ref[0, pl.ds(k0, tq), slabs[c]]) for c in range(FOX_PAIRS_PER_STEP)]
        stats = []
        for (m, _), s in zip(carry, logits):
            if diagonal:
                s = jnp.where(coli <= rowi, s, NEG)
            m_new = jnp.maximum(m, jnp.max(s, axis=-1, keepdims=True))
            stats.append((m_new, jnp.exp(m - m_new), jnp.exp(s - m_new).astype(BF16)))
        return tuple((m_new, alpha * acc + _dot(p, v_ref[0, pl.ds(k0, tq), slabs[c]]))
                     for c, ((_, acc), (m_new, alpha, p)) in enumerate(zip(carry, stats)))

    init = tuple((jnp.full((2 * tq, 1), NEG, F32), jnp.zeros((2 * tq, SLAB), F32))
                 for _ in range(FOX_PAIRS_PER_STEP))
    def run_blocks(first_block, n, carry):
        for j in range(n):
            carry = block(pl.multiple_of((first_block + j) * tq, tq), carry, False)
        return carry

    def finish(carry):
        for c, (_, acc) in enumerate(block(q0, carry, True)):
            out = acc[:, 0:LANES] / acc[:, LANES:LANES + 1]
            o_ref[0, :, c * LANES:(c + 1) * LANES] = jnp.where(lane // HEAD_DIM == 0, out[0:tq],
                                                               out[tq:2 * tq]).astype(BF16)

    U = FOX_BLOCKS_PER_ITER
    carry = lax.fori_loop(0, i // U, lambda g, c: run_blocks(g * U, U, c), init)
    for rem in range(U):
        @pl.when(i % U == rem)
        def _(rem=rem):
            finish(run_blocks(i - rem, rem, carry))


def _fox_attention(q, k_slabs, v_slabs, B, S, tq):
    n_groups = FOX_WIDTH // LANES // FOX_PAIRS_PER_STEP
    kv_spec = pl.BlockSpec((1, S, FOX_PAIRS_PER_STEP * SLAB), lambda b, g, i: (b, 0, g))
    q_spec = pl.BlockSpec((1, tq, FOX_PAIRS_PER_STEP * LANES), lambda b, g, i: (b, i, g))
    return pl.pallas_call(
        functools.partial(_fox_attn_kernel, tq=tq),
        out_shape=jax.ShapeDtypeStruct((B, S, FOX_WIDTH), BF16),
        grid=(B, n_groups, S // tq),
        in_specs=[q_spec, kv_spec, kv_spec],
        out_specs=q_spec,
        compiler_params=_params(3),
        name="fox_attention",
    )(q, k_slabs, v_slabs)


DIL_BLOCKS_PER_STEP = 8


def _dil_attn_kernel(q_ref, k_ref, v_ref, o_ref, lse_ref, *, n_blocks):
    P = DIL_SPAN
    H = HEADS_PER_SLAB
    masks = _slab_head_masks(P)

    def attend(r, q_rows, k_rows, valid):
        q = q_ref[r, q_rows, :]
        zero = jnp.zeros_like(q)
        qs = jnp.concatenate([jnp.where(masks[h], q, zero) for h in range(H)], axis=0)
        s = jnp.where(valid, _dot_nt(qs, k_ref[r, k_rows, :]), NEG)
        m = jnp.max(s, axis=-1, keepdims=True)
        p = jnp.exp(s - m)
        l = jnp.sum(p, axis=-1, keepdims=True)
        pv = _dot(p.astype(BF16), v_ref[r, k_rows, :]) / l
        lse = jnp.broadcast_to(m + jnp.log(l), (H * P, SLAB))
        o, ls = pv[0:P], lse[0:P]
        for h in range(1, H):
            o = jnp.where(masks[h], pv[h * P:(h + 1) * P], o)
            ls = jnp.where(masks[h], lse[h * P:(h + 1) * P], ls)
        o_ref[r, q_rows, :] = o
        lse_ref[r, q_rows, :] = ls

    qi = lax.broadcasted_iota(jnp.int32, (H * P, P), 0) % P
    ki = lax.broadcasted_iota(jnp.int32, (H * P, P), 1)
    qi2 = lax.broadcasted_iota(jnp.int32, (H * P, 2 * P), 0) % P
    ki2 = lax.broadcasted_iota(jnp.int32, (H * P, 2 * P), 1)
    valid2 = (ki2 >= qi2) & (ki2 <= qi2 + P)

    for r in range(q_ref.shape[0]):
        attend(r, pl.ds(0, P), pl.ds(0, P), ki <= qi)

        def body(n, _, r=r):
            start = pl.multiple_of(n * P, P)
            attend(r, pl.ds(start, P), pl.ds(pl.multiple_of(start - P, P), 2 * P), valid2)
            return 0

        lax.fori_loop(1, n_blocks, body, 0, unroll=min(n_blocks - 1, DIL_BLOCKS_PER_STEP))


def _dil_attention(q, k, v, dilation):
    B, d, L, _ = q.shape
    assert d == dilation
    n_blocks = L // DIL_SPAN
    per_step = max(1, min(d, DIL_BLOCKS_PER_STEP // n_blocks))
    spec = pl.BlockSpec((None, per_step, L, SLAB), lambda b, r: (b, r, 0, 0))
    return pl.pallas_call(
        functools.partial(_dil_attn_kernel, n_blocks=n_blocks),
        out_shape=[jax.ShapeDtypeStruct((B, d, L, SLAB), F32)] * 2,
        grid=(B, d // per_step),
        in_specs=[spec] * 3,
        out_specs=[spec] * 2,
        compiler_params=_params(2),
        name=f"dil_attention_d{dilation}",
    )(q, k, v)


def _dil_merge_kernel(*refs):
    n_groups = len(DIL_PATTERNS)
    o_refs, l_refs = refs[:n_groups], refs[n_groups:2 * n_groups]
    out_ref, stage_ref = refs[2 * n_groups], refs[2 * n_groups + 1]
    tm = out_ref.shape[0]
    slot = 0
    outs, lses = [], []
    for src, dst in ((o_refs, outs), (l_refs, lses)):
        for (_, d), ref in zip(DIL_PATTERNS, src):
            if d == 1:
                dst.append(ref[0, 0])
                continue
            halves = []
            for j in range(SLAB // LANES):
                for r in range(d):
                    stage_ref[slot, pl.ds(r, tm // d, stride=d), :] = ref[0, r, :, j * LANES:(j + 1) * LANES]
                halves.append(stage_ref[slot])
                slot += 1
            dst.append(jnp.concatenate(halves, axis=1))
    m = functools.reduce(jnp.maximum, lses)
    es = [jnp.exp(l - m) for l in lses]
    num = es[0] * outs[0] + es[1] * outs[1] + es[2] * outs[2]
    out_ref[...] = (num / (es[0] + es[1] + es[2])).astype(BF16)


def _dil_merge(outs, lses, B, S, tm):
    T = B * S
    tiles_per_seq = S // tm
    specs = [pl.BlockSpec((1, d, tm // d, SLAB), lambda i: (i // tiles_per_seq, 0, i % tiles_per_seq, 0))
             for _, d in DIL_PATTERNS]
    n_strided = 2 * (SLAB // LANES) * sum(d > 1 for _, d in DIL_PATTERNS)
    return pl.pallas_call(
        _dil_merge_kernel,
        out_shape=jax.ShapeDtypeStruct((T, SLAB), BF16),
        grid=(T // tm,),
        in_specs=specs * 2,
        out_specs=pl.BlockSpec((tm, SLAB), lambda i: (i, 0)),
        scratch_shapes=[pltpu.VMEM((n_strided, tm, LANES), F32)],
        compiler_params=_params(1),
        name="dil_merge",
    )(*outs, *lses)


def _gated_merge_kernel(h_ref, oa_ref, ob_ref, oc_ref, od_ref, g0, g1, g2, g3, wa, wb, wc, wd, out_ref):
    h = h_ref[...]
    acc = None
    for o_ref, g_ref, w_ref in ((oa_ref, g0, wa), (ob_ref, g1, wb), (oc_ref, g2, wc), (od_ref, g3, wd)):
        term = jax.nn.sigmoid(_dot(h, g_ref[...])) * _dot(o_ref[...], w_ref[...])
        acc = term if acc is None else acc + term
    out_ref[...] = acc.astype(BF16)


def _gated_merge(h, branches, w_mix, gate_block, w_branches, layer, tm, tn):
    T, D = h.shape
    n_col = D // tn
    row = lambda n: pl.BlockSpec((tm, n), lambda j, i: (i, 0))
    gate_specs = [pl.BlockSpec((None, D, tn), functools.partial(lambda j, i, br: (layer, 0, gate_block + br * n_col + j), br=br))
                  for br in range(N_BRANCH)]
    w_specs = [pl.BlockSpec((None, w.shape[1], tn), lambda j, i: (layer, 0, j)) for w in w_branches]
    return pl.pallas_call(
        _gated_merge_kernel,
        out_shape=jax.ShapeDtypeStruct((T, D), BF16),
        grid=(n_col, T // tm),
        in_specs=[row(D)] + [row(o.shape[1]) for o in branches] + gate_specs + w_specs,
        out_specs=pl.BlockSpec((tm, tn), lambda j, i: (i, j)),
        compiler_params=_params(2),
        name="gated_merge",
    )(h, *branches, w_mix, w_mix, w_mix, w_mix, *w_branches)


def _residual_proj_kernel(a_ref, w_ref, x_ref, g_ref, xo_ref, h_ref):
    x = x_ref[...] + _dot(a_ref[...], w_ref[...])
    xo_ref[...] = x
    h_ref[...] = _rms(x, g_ref[...]).astype(h_ref.dtype)


def _residual_proj(a, w, x, g_next, layer, tm):
    T, D = x.shape
    K = a.shape[1]
    row = lambda n: pl.BlockSpec((tm, n), lambda i: (i, 0))
    return pl.pallas_call(
        _residual_proj_kernel,
        out_shape=[jax.ShapeDtypeStruct((T, D), F32), jax.ShapeDtypeStruct((T, D), BF16)],
        grid=(T // tm,),
        in_specs=[row(K), _layer_weight(layer, K, D), row(D), _resident((1, D))],
        out_specs=[row(D), row(D)],
        compiler_params=_params(1),
        name=f"residual_proj_k{K}",
    )(a, w, x, g_next)


FFN_CHUNK = 2 * SLAB
SHIFT_PAD = 16


def _shifted_rows(z, s_ref):
    tm = z.shape[0]
    half = tm // 2
    P = SHIFT_PAD
    s_ref[P - 2:P - 1, :] = s_ref[P + tm - 1:P + tm, :]
    s_ref[P - 4:P - 3, :] = s_ref[P + tm - 3:P + tm - 2, :]
    s_ref[pl.ds(P, half, stride=2), :] = z[:half]
    s_ref[pl.ds(P + 1, half, stride=2), :] = z[half:]
    s_ref[P - 1:P, :] = z[half - 1:half]
    s_ref[P - 3:P - 2, :] = z[half - 2:half - 1]
    z1 = jnp.concatenate([s_ref[pl.ds(P - 2, half, stride=2), :], s_ref[pl.ds(P - 1, half, stride=2), :]], axis=0)
    z2 = jnp.concatenate([s_ref[pl.ds(P - 4, half, stride=2), :], s_ref[pl.ds(P - 3, half, stride=2), :]], axis=0)
    return z1, z2


def _ffn_up_kernel(h_ref, wg_ref, wv_ref, cg_ref, cv_ref, act_ref, wg_bf, wv_bf, shift_ref, *, tiles_per_seq):
    i = pl.program_id(1)
    tm = h_ref.shape[0]

    @pl.when(i == 0)
    def _():
        wg_bf[...] = wg_ref[...].astype(BF16)
        wv_bf[...] = wv_ref[...].astype(BF16)

    @pl.when(i % tiles_per_seq == 0)
    def _():
        shift_ref[:, SHIFT_PAD + tm - SUBLANES:, :] = jnp.zeros((shift_ref.shape[0], SUBLANES, LANES), F32)

    h = h_ref[...]

    def conv(z, w_ref, col, buf):
        parts = []
        for j in range(FFN_CHUNK // LANES):
            zj = z[:, j * LANES:(j + 1) * LANES]
            z1, z2 = _shifted_rows(zj, shift_ref.at[buf + j])
            w = w_ref[:, col + j * LANES:col + (j + 1) * LANES]
            parts.append(w[0:1] * z2 + w[1:2] * z1 + w[2:3] * zj)
        return jnp.concatenate(parts, axis=1)

    lane_blocks = act_ref.shape[1] // LANES
    for c in range(act_ref.shape[1] // FFN_CHUNK):
        col = c * FFN_CHUNK
        cols = slice(col, col + FFN_CHUNK)
        up_gate = conv(_dot(h, wg_bf[:, cols]), cg_ref, col, col // LANES)
        up_val = conv(_dot(h, wv_bf[:, cols]), cv_ref, col, lane_blocks + col // LANES)
        act_ref[:, cols] = (jax.nn.silu(up_gate) * up_val).astype(BF16)


def _ffn_up(h, w_up, conv_w, layer, tm, tn, tiles_per_seq):
    T, D = h.shape
    d_ff = w_up.shape[2] // 2
    n_col = d_ff // tn
    return pl.pallas_call(
        functools.partial(_ffn_up_kernel, tiles_per_seq=tiles_per_seq),
        out_shape=jax.ShapeDtypeStruct((T, d_ff), BF16),
        grid=(n_col, T // tm),
        in_specs=[pl.BlockSpec((tm, D), lambda j, i: (i, 0)),
                  pl.BlockSpec((None, D, tn), lambda j, i: (layer, 0, j)),
                  pl.BlockSpec((None, D, tn), lambda j, i: (layer, 0, n_col + j)),
                  pl.BlockSpec((None, CONV_TAPS, tn), lambda j, i: (layer, 0, j)),
                  pl.BlockSpec((None, CONV_TAPS, tn), lambda j, i: (layer, 0, n_col + j))],
        out_specs=pl.BlockSpec((tm, tn), lambda j, i: (i, j)),
        scratch_shapes=[pltpu.VMEM((D, tn), BF16)] * 2 + [pltpu.VMEM((2 * tn // LANES, SHIFT_PAD + tm, LANES), F32)],
        compiler_params=_params(2),
        name="ffn_up",
    )(h, w_up, w_up, conv_w, conv_w)


def _ffn_down_ple_kernel(a_ref, wd_ref, x_ref, gp_ref, wg_ref, p_ref, wp_ref, gn_ref, *out_refs, final):
    x = x_ref[...] + _dot(a_ref[...], wd_ref[...])
    h = _rms(x, gp_ref[...]).astype(BF16)
    gate = jax.nn.sigmoid(_dot(h, wg_ref[...]))
    x = x + gate * _dot(p_ref[...].astype(BF16), wp_ref[...])
    if not final:
        out_refs[0][...] = x
    out_refs[-1][...] = _rms(x, gn_ref[...]).astype(out_refs[-1].dtype)


def _ffn_down_ple(act, w_down, x, g_ple, w_gate, p, w_proj, g_next, layer, tm, final):
    T, D = x.shape
    K = act.shape[1]
    P = p.shape[2]
    row = lambda n: pl.BlockSpec((tm, n), lambda i: (i, 0))
    if final:
        out_shape = [jax.ShapeDtypeStruct((T, D), F32)]
    else:
        out_shape = [jax.ShapeDtypeStruct((T, D), F32), jax.ShapeDtypeStruct((T, D), BF16)]
    return pl.pallas_call(
        functools.partial(_ffn_down_ple_kernel, final=final),
        out_shape=out_shape,
        grid=(T // tm,),
        in_specs=[row(K), _layer_weight(layer, K, D), row(D), _resident((1, D)), _layer_weight(layer, D, D),
                  pl.BlockSpec((None, tm, P), lambda i: (layer, i, 0)), _layer_weight(layer, P, D),
                  _resident((1, D))],
        out_specs=[row(D)] * len(out_shape),
        compiler_params=_params(1),
        name="ffn_down_ple_final" if final else "ffn_down_ple",
    )(act, w_down, x, g_ple, w_gate, p, w_proj, g_next)


def kernel(x, p, positions, norm_mix_g, w_in, fox_forget_b, shortconv_w, sgu_norm_g, sgu_w, sgu_b, w_br_fox, w_br_conv, w_br_sgu, w_br_dil, w_out, norm_ffn_g, w_up, ffn_conv_w, w_down, norm_ple_g, w_ple_gate, w_ple_proj, final_norm_g):
    B, S, D = x.shape
    depth = w_in.shape[0]
    T = B * S
    d_ff = w_down.shape[1]

    tm = _tile(S, 1024)
    tm_attn = _tile(S, 512)
    tm_big = _tile(S, 1024)
    tm_wide = _tile(S, 512)
    tm_down = _tile(S, 256)
    tq = _tile(S, 512)
    tn_merge = _tile(D, 512)
    tn_ffn = _tile(d_ff, 512)

    sizes = (3 * FOX_WIDTH, FOX_HEADS, 3 * CONV_WIDTH, 2 * SGU_WIDTH, 3 * DIL_WIDTH, N_BRANCH * D)
    offs = [0]
    for n in sizes:
        offs.append(offs[-1] + n)

    bf = lambda w: w.astype(BF16)
    b_f = jnp.pad(fox_forget_b.astype(F32), ((0, 0), (0, LANES - FOX_HEADS)))[:, None, :]
    moves, seg_block, pos = [], {}, 0
    for name, seg, block in (("gate", 5, tn_merge), ("sgu", 3, sizes[3]), ("conv", 2, sizes[2]), ("dil", 4, sizes[4])):
        pos += -pos % block
        seg_block[name] = pos // block
        moves.append((offs[seg], sizes[seg], pos))
        pos += sizes[seg]
    w_fox, w_mix = _repack_w_in(w_in, offs[1] + LANES, tuple(moves), pos)
    w_brs = [bf(w_br_fox), bf(w_br_conv), bf(w_br_sgu), bf(w_br_dil)]
    w_out_b, w_down_b = bf(w_out), bf(w_down)
    w_pg_b, w_pp_b = bf(w_ple_gate), bf(w_ple_proj)
    sgu_bias = jnp.repeat(jnp.swapaxes(sgu_b, 1, 2), SGU_GROUP_DIM, axis=2)
    sgu_norm = sgu_norm_g[:, None, :]
    p_tok = p.reshape(depth, T, p.shape[-1])
    row_vec = lambda g: g[None, :]

    tables = _rope_tables(positions, tm)
    xf = x.reshape(T, D)
    h = _rmsnorm(xf, row_vec(norm_mix_g[0]), tm)
    out = None
    for i in range(depth):
        (qa, ka, va), qkv = _attn_proj(h, w_fox, b_f, w_mix, seg_block["dil"], tables, i, B, S, tm_attn)
        o_a = _fox_attention(qa.reshape(B, S, -1), ka.reshape(B, S, -1), va.reshape(B, S, -1),
                             B, S, tq).reshape(T, FOX_WIDTH)
        o_b, o_c = _conv_sgu_branches(h, w_mix, seg_block["conv"], seg_block["sgu"], shortconv_w, sgu_norm, sgu_w,
                                      sgu_bias, i, S, tm)
        outs, lses = [], []
        for g, (window, dil) in enumerate(DIL_PATTERNS):
            assert window // dil == DIL_SPAN
            o_g, l_g = _dil_attention(*qkv[3 * g:3 * g + 3], dil)
            outs.append(o_g)
            lses.append(l_g)
        o_d = _dil_merge(outs, lses, B, S, tm)

        merged = _gated_merge(h, (o_a, o_b, o_c, o_d), w_mix, seg_block["gate"], w_brs, i, tm_big, tn_merge)
        xf, h = _residual_proj(merged, w_out_b, xf, row_vec(norm_ffn_g[i]), i, tm_wide)

        act = _ffn_up(h, w_up, ffn_conv_w, i, tm_big, tn_ffn, S // tm_big)
        final = i == depth - 1
        g_next = final_norm_g if final else norm_mix_g[i + 1]
        res = _ffn_down_ple(act, w_down_b, xf, row_vec(norm_ple_g[i]), w_pg_b, p_tok, w_pp_b, row_vec(g_next), i,
                            tm_down, final)
        if final:
            out = res[0]
        else:
            xf, h = res
    return out.reshape(B, S, D)
```

```python
import functools

import jax
import jax.numpy as jnp
from jax import lax
from jax.experimental import pallas as pl
from jax.experimental.pallas import tpu as pltpu

F32 = jnp.float32
BF16 = jnp.bfloat16

HEAD_DIM = 64
EPS = 1e-6
FOX_HEADS = 8
FOX_WIDTH = FOX_HEADS * HEAD_DIM
CONV_WIDTH = 512
CONV_TAPS = 3
SGU_GROUPS = 4
SGU_GROUP_DIM = 128
SGU_WIDTH = SGU_GROUPS * SGU_GROUP_DIM
SGU_CHUNK = 128
DIL_PATTERNS = ((128, 1), (512, 4), (2048, 16))
DIL_HEADS_PER_GROUP = 4
DIL_HEADS = DIL_HEADS_PER_GROUP * len(DIL_PATTERNS)
DIL_WIDTH = DIL_HEADS * HEAD_DIM
DIL_OUT = DIL_HEADS_PER_GROUP * HEAD_DIM
DIL_SPAN = 128
ROPE_THETA = 500000.0
ROPE_DIM = HEAD_DIM // 4
N_BRANCH = 4

LANES = 128
SUBLANES = 8
BF16_ROWS = 16
HEADS_PER_SLAB = 4
SLAB = HEADS_PER_SLAB * HEAD_DIM
VMEM_LIMIT = 56 * 1024 * 1024
NEG = -1e30
QK_SCALE = HEAD_DIM ** -0.5


def _params(n_axes):
    return pltpu.CompilerParams(dimension_semantics=("arbitrary",) * n_axes, vmem_limit_bytes=VMEM_LIMIT)


def _resident(shape, index=None):
    index = (0,) * len(shape) if index is None else index
    return pl.BlockSpec(shape, lambda *_: index, pipeline_mode=pl.Buffered(1))


def _layer_weight(layer, rows, cols):
    return _resident((None, rows, cols), (layer, 0, 0))


def _tile(n, pref):
    t = min(n, pref)
    assert n % t == 0, (n, t)
    return t


def _rms(x, g):
    var = jnp.mean(x * x, axis=-1, keepdims=True)
    return x * lax.rsqrt(var + EPS) * g


def _dot(a, b):
    return jnp.dot(a, b, preferred_element_type=F32)


def _dot_nt(a, b):
    return lax.dot_general(a, b, (((1,), (1,)), ((), ())), preferred_element_type=F32)


def _causal_conv3(z, w, prev):
    y = w[0:1] * pltpu.roll(z, 2, 0) + w[1:2] * pltpu.roll(z, 1, 0) + w[2:3] * z
    zh = z[0:BF16_ROWS]
    row = lax.broadcasted_iota(jnp.int32, zh.shape, 0)
    p1 = prev[SUBLANES - 1:SUBLANES]
    p2 = prev[SUBLANES - 2:SUBLANES - 1]
    z1 = jnp.where(row == 0, p1, pltpu.roll(zh, 1, 0))
    z2 = jnp.where(row == 0, p2, jnp.where(row == 1, p1, pltpu.roll(zh, 2, 0)))
    yh = w[0:1] * z2 + w[1:2] * z1 + w[2:3] * zh
    return jnp.concatenate([yh, y[BF16_ROWS:]], axis=0)


def _rmsnorm_kernel(x_ref, g_ref, h_ref):
    h_ref[...] = _rms(x_ref[...], g_ref[...]).astype(h_ref.dtype)


def _rmsnorm(x, g, tm):
    T, D = x.shape
    return pl.pallas_call(
        _rmsnorm_kernel,
        out_shape=jax.ShapeDtypeStruct((T, D), BF16),
        grid=(T // tm,),
        in_specs=[pl.BlockSpec((tm, D), lambda i: (i, 0)), _resident((1, D))],
        out_specs=pl.BlockSpec((tm, D), lambda i: (i, 0)),
        compiler_params=_params(1),
        name="rmsnorm",
    )(x, g)


def _rope_table_kernel(pos_ref, inv_ref, c_ref, s1_ref, s2_ref):
    ang = pos_ref[...] * inv_ref[...]
    cos = jnp.cos(ang)
    sin = jnp.sin(ang)
    lane = lax.broadcasted_iota(jnp.int32, ang.shape, 1) % HEAD_DIM
    half = ROPE_DIM // 2
    c_ref[...] = jnp.where(lane < ROPE_DIM, cos, 1.0)
    s1_ref[...] = jnp.where((lane >= half) & (lane < ROPE_DIM), sin, 0.0)
    s2_ref[...] = jnp.where(lane < half, -sin, 0.0)


def _rope_tables(positions, tm):
    T = positions.size
    half = ROPE_DIM // 2
    inv = ROPE_THETA ** (-jnp.arange(half, dtype=F32) * (2.0 / ROPE_DIM))
    lane = jnp.arange(LANES) % HEAD_DIM
    inv_lane = jnp.where(lane < ROPE_DIM, inv[lane % half], 0.0).astype(F32)[None, :]
    posf = jnp.broadcast_to(positions.reshape(T, 1).astype(F32), (T, LANES))
    spec = pl.BlockSpec((tm, LANES), lambda i: (i, 0))
    return pl.pallas_call(
        _rope_table_kernel,
        out_shape=[jax.ShapeDtypeStruct((T, LANES), F32)] * 3,
        grid=(T // tm,),
        in_specs=[spec, _resident((1, LANES))],
        out_specs=[spec] * 3,
        compiler_params=_params(1),
        name="rope_tables",
    )(posf, inv_lane)


def _repack_kernel(wt_ref, fox_ref, mix_ref, *, fox_width, moves, mix_width):
    def rows_of(src, width):
        return wt_ref[src:src + width, :].T.astype(BF16)

    fox_ref[...] = rows_of(0, fox_width)
    end = 0
    for src, width, dst in moves:
        if dst > end:
            mix_ref[:, end:dst] = jnp.zeros((mix_ref.shape[0], dst - end), BF16)
        mix_ref[:, dst:dst + width] = rows_of(src, width)
        end = dst + width
    assert end == mix_width


def _repack_w_in(w_in, fox_width, moves, mix_width):
    depth, D, d_in = w_in.shape
    assert all(src % SUBLANES == 0 for src, _, _ in moves)
    tr = _tile(D, LANES)
    return pl.pallas_call(
        functools.partial(_repack_kernel, fox_width=fox_width, moves=moves, mix_width=mix_width),
        out_shape=[jax.ShapeDtypeStruct((depth, D, fox_width), BF16),
                   jax.ShapeDtypeStruct((depth, D, mix_width), BF16)],
        grid=(depth, D // tr),
        in_specs=[pl.BlockSpec((None, d_in, tr), lambda l, r: (l, 0, r))],
        out_specs=[pl.BlockSpec((None, tr, fox_width), lambda l, r: (l, r, 0)),
                   pl.BlockSpec((None, tr, mix_width), lambda l, r: (l, r, 0))],
        compiler_params=_params(2),
        name="repack_w_in",
    )(jnp.swapaxes(w_in, 1, 2))


def _fox_proj_kernel(h_ref, w_ref, wf_ref, bf_ref, q_ref, k_ref, v_ref, carry_ref, *, tiles_per_seq):
    @pl.when(pl.program_id(0) % tiles_per_seq == 0)
    def _():
        carry_ref[...] = jnp.zeros_like(carry_ref)

    h = h_ref[...]
    W = FOX_WIDTH
    q_ref[...] = (_dot(h, w_ref[:, 0:W]) * QK_SCALE).astype(BF16)
    k = _dot(h, w_ref[:, W:2 * W]).astype(BF16)
    v = _dot(h, w_ref[:, 2 * W:3 * W]).astype(BF16)

    x = jax.nn.log_sigmoid(_dot(h, wf_ref[...]) + bf_ref[...])
    tm = x.shape[0]
    row = lax.broadcasted_iota(jnp.int32, x.shape, 0)
    lane = lax.broadcasted_iota(jnp.int32, x.shape, 1)
    s = 1
    while s < tm:
        x = x + jnp.where(row >= s, pltpu.roll(x, s, 0), 0.0)
        s *= 2
    x = x + carry_ref[0:1, :]
    carry_ref[...] = jnp.broadcast_to(x[tm - 1:tm, :], carry_ref.shape)

    x = jnp.where(lane < FOX_HEADS, x, 0.0)
    hi = x.astype(BF16).astype(F32)
    mid = (x - hi).astype(BF16).astype(F32)
    lo = (x - hi - mid).astype(BF16).astype(F32)
    f_lanes = (hi + pltpu.roll(mid, FOX_HEADS, 1) + pltpu.roll(lo, 2 * FOX_HEADS, 1)).astype(BF16)
    one_lane = jnp.where(lane == 0, 1.0, 0.0).astype(BF16)
    for j in range(FOX_WIDTH // LANES):
        k_ref[:, j * SLAB:j * SLAB + LANES] = k[:, j * LANES:(j + 1) * LANES]
        k_ref[:, j * SLAB + LANES:(j + 1) * SLAB] = f_lanes
        v_ref[:, j * SLAB:j * SLAB + LANES] = v[:, j * LANES:(j + 1) * LANES]
        v_ref[:, j * SLAB + LANES:(j + 1) * SLAB] = one_lane


def _conv_branch_finish(dots, cw_ref, o_ref, prev_ref):
    xb, gate_b, gate_c = dots
    tm = xb.shape[0]
    z = gate_c * xb
    y = _causal_conv3(z, cw_ref[...], prev_ref[...])
    prev_ref[...] = z[tm - SUBLANES:, :]
    o_ref[...] = (gate_b * y).astype(BF16)


def _sgu_branch_finish(dots, ng_ref, ws_ref, bias_ref, o_ref):
    C = SGU_CHUNK
    G = SGU_GROUP_DIM
    u = jax.nn.gelu(dots[0])
    v = _rms(jax.nn.gelu(dots[1]), ng_ref[...]).astype(BF16)
    row = lax.broadcasted_iota(jnp.int32, (C, C), 0)
    col = lax.broadcasted_iota(jnp.int32, (C, C), 1)
    for g in range(SGU_GROUPS):
        w_tril = jnp.where(col <= row, ws_ref[g], 0.0).astype(BF16)
        cols = slice(g * G, (g + 1) * G)
        for c in range(u.shape[0] // C):
            rows = slice(c * C, (c + 1) * C)
            mixed = _dot(w_tril, v[rows, cols]) + bias_ref[:, cols]
            o_ref[rows, cols] = (u[rows, cols] * mixed).astype(BF16)


def _column_dots(h, w_ref, width, n):
    return [_dot(h, w_ref[:, j * width:(j + 1) * width]) for j in range(n)]


def _dil_proj_kernel(h_ref, w_ref, c_ref, s1_ref, s2_ref, *refs):
    n_groups = len(DIL_PATTERNS)
    out_refs = refs[:3 * n_groups]
    stage_ref = refs[3 * n_groups]
    h = h_ref[...]
    tm = h.shape[0]
    cos = c_ref[...]
    s1 = s1_ref[...]
    s2 = s2_ref[...]
    shift = ROPE_DIM // 2
    slot = 0
    kinds = ((0, True, QK_SCALE), (1, True, None), (2, False, None))
    wide = [_dot(h, w_ref[:, which * DIL_WIDTH:(which + 1) * DIL_WIDTH]) for which, _, _ in kinds]
    for (which, rope, scale), x_all in zip(kinds, wide):
        for g, (_, d) in enumerate(DIL_PATTERNS):
            parts = []
            for j in range(SLAB // LANES):
                xj = x_all[:, g * SLAB + j * LANES:g * SLAB + (j + 1) * LANES]
                if rope:
                    xj = xj * cos + pltpu.roll(xj, shift, 1) * s1 + pltpu.roll(xj, LANES - shift, 1) * s2
                if scale is not None:
                    xj = xj * scale
                parts.append(xj)
            out = out_refs[g * 3 + which]
            for j, xj in enumerate(parts):
                lanes = slice(j * LANES, (j + 1) * LANES)
                if d == 1:
                    out[0, 0, :, lanes] = xj.astype(BF16)
                else:
                    stage_ref[slot] = xj
                    for r in range(d):
                        out[0, r, :, lanes] = stage_ref[slot, pl.ds(r, tm // d, stride=d), :].astype(BF16)
                    slot += 1


N_FOX_PROJ_IN, N_FOX_PROJ_OUT = 3, 3


def _attn_proj_kernel(h_ref, *refs, tiles_per_seq):
    n_dil_out = 3 * len(DIL_PATTERNS)
    fox_in, refs = refs[:N_FOX_PROJ_IN], refs[N_FOX_PROJ_IN:]
    dil_in, refs = refs[:4], refs[4:]
    fox_out, refs = refs[:N_FOX_PROJ_OUT], refs[N_FOX_PROJ_OUT:]
    dil_out, (carry_ref, stage_ref) = refs[:n_dil_out], refs[n_dil_out:]
    _fox_proj_kernel(h_ref, *fox_in, *fox_out, carry_ref, tiles_per_seq=tiles_per_seq)
    _dil_proj_kernel(h_ref, *dil_in, *dil_out, stage_ref)


def _attn_proj(h, w_fox, b_f, w_mix, dil_block, tables, layer, B, S, tm):
    T, D = h.shape
    W = FOX_WIDTH
    n_pairs = FOX_WIDTH // LANES
    tiles_per_seq = S // tm
    row = lambda n: pl.BlockSpec((tm, n), lambda i: (i, 0))
    out_shape = [jax.ShapeDtypeStruct((T, W), BF16)] + [jax.ShapeDtypeStruct((T, n_pairs * SLAB), BF16)] * 2
    out_specs = [row(W), row(n_pairs * SLAB), row(n_pairs * SLAB)]
    n_strided = 0
    for _, d in DIL_PATTERNS:
        assert (tm // d) % BF16_ROWS == 0
        n_strided += 3 * (SLAB // LANES) * (d > 1)
        for _ in range(3):
            out_shape.append(jax.ShapeDtypeStruct((B, d, S // d, SLAB), BF16))
            out_specs.append(pl.BlockSpec((1, d, tm // d, SLAB),
                                          lambda i: (i // tiles_per_seq, 0, i % tiles_per_seq, 0)))
    res = pl.pallas_call(
        functools.partial(_attn_proj_kernel, tiles_per_seq=tiles_per_seq),
        out_shape=out_shape,
        grid=(T // tm,),
        in_specs=[row(D), _resident((None, D, 3 * W), (layer, 0, 0)),
                  _resident((None, D, LANES), (layer, 0, 3 * W // LANES)), _layer_weight(layer, 1, LANES),
                  _resident((None, D, 3 * DIL_WIDTH), (layer, 0, dil_block)), row(LANES), row(LANES), row(LANES)],
        out_specs=out_specs,
        scratch_shapes=[pltpu.VMEM((SUBLANES, LANES), F32), pltpu.VMEM((n_strided, tm, LANES), F32)],
        compiler_params=_params(1),
        name="attn_proj",
    )(h, w_fox, w_fox, b_f, w_mix, *tables)
    return res[:3], res[3:]


def _conv_sgu_kernel(h_ref, wc_ref, cw_ref, ws_ref, ng_ref, wsp_ref, bias_ref, ob_ref, oc_ref, prev_ref, *,
                     tiles_per_seq):
    @pl.when(pl.program_id(0) % tiles_per_seq == 0)
    def _():
        prev_ref[...] = jnp.zeros_like(prev_ref)

    h = h_ref[...]
    conv_dots = _column_dots(h, wc_ref, CONV_WIDTH, 3)
    sgu_dots = _column_dots(h, ws_ref, SGU_WIDTH, 2)
    _conv_branch_finish(conv_dots, cw_ref, ob_ref, prev_ref)
    _sgu_branch_finish(sgu_dots, ng_ref, wsp_ref, bias_ref, oc_ref)


def _conv_sgu_branches(h, w_mix, conv_block, sgu_block, conv_w, norm_g, w_s, bias_full, layer, S, tm):
    T, D = h.shape
    row = lambda n: pl.BlockSpec((tm, n), lambda i: (i, 0))
    return pl.pallas_call(
        functools.partial(_conv_sgu_kernel, tiles_per_seq=S // tm),
        out_shape=[jax.ShapeDtypeStruct((T, CONV_WIDTH), BF16), jax.ShapeDtypeStruct((T, SGU_WIDTH), BF16)],
        grid=(T // tm,),
        in_specs=[row(D), _resident((None, D, 3 * CONV_WIDTH), (layer, 0, conv_block)),
                  _layer_weight(layer, CONV_TAPS, CONV_WIDTH),
                  _resident((None, D, 2 * SGU_WIDTH), (layer, 0, sgu_block)), _layer_weight(layer, 1, SGU_WIDTH),
                  _resident((None, SGU_GROUPS, SGU_CHUNK, SGU_CHUNK), (layer, 0, 0, 0)),
                  _layer_weight(layer, SGU_CHUNK, SGU_WIDTH)],
        out_specs=[row(CONV_WIDTH), row(SGU_WIDTH)],
        scratch_shapes=[pltpu.VMEM((SUBLANES, CONV_WIDTH), F32)],
        compiler_params=_params(1),
        name="conv_sgu_branches",
    )(h, w_mix, conv_w, w_mix, norm_g, w_s, bias_full)


def _slab_head_masks(n_rows):
    lane = lax.broadcasted_iota(jnp.int32, (n_rows, SLAB), 1)
    return [(lane >= h * HEAD_DIM) & (lane < (h + 1) * HEAD_DIM) for h in range(HEADS_PER_SLAB)]


FOX_BLOCKS_PER_ITER = 4
FOX_PAIRS_PER_STEP = 2


def _fox_attn_kernel(q_ref, k_ref, v_ref, o_ref, *, tq):
    group = pl.program_id(1)
    i = pl.program_id(2)
    q0 = pl.multiple_of(i * tq, tq)
    lane = lax.broadcasted_iota(jnp.int32, (tq, LANES), 1)
    rowi = lax.broadcasted_iota(jnp.int32, (2 * tq, tq), 0) % tq
    coli = lax.broadcasted_iota(jnp.int32, (2 * tq, tq), 1)

    def stacked_queries(c):
        q = q_ref[0, :, c * LANES:(c + 1) * LANES]
        zero = jnp.zeros_like(q)
        halves = []
        for e in range(2):
            head = 2 * (FOX_PAIRS_PER_STEP * group + c) + e
            f_pick = (lane == head) | (lane == head + FOX_HEADS) | (lane == head + 2 * FOX_HEADS)
            halves.append(jnp.concatenate(
                [jnp.where(lane // HEAD_DIM == e, q, zero), jnp.where(f_pick, -1.0, 0.0).astype(BF16)], axis=1))
        return jnp.concatenate(halves, axis=0)

    qs = [stacked_queries(c) for c in range(FOX_PAIRS_PER_STEP)]

    def block(k0, carry, diagonal):
        slabs = [slice(c * SLAB, (c + 1) * SLAB) for c in range(FOX_PAIRS_PER_STEP)]
        logits = [_dot_nt(qs[c], k_ref[0, pl.ds(k0, tq), slabs[c]]) for c in range(FOX_PAIRS_PER_STEP)]
        stats = []
        for (m, _), s in zip(carry, logits):
            if diagonal:
                s = jnp.where(coli <= rowi, s, NEG)
            m_new = jnp.maximum(m, jnp.max(s, axis=-1, keepdims=True))
            stats.append((m_new, jnp.exp(m - m_new), jnp.exp(s - m_new).astype(BF16)))
        return tuple((m_new, alpha * acc + _dot(p, v_ref[0, pl.ds(k0, tq), slabs[c]]))
                     for c, ((_, acc), (m_new, alpha, p)) in enumerate(zip(carry, stats)))

    init = tuple((jnp.full((2 * tq, 1), NEG, F32), jnp.zeros((2 * tq, SLAB), F32))
                 for _ in range(FOX_PAIRS_PER_STEP))
    def run_blocks(first_block, n, carry):
        for j in range(n):
            carry = block(pl.multiple_of((first_block + j) * tq, tq), carry, False)
        return carry

    def finish(carry):
        for c, (_, acc) in enumerate(block(q0, carry, True)):
            out = acc[:, 0:LANES] / acc[:, LANES:LANES + 1]
            o_ref[0, :, c * LANES:(c + 1) * LANES] = jnp.where(lane // HEAD_DIM == 0, out[0:tq],
                                                               out[tq:2 * tq]).astype(BF16)

    U = FOX_BLOCKS_PER_ITER
    carry = lax.fori_loop(0, i // U, lambda g, c: run_blocks(g * U, U, c), init)
    for rem in range(U):
        @pl.when(i % U == rem)
        def _(rem=rem):
            finish(run_blocks(i - rem, rem, carry))


def _fox_attention(q, k_slabs, v_slabs, B, S, tq):
    n_groups = FOX_WIDTH // LANES // FOX_PAIRS_PER_STEP
    kv_spec = pl.BlockSpec((1, S, FOX_PAIRS_PER_STEP * SLAB), lambda b, g, i: (b, 0, g))
    q_spec = pl.BlockSpec((1, tq, FOX_PAIRS_PER_STEP * LANES), lambda b, g, i: (b, i, g))
    return pl.pallas_call(
        functools.partial(_fox_attn_kernel, tq=tq),
        out_shape=jax.ShapeDtypeStruct((B, S, FOX_WIDTH), BF16),
        grid=(B, n_groups, S // tq),
        in_specs=[q_spec, kv_spec, kv_spec],
        out_specs=q_spec,
        compiler_params=_params(3),
        name="fox_attention",
    )(q, k_slabs, v_slabs)


assert DIL_OUT == SLAB
DIL_BLOCKS_PER_STEP = 8


def _dil_attn_kernel(q_ref, k_ref, v_ref, o_ref, lse_ref, *, n_blocks):
    P = DIL_SPAN
    H = HEADS_PER_SLAB
    masks = _slab_head_masks(P)

    def attend(r, q_rows, k_rows, valid):
        q = q_ref[r, q_rows, :]
        zero = jnp.zeros_like(q)
        qs = jnp.concatenate([jnp.where(masks[h], q, zero) for h in range(H)], axis=0)
        s = jnp.where(valid, _dot_nt(qs, k_ref[r, k_rows, :]), NEG)
        m = jnp.max(s, axis=-1, keepdims=True)
        p = jnp.exp(s - m)
        l = jnp.sum(p, axis=-1, keepdims=True)
        pv = _dot(p.astype(BF16), v_ref[r, k_rows, :]) / l
        lse = jnp.broadcast_to(m + jnp.log(l), (H * P, SLAB))
        o, ls = pv[0:P], lse[0:P]
        for h in range(1, H):
            o = jnp.where(masks[h], pv[h * P:(h + 1) * P], o)
            ls = jnp.where(masks[h], lse[h * P:(h + 1) * P], ls)
        o_ref[r, q_rows, :] = o
        lse_ref[r, q_rows, :] = ls

    qi = lax.broadcasted_iota(jnp.int32, (H * P, P), 0) % P
    ki = lax.broadcasted_iota(jnp.int32, (H * P, P), 1)
    qi2 = lax.broadcasted_iota(jnp.int32, (H * P, 2 * P), 0) % P
    ki2 = lax.broadcasted_iota(jnp.int32, (H * P, 2 * P), 1)
    valid2 = (ki2 >= qi2) & (ki2 <= qi2 + P)

    for r in range(q_ref.shape[0]):
        attend(r, pl.ds(0, P), pl.ds(0, P), ki <= qi)

        def body(n, _, r=r):
            start = pl.multiple_of(n * P, P)
            attend(r, pl.ds(start, P), pl.ds(pl.multiple_of(start - P, P), 2 * P), valid2)
            return 0

        lax.fori_loop(1, n_blocks, body, 0, unroll=min(n_blocks - 1, DIL_BLOCKS_PER_STEP))


def _dil_attention(q, k, v, dilation):
    B, d, L, _ = q.shape
    assert d == dilation
    n_blocks = L // DIL_SPAN
    per_step = max(1, min(d, DIL_BLOCKS_PER_STEP // n_blocks))
    spec = pl.BlockSpec((None, per_step, L, SLAB), lambda b, r: (b, r, 0, 0))
    return pl.pallas_call(
        functools.partial(_dil_attn_kernel, n_blocks=n_blocks),
        out_shape=[jax.ShapeDtypeStruct((B, d, L, SLAB), F32)] * 2,
        grid=(B, d // per_step),
        in_specs=[spec] * 3,
        out_specs=[spec] * 2,
        compiler_params=_params(2),
        name=f"dil_attention_d{dilation}",
    )(q, k, v)


def _dil_merge_kernel(*refs):
    n_groups = len(DIL_PATTERNS)
    o_refs, l_refs = refs[:n_groups], refs[n_groups:2 * n_groups]
    out_ref, stage_ref = refs[2 * n_groups], refs[2 * n_groups + 1]
    tm = out_ref.shape[0]
    slot = 0
    outs, lses = [], []
    for src, dst in ((o_refs, outs), (l_refs, lses)):
        for (_, d), ref in zip(DIL_PATTERNS, src):
            if d == 1:
                dst.append(ref[0, 0])
                continue
            halves = []
            for j in range(SLAB // LANES):
                for r in range(d):
                    stage_ref[slot, pl.ds(r, tm // d, stride=d), :] = ref[0, r, :, j * LANES:(j + 1) * LANES]
                halves.append(stage_ref[slot])
                slot += 1
            dst.append(jnp.concatenate(halves, axis=1))
    m = functools.reduce(jnp.maximum, lses)
    es = [jnp.exp(l - m) for l in lses]
    num = es[0] * outs[0] + es[1] * outs[1] + es[2] * outs[2]
    out_ref[...] = (num / (es[0] + es[1] + es[2])).astype(BF16)


def _dil_merge(outs, lses, B, S, tm):
    T = B * S
    tiles_per_seq = S // tm
    specs = [pl.BlockSpec((1, d, tm // d, SLAB), lambda i: (i // tiles_per_seq, 0, i % tiles_per_seq, 0))
             for _, d in DIL_PATTERNS]
    n_strided = 2 * (SLAB // LANES) * sum(d > 1 for _, d in DIL_PATTERNS)
    return pl.pallas_call(
        _dil_merge_kernel,
        out_shape=jax.ShapeDtypeStruct((T, SLAB), BF16),
        grid=(T // tm,),
        in_specs=specs * 2,
        out_specs=pl.BlockSpec((tm, SLAB), lambda i: (i, 0)),
        scratch_shapes=[pltpu.VMEM((n_strided, tm, LANES), F32)],
        compiler_params=_params(1),
        name="dil_merge",
    )(*outs, *lses)


def _gated_merge_kernel(h_ref, oa_ref, ob_ref, oc_ref, od_ref, g0, g1, g2, g3, wa, wb, wc, wd, out_ref):
    h = h_ref[...]
    acc = None
    for o_ref, g_ref, w_ref in ((oa_ref, g0, wa), (ob_ref, g1, wb), (oc_ref, g2, wc), (od_ref, g3, wd)):
        term = jax.nn.sigmoid(_dot(h, g_ref[...])) * _dot(o_ref[...], w_ref[...])
        acc = term if acc is None else acc + term
    out_ref[...] = acc.astype(BF16)


def _gated_merge(h, branches, w_mix, gate_block, w_branches, layer, tm, tn):
    T, D = h.shape
    n_col = D // tn
    row = lambda n: pl.BlockSpec((tm, n), lambda j, i: (i, 0))
    gate_specs = [pl.BlockSpec((None, D, tn), functools.partial(lambda j, i, br: (layer, 0, gate_block + br * n_col + j), br=br))
                  for br in range(N_BRANCH)]
    w_specs = [pl.BlockSpec((None, w.shape[1], tn), lambda j, i: (layer, 0, j)) for w in w_branches]
    return pl.pallas_call(
        _gated_merge_kernel,
        out_shape=jax.ShapeDtypeStruct((T, D), BF16),
        grid=(n_col, T // tm),
        in_specs=[row(D)] + [row(o.shape[1]) for o in branches] + gate_specs + w_specs,
        out_specs=pl.BlockSpec((tm, tn), lambda j, i: (i, j)),
        compiler_params=_params(2),
        name="gated_merge",
    )(h, *branches, w_mix, w_mix, w_mix, w_mix, *w_branches)


def _residual_proj_kernel(a_ref, w_ref, x_ref, g_ref, xo_ref, h_ref):
    x = x_ref[...] + _dot(a_ref[...], w_ref[...])
    xo_ref[...] = x
    h_ref[...] = _rms(x, g_ref[...]).astype(h_ref.dtype)


def _residual_proj(a, w, x, g_next, layer, tm):
    T, D = x.shape
    K = a.shape[1]
    row = lambda n: pl.BlockSpec((tm, n), lambda i: (i, 0))
    return pl.pallas_call(
        _residual_proj_kernel,
        out_shape=[jax.ShapeDtypeStruct((T, D), F32), jax.ShapeDtypeStruct((T, D), BF16)],
        grid=(T // tm,),
        in_specs=[row(K), _layer_weight(layer, K, D), row(D), _resident((1, D))],
        out_specs=[row(D), row(D)],
        compiler_params=_params(1),
        name=f"residual_proj_k{K}",
    )(a, w, x, g_next)


FFN_CHUNK = 2 * SLAB
SHIFT_PAD = 16


def _shifted_rows(z, s_ref):
    tm = z.shape[0]
    half = tm // 2
    P = SHIFT_PAD
    s_ref[P - 2:P - 1, :] = s_ref[P + tm - 1:P + tm, :]
    s_ref[P - 4:P - 3, :] = s_ref[P + tm - 3:P + tm - 2, :]
    s_ref[pl.ds(P, half, stride=2), :] = z[:half]
    s_ref[pl.ds(P + 1, half, stride=2), :] = z[half:]
    s_ref[P - 1:P, :] = z[half - 1:half]
    s_ref[P - 3:P - 2, :] = z[half - 2:half - 1]
    z1 = jnp.concatenate([s_ref[pl.ds(P - 2, half, stride=2), :], s_ref[pl.ds(P - 1, half, stride=2), :]], axis=0)
    z2 = jnp.concatenate([s_ref[pl.ds(P - 4, half, stride=2), :], s_ref[pl.ds(P - 3, half, stride=2), :]], axis=0)
    return z1, z2


def _ffn_up_kernel(h_ref, wg_ref, wv_ref, cg_ref, cv_ref, act_ref, wg_bf, wv_bf, shift_ref, *, tiles_per_seq):
    i = pl.program_id(1)
    tm = h_ref.shape[0]

    @pl.when(i == 0)
    def _():
        wg_bf[...] = wg_ref[...].astype(BF16)
        wv_bf[...] = wv_ref[...].astype(BF16)

    @pl.when(i % tiles_per_seq == 0)
    def _():
        shift_ref[:, SHIFT_PAD + tm - SUBLANES:, :] = jnp.zeros((shift_ref.shape[0], SUBLANES, LANES), F32)

    h = h_ref[...]

    def conv(z, w_ref, col, buf):
        parts = []
        for j in range(FFN_CHUNK // LANES):
            zj = z[:, j * LANES:(j + 1) * LANES]
            z1, z2 = _shifted_rows(zj, shift_ref.at[buf + j])
            w = w_ref[:, col + j * LANES:col + (j + 1) * LANES]
            parts.append(w[0:1] * z2 + w[1:2] * z1 + w[2:3] * zj)
        return jnp.concatenate(parts, axis=1)

    lane_blocks = act_ref.shape[1] // LANES
    for c in range(act_ref.shape[1] // FFN_CHUNK):
        col = c * FFN_CHUNK
        cols = slice(col, col + FFN_CHUNK)
        up_gate = conv(_dot(h, wg_bf[:, cols]), cg_ref, col, col // LANES)
        up_val = conv(_dot(h, wv_bf[:, cols]), cv_ref, col, lane_blocks + col // LANES)
        act_ref[:, cols] = (jax.nn.silu(up_gate) * up_val).astype(BF16)


def _ffn_up(h, w_up, conv_w, layer, tm, tn, tiles_per_seq):
    T, D = h.shape
    d_ff = w_up.shape[2] // 2
    n_col = d_ff // tn
    return pl.pallas_call(
        functools.partial(_ffn_up_kernel, tiles_per_seq=tiles_per_seq),
        out_shape=jax.ShapeDtypeStruct((T, d_ff), BF16),
        grid=(n_col, T // tm),
        in_specs=[pl.BlockSpec((tm, D), lambda j, i: (i, 0)),
                  pl.BlockSpec((None, D, tn), lambda j, i: (layer, 0, j)),
                  pl.BlockSpec((None, D, tn), lambda j, i: (layer, 0, n_col + j)),
                  pl.BlockSpec((None, CONV_TAPS, tn), lambda j, i: (layer, 0, j)),
                  pl.BlockSpec((None, CONV_TAPS, tn), lambda j, i: (layer, 0, n_col + j))],
        out_specs=pl.BlockSpec((tm, tn), lambda j, i: (i, j)),
        scratch_shapes=[pltpu.VMEM((D, tn), BF16)] * 2 + [pltpu.VMEM((2 * tn // LANES, SHIFT_PAD + tm, LANES), F32)],
        compiler_params=_params(2),
        name="ffn_up",
    )(h, w_up, w_up, conv_w, conv_w)


def _ffn_down_ple_kernel(a_ref, wd_ref, x_ref, gp_ref, wg_ref, p_ref, wp_ref, gn_ref, *out_refs, final):
    x = x_ref[...] + _dot(a_ref[...], wd_ref[...])
    h = _rms(x, gp_ref[...]).astype(BF16)
    gate = jax.nn.sigmoid(_dot(h, wg_ref[...]))
    x = x + gate * _dot(p_ref[...].astype(BF16), wp_ref[...])
    if not final:
        out_refs[0][...] = x
    out_refs[-1][...] = _rms(x, gn_ref[...]).astype(out_refs[-1].dtype)


def _ffn_down_ple(act, w_down, x, g_ple, w_gate, p, w_proj, g_next, layer, tm, final):
    T, D = x.shape
    K = act.shape[1]
    P = p.shape[2]
    row = lambda n: pl.BlockSpec((tm, n), lambda i: (i, 0))
    if final:
        out_shape = [jax.ShapeDtypeStruct((T, D), F32)]
    else:
        out_shape = [jax.ShapeDtypeStruct((T, D), F32), jax.ShapeDtypeStruct((T, D), BF16)]
    return pl.pallas_call(
        functools.partial(_ffn_down_ple_kernel, final=final),
        out_shape=out_shape,
        grid=(T // tm,),
        in_specs=[row(K), _layer_weight(layer, K, D), row(D), _resident((1, D)), _layer_weight(layer, D, D),
                  pl.BlockSpec((None, tm, P), lambda i: (layer, i, 0)), _layer_weight(layer, P, D),
                  _resident((1, D))],
        out_specs=[row(D)] * len(out_shape),
        compiler_params=_params(1),
        name="ffn_down_ple_final" if final else "ffn_down_ple",
    )(act, w_down, x, g_ple, w_gate, p, w_proj, g_next)


def kernel(x, p, positions, norm_mix_g, w_in, fox_forget_b, shortconv_w, sgu_norm_g, sgu_w, sgu_b, w_br_fox, w_br_conv, w_br_sgu, w_br_dil, w_out, norm_ffn_g, w_up, ffn_conv_w, w_down, norm_ple_g, w_ple_gate, w_ple_proj, final_norm_g):
    B, S, D = x.shape
    depth = w_in.shape[0]
    T = B * S
    d_ff = w_down.shape[1]

    tm = _tile(S, 1024)
    tm_attn = _tile(S, 512)
    tm_big = _tile(S, 1024)
    tm_wide = _tile(S, 512)
    tm_down = _tile(S, 256)
    tq = _tile(S, 512)
    tn_merge = _tile(D, 512)
    tn_ffn = _tile(d_ff, 512)

    sizes = (3 * FOX_WIDTH, FOX_HEADS, 3 * CONV_WIDTH, 2 * SGU_WIDTH, 3 * DIL_WIDTH, N_BRANCH * D)
    offs = [0]
    for n in sizes:
        offs.append(offs[-1] + n)

    bf = lambda w: w.astype(BF16)
    b_f = jnp.pad(fox_forget_b.astype(F32), ((0, 0), (0, LANES - FOX_HEADS)))[:, None, :]
    moves, seg_block, pos = [], {}, 0
    for name, seg, block in (("gate", 5, tn_merge), ("sgu", 3, sizes[3]), ("conv", 2, sizes[2]), ("dil", 4, sizes[4])):
        pos += -pos % block
        seg_block[name] = pos // block
        moves.append((offs[seg], sizes[seg], pos))
        pos += sizes[seg]
    w_fox, w_mix = _repack_w_in(w_in, offs[1] + LANES, tuple(moves), pos)
    w_brs = [bf(w_br_fox), bf(w_br_conv), bf(w_br_sgu), bf(w_br_dil)]
    w_out_b, w_down_b = bf(w_out), bf(w_down)
    w_pg_b, w_pp_b = bf(w_ple_gate), bf(w_ple_proj)
    sgu_bias = jnp.repeat(jnp.swapaxes(sgu_b, 1, 2), SGU_GROUP_DIM, axis=2)
    sgu_norm = sgu_norm_g[:, None, :]
    p_tok = p.reshape(depth, T, p.shape[-1])
    row_vec = lambda g: g[None, :]

    tables = _rope_tables(positions, tm)
    xf = x.reshape(T, D)
    h = _rmsnorm(xf, row_vec(norm_mix_g[0]), tm)
    out = None
    for i in range(depth):
        (qa, ka, va), qkv = _attn_proj(h, w_fox, b_f, w_mix, seg_block["dil"], tables, i, B, S, tm_attn)
        o_a = _fox_attention(qa.reshape(B, S, -1), ka.reshape(B, S, -1), va.reshape(B, S, -1),
                             B, S, tq).reshape(T, FOX_WIDTH)
        o_b, o_c = _conv_sgu_branches(h, w_mix, seg_block["conv"], seg_block["sgu"], shortconv_w, sgu_norm, sgu_w,
                                      sgu_bias, i, S, tm)
        outs, lses = [], []
        for g, (window, dil) in enumerate(DIL_PATTERNS):
            assert window // dil == DIL_SPAN
            o_g, l_g = _dil_attention(*qkv[3 * g:3 * g + 3], dil)
            outs.append(o_g)
            lses.append(l_g)
        o_d = _dil_merge(outs, lses, B, S, tm)

        merged = _gated_merge(h, (o_a, o_b, o_c, o_d), w_mix, seg_block["gate"], w_brs, i, tm_big, tn_merge)
        xf, h = _residual_proj(merged, w_out_b, xf, row_vec(norm_ffn_g[i]), i, tm_wide)

        act = _ffn_up(h, w_up, ffn_conv_w, i, tm_big, tn_ffn, S // tm_big)
        final = i == depth - 1
        g_next = final_norm_g if final else norm_mix_g[i + 1]
        res = _ffn_down_ple(act, w_down_b, xf, row_vec(norm_ple_g[i]), w_pg_b, p_tok, w_pp_b, row_vec(g_next), i,
                            tm_down, final)
        if final:
            out = res[0]
        else:
            xf, h = res
    return out.reshape(B, S, D)
```

```python
import functools

import jax
import jax.numpy as jnp
from jax import lax
from jax.experimental import pallas as pl
from jax.experimental.pallas import tpu as pltpu

F32 = jnp.float32
BF16 = jnp.bfloat16

HEAD_DIM = 64
EPS = 1e-6
FOX_HEADS = 8
FOX_WIDTH = FOX_HEADS * HEAD_DIM
CONV_WIDTH = 512
CONV_TAPS = 3
SGU_GROUPS = 4
SGU_GROUP_DIM = 128
SGU_WIDTH = SGU_GROUPS * SGU_GROUP_DIM
SGU_CHUNK = 128
DIL_PATTERNS = ((128, 1), (512, 4), (2048, 16))
DIL_HEADS_PER_GROUP = 4
DIL_HEADS = DIL_HEADS_PER_GROUP * len(DIL_PATTERNS)
DIL_WIDTH = DIL_HEADS * HEAD_DIM
DIL_OUT = DIL_HEADS_PER_GROUP * HEAD_DIM
DIL_SPAN = 128
ROPE_THETA = 500000.0
ROPE_DIM = HEAD_DIM // 4
N_BRANCH = 4

LANES = 128
SUBLANES = 8
BF16_ROWS = 16
HEADS_PER_SLAB = 4
SLAB = HEADS_PER_SLAB * HEAD_DIM
VMEM_LIMIT = 56 * 1024 * 1024
NEG = -1e30
QK_SCALE = HEAD_DIM ** -0.5


def _params(n_axes):
    return pltpu.CompilerParams(dimension_semantics=("arbitrary",) * n_axes, vmem_limit_bytes=VMEM_LIMIT)


def _resident(shape, index=None):
    index = (0,) * len(shape) if index is None else index
    return pl.BlockSpec(shape, lambda *_: index, pipeline_mode=pl.Buffered(1))


def _layer_weight(layer, rows, cols):
    return _resident((None, rows, cols), (layer, 0, 0))


def _tile(n, pref):
    t = min(n, pref)
    assert n % t == 0, (n, t)
    return t


def _rms(x, g):
    var = jnp.mean(x * x, axis=-1, keepdims=True)
    return x * lax.rsqrt(var + EPS) * g


def _dot(a, b):
    return jnp.dot(a, b, preferred_element_type=F32)


def _dot_nt(a, b):
    return lax.dot_general(a, b, (((1,), (1,)), ((), ())), preferred_element_type=F32)


def _causal_conv3(z, w, prev):
    y = w[0:1] * pltpu.roll(z, 2, 0) + w[1:2] * pltpu.roll(z, 1, 0) + w[2:3] * z
    zh = z[0:BF16_ROWS]
    row = lax.broadcasted_iota(jnp.int32, zh.shape, 0)
    p1 = prev[SUBLANES - 1:SUBLANES]
    p2 = prev[SUBLANES - 2:SUBLANES - 1]
    z1 = jnp.where(row == 0, p1, pltpu.roll(zh, 1, 0))
    z2 = jnp.where(row == 0, p2, jnp.where(row == 1, p1, pltpu.roll(zh, 2, 0)))
    yh = w[0:1] * z2 + w[1:2] * z1 + w[2:3] * zh
    return jnp.concatenate([yh, y[BF16_ROWS:]], axis=0)


def _rmsnorm_kernel(x_ref, g_ref, h_ref):
    h_ref[...] = _rms(x_ref[...], g_ref[...]).astype(h_ref.dtype)


def _rmsnorm(x, g, tm):
    T, D = x.shape
    return pl.pallas_call(
        _rmsnorm_kernel,
        out_shape=jax.ShapeDtypeStruct((T, D), BF16),
        grid=(T // tm,),
        in_specs=[pl.BlockSpec((tm, D), lambda i: (i, 0)), _resident((1, D))],
        out_specs=pl.BlockSpec((tm, D), lambda i: (i, 0)),
        compiler_params=_params(1),
        name="rmsnorm",
    )(x, g)


def _rope_table_kernel(pos_ref, inv_ref, c_ref, s1_ref, s2_ref):
    ang = pos_ref[...] * inv_ref[...]
    cos = jnp.cos(ang)
    sin = jnp.sin(ang)
    lane = lax.broadcasted_iota(jnp.int32, ang.shape, 1) % HEAD_DIM
    half = ROPE_DIM // 2
    c_ref[...] = jnp.where(lane < ROPE_DIM, cos, 1.0)
    s1_ref[...] = jnp.where((lane >= half) & (lane < ROPE_DIM), sin, 0.0)
    s2_ref[...] = jnp.where(lane < half, -sin, 0.0)


def _rope_tables(positions, tm):
    T = positions.size
    half = ROPE_DIM // 2
    inv = ROPE_THETA ** (-jnp.arange(half, dtype=F32) * (2.0 / ROPE_DIM))
    lane = jnp.arange(LANES) % HEAD_DIM
    inv_lane = jnp.where(lane < ROPE_DIM, inv[lane % half], 0.0).astype(F32)[None, :]
    posf = jnp.broadcast_to(positions.reshape(T, 1).astype(F32), (T, LANES))
    spec = pl.BlockSpec((tm, LANES), lambda i: (i, 0))
    return pl.pallas_call(
        _rope_table_kernel,
        out_shape=[jax.ShapeDtypeStruct((T, LANES), F32)] * 3,
        grid=(T // tm,),
        in_specs=[spec, _resident((1, LANES))],
        out_specs=[spec] * 3,
        compiler_params=_params(1),
        name="rope_tables",
    )(posf, inv_lane)


def _repack_kernel(wt_ref, fox_ref, mix_ref, *, fox_width, moves, mix_width):
    def rows_of(src, width):
        return wt_ref[src:src + width, :].T.astype(BF16)

    fox_ref[...] = rows_of(0, fox_width)
    end = 0
    for src, width, dst in moves:
        if dst > end:
            mix_ref[:, end:dst] = jnp.zeros((mix_ref.shape[0], dst - end), BF16)
        mix_ref[:, dst:dst + width] = rows_of(src, width)
        end = dst + width
    assert end == mix_width


def _repack_w_in(w_in, fox_width, moves, mix_width):
    depth, D, d_in = w_in.shape
    assert all(src % SUBLANES == 0 for src, _, _ in moves)
    tr = _tile(D, LANES)
    return pl.pallas_call(
        functools.partial(_repack_kernel, fox_width=fox_width, moves=moves, mix_width=mix_width),
        out_shape=[jax.ShapeDtypeStruct((depth, D, fox_width), BF16),
                   jax.ShapeDtypeStruct((depth, D, mix_width), BF16)],
        grid=(depth, D // tr),
        in_specs=[pl.BlockSpec((None, d_in, tr), lambda l, r: (l, 0, r))],
        out_specs=[pl.BlockSpec((None, tr, fox_width), lambda l, r: (l, r, 0)),
                   pl.BlockSpec((None, tr, mix_width), lambda l, r: (l, r, 0))],
        compiler_params=_params(2),
        name="repack_w_in",
    )(jnp.swapaxes(w_in, 1, 2))


def _fox_proj_kernel(h_ref, w_ref, wf_ref, bf_ref, q_ref, k_ref, v_ref, carry_ref, *, tiles_per_seq):
    @pl.when(pl.program_id(0) % tiles_per_seq == 0)
    def _():
        carry_ref[...] = jnp.zeros_like(carry_ref)

    h = h_ref[...]
    W = FOX_WIDTH
    q_ref[...] = (_dot(h, w_ref[:, 0:W]) * QK_SCALE).astype(BF16)
    k = _dot(h, w_ref[:, W:2 * W]).astype(BF16)
    v = _dot(h, w_ref[:, 2 * W:3 * W]).astype(BF16)

    x = jax.nn.log_sigmoid(_dot(h, wf_ref[...]) + bf_ref[...])
    tm = x.shape[0]
    row = lax.broadcasted_iota(jnp.int32, x.shape, 0)
    lane = lax.broadcasted_iota(jnp.int32, x.shape, 1)
    s = 1
    while s < tm:
        x = x + jnp.where(row >= s, pltpu.roll(x, s, 0), 0.0)
        s *= 2
    x = x + carry_ref[0:1, :]
    carry_ref[...] = jnp.broadcast_to(x[tm - 1:tm, :], carry_ref.shape)

    x = jnp.where(lane < FOX_HEADS, x, 0.0)
    hi = x.astype(BF16).astype(F32)
    mid = (x - hi).astype(BF16).astype(F32)
    lo = (x - hi - mid).astype(BF16).astype(F32)
    f_lanes = (hi + pltpu.roll(mid, FOX_HEADS, 1) + pltpu.roll(lo, 2 * FOX_HEADS, 1)).astype(BF16)
    one_lane = jnp.where(lane == 0, 1.0, 0.0).astype(BF16)
    for j in range(FOX_WIDTH // LANES):
        k_ref[:, j * SLAB:j * SLAB + LANES] = k[:, j * LANES:(j + 1) * LANES]
        k_ref[:, j * SLAB + LANES:(j + 1) * SLAB] = f_lanes
        v_ref[:, j * SLAB:j * SLAB + LANES] = v[:, j * LANES:(j + 1) * LANES]
        v_ref[:, j * SLAB + LANES:(j + 1) * SLAB] = one_lane


def _conv_branch_finish(dots, cw_ref, o_ref, prev_ref):
    xb, gate_b, gate_c = dots
    tm = xb.shape[0]
    z = gate_c * xb
    y = _causal_conv3(z, cw_ref[...], prev_ref[...])
    prev_ref[...] = z[tm - SUBLANES:, :]
    o_ref[...] = (gate_b * y).astype(BF16)


def _sgu_branch_finish(dots, ng_ref, ws_ref, bias_ref, o_ref):
    C = SGU_CHUNK
    G = SGU_GROUP_DIM
    u = jax.nn.gelu(dots[0])
    v = _rms(jax.nn.gelu(dots[1]), ng_ref[...]).astype(BF16)
    row = lax.broadcasted_iota(jnp.int32, (C, C), 0)
    col = lax.broadcasted_iota(jnp.int32, (C, C), 1)
    for g in range(SGU_GROUPS):
        w_tril = jnp.where(col <= row, ws_ref[g], 0.0).astype(BF16)
        cols = slice(g * G, (g + 1) * G)
        for c in range(u.shape[0] // C):
            rows = slice(c * C, (c + 1) * C)
            mixed = _dot(w_tril, v[rows, cols]) + bias_ref[:, cols]
            o_ref[rows, cols] = (u[rows, cols] * mixed).astype(BF16)


def _column_dots(h, w_ref, width, n):
    return [_dot(h, w_ref[:, j * width:(j + 1) * width]) for j in range(n)]


def _dil_proj_kernel(h_ref, w_ref, c_ref, s1_ref, s2_ref, *refs):
    n_groups = len(DIL_PATTERNS)
    out_refs = refs[:3 * n_groups]
    stage_ref = refs[3 * n_groups]
    h = h_ref[...]
    tm = h.shape[0]
    cos = c_ref[...]
    s1 = s1_ref[...]
    s2 = s2_ref[...]
    shift = ROPE_DIM // 2
    slot = 0
    kinds = ((0, True, QK_SCALE), (1, True, None), (2, False, None))
    wide = [_dot(h, w_ref[:, which * DIL_WIDTH:(which + 1) * DIL_WIDTH]) for which, _, _ in kinds]
    for (which, rope, scale), x_all in zip(kinds, wide):
        for g, (_, d) in enumerate(DIL_PATTERNS):
            parts = []
            for j in range(SLAB // LANES):
                xj = x_all[:, g * SLAB + j * LANES:g * SLAB + (j + 1) * LANES]
                if rope:
                    xj = xj * cos + pltpu.roll(xj, shift, 1) * s1 + pltpu.roll(xj, LANES - shift, 1) * s2
                if scale is not None:
                    xj = xj * scale
                parts.append(xj)
            out = out_refs[g * 3 + which]
            for j, xj in enumerate(parts):
                lanes = slice(j * LANES, (j + 1) * LANES)
                if d == 1:
                    out[0, 0, :, lanes] = xj.astype(BF16)
                else:
                    stage_ref[slot] = xj
                    for r in range(d):
                        out[0, r, :, lanes] = stage_ref[slot, pl.ds(r, tm // d, stride=d), :].astype(BF16)
                    slot += 1


N_FOX_PROJ_IN, N_FOX_PROJ_OUT = 3, 3


def _attn_proj_kernel(h_ref, *refs, tiles_per_seq):
    n_dil_out = 3 * len(DIL_PATTERNS)
    fox_in, refs = refs[:N_FOX_PROJ_IN], refs[N_FOX_PROJ_IN:]
    dil_in, refs = refs[:4], refs[4:]
    fox_out, refs = refs[:N_FOX_PROJ_OUT], refs[N_FOX_PROJ_OUT:]
    dil_out, (carry_ref, stage_ref) = refs[:n_dil_out], refs[n_dil_out:]
    _fox_proj_kernel(h_ref, *fox_in, *fox_out, carry_ref, tiles_per_seq=tiles_per_seq)
    _dil_proj_kernel(h_ref, *dil_in, *dil_out, stage_ref)


def _attn_proj(h, w_fox, b_f, w_mix, dil_block, tables, layer, B, S, tm):
    T, D = h.shape
    W = FOX_WIDTH
    n_pairs = FOX_WIDTH // LANES
    tiles_per_seq = S // tm
    row = lambda n: pl.BlockSpec((tm, n), lambda i: (i, 0))
    out_shape = [jax.ShapeDtypeStruct((T, W), BF16)] + [jax.ShapeDtypeStruct((T, n_pairs * SLAB), BF16)] * 2
    out_specs = [row(W), row(n_pairs * SLAB), row(n_pairs * SLAB)]
    n_strided = 0
    for _, d in DIL_PATTERNS:
        assert (tm // d) % BF16_ROWS == 0
        n_strided += 3 * (SLAB // LANES) * (d > 1)
        for _ in range(3):
            out_shape.append(jax.ShapeDtypeStruct((B, d, S // d, SLAB), BF16))
            out_specs.append(pl.BlockSpec((1, d, tm // d, SLAB),
                                          lambda i: (i // tiles_per_seq, 0, i % tiles_per_seq, 0)))
    res = pl.pallas_call(
        functools.partial(_attn_proj_kernel, tiles_per_seq=tiles_per_seq),
        out_shape=out_shape,
        grid=(T // tm,),
        in_specs=[row(D), _resident((None, D, 3 * W), (layer, 0, 0)),
                  _resident((None, D, LANES), (layer, 0, 3 * W // LANES)), _layer_weight(layer, 1, LANES),
                  _resident((None, D, 3 * DIL_WIDTH), (layer, 0, dil_block)), row(LANES), row(LANES), row(LANES)],
        out_specs=out_specs,
        scratch_shapes=[pltpu.VMEM((SUBLANES, LANES), F32), pltpu.VMEM((n_strided, tm, LANES), F32)],
        compiler_params=_params(1),
        name="attn_proj",
    )(h, w_fox, w_fox, b_f, w_mix, *tables)
    return res[:3], res[3:]


def _conv_sgu_kernel(h_ref, wc_ref, cw_ref, ws_ref, ng_ref, wsp_ref, bias_ref, ob_ref, oc_ref, prev_ref, *,
                     tiles_per_seq):
    @pl.when(pl.program_id(0) % tiles_per_seq == 0)
    def _():
        prev_ref[...] = jnp.zeros_like(prev_ref)

    h = h_ref[...]
    conv_dots = _column_dots(h, wc_ref, CONV_WIDTH, 3)
    sgu_dots = _column_dots(h, ws_ref, SGU_WIDTH, 2)
    _conv_branch_finish(conv_dots, cw_ref, ob_ref, prev_ref)
    _sgu_branch_finish(sgu_dots, ng_ref, wsp_ref, bias_ref, oc_ref)


def _conv_sgu_branches(h, w_mix, conv_block, sgu_block, conv_w, norm_g, w_s, bias_full, layer, S, tm):
    T, D = h.shape
    row = lambda n: pl.BlockSpec((tm, n), lambda i: (i, 0))
    return pl.pallas_call(
        functools.partial(_conv_sgu_kernel, tiles_per_seq=S // tm),
        out_shape=[jax.ShapeDtypeStruct((T, CONV_WIDTH), BF16), jax.ShapeDtypeStruct((T, SGU_WIDTH), BF16)],
        grid=(T // tm,),
        in_specs=[row(D), _resident((None, D, 3 * CONV_WIDTH), (layer, 0, conv_block)),
                  _layer_weight(layer, CONV_TAPS, CONV_WIDTH),
                  _resident((None, D, 2 * SGU_WIDTH), (layer, 0, sgu_block)), _layer_weight(layer, 1, SGU_WIDTH),
                  _resident((None, SGU_GROUPS, SGU_CHUNK, SGU_CHUNK), (layer, 0, 0, 0)),
                  _layer_weight(layer, SGU_CHUNK, SGU_WIDTH)],
        out_specs=[row(CONV_WIDTH), row(SGU_WIDTH)],
        scratch_shapes=[pltpu.VMEM((SUBLANES, CONV_WIDTH), F32)],
        compiler_params=_params(1),
        name="conv_sgu_branches",
    )(h, w_mix, conv_w, w_mix, norm_g, w_s, bias_full)


def _slab_head_masks(n_rows):
    lane = lax.broadcasted_iota(jnp.int32, (n_rows, SLAB), 1)
    return [(lane >= h * HEAD_DIM) & (lane < (h + 1) * HEAD_DIM) for h in range(HEADS_PER_SLAB)]


FOX_BLOCKS_PER_ITER = 4
FOX_PAIRS_PER_STEP = 2


def _fox_attn_kernel(q_ref, k_ref, v_ref, o_ref, *, tq):
    group = pl.program_id(1)
    i = pl.program_id(2)
    q0 = pl.multiple_of(i * tq, tq)
    lane = lax.broadcasted_iota(jnp.int32, (tq, LANES), 1)
    rowi = lax.broadcasted_iota(jnp.int32, (2 * tq, tq), 0) % tq
    coli = lax.broadcasted_iota(jnp.int32, (2 * tq, tq), 1)

    def stacked_queries(c):
        q = q_ref[0, :, c * LANES:(c + 1) * LANES]
        zero = jnp.zeros_like(q)
        halves = []
        for e in range(2):
            head = 2 * (FOX_PAIRS_PER_STEP * group + c) + e
            f_pick = (lane == head) | (lane == head + FOX_HEADS) | (lane == head + 2 * FOX_HEADS)
            halves.append(jnp.concatenate(
                [jnp.where(lane // HEAD_DIM == e, q, zero), jnp.where(f_pick, -1.0, 0.0).astype(BF16)], axis=1))
        return jnp.concatenate(halves, axis=0)

    qs = [stacked_queries(c) for c in range(FOX_PAIRS_PER_STEP)]

    def block(k0, carry, diagonal):
        slabs = [slice(c * SLAB, (c + 1) * SLAB) for c in range(FOX_PAIRS_PER_STEP)]
        logits = [_dot_nt(qs[c], k_ref[0, pl.ds(k0, tq), slabs[c]]) for c in range(FOX_PAIRS_PER_STEP)]
        stats = []
        for (m, _), s in zip(carry, logits):
            if diagonal:
                s = jnp.where(coli <= rowi, s, NEG)
            m_new = jnp.maximum(m, jnp.max(s, axis=-1, keepdims=True))
            stats.append((m_new, jnp.exp(m - m_new), jnp.exp(s - m_new).astype(BF16)))
        return tuple((m_new, alpha * acc + _dot(p, v_ref[0, pl.ds(k0, tq), slabs[c]]))
                     for c, ((_, acc), (m_new, alpha, p)) in enumerate(zip(carry, stats)))

    init = tuple((jnp.full((2 * tq, 1), NEG, F32), jnp.zeros((2 * tq, SLAB), F32))
                 for _ in range(FOX_PAIRS_PER_STEP))
    def run_blocks(first_block, n, carry):
        for j in range(n):
            carry = block(pl.multiple_of((first_block + j) * tq, tq), carry, False)
        return carry

    def finish(carry):
        for c, (_, acc) in enumerate(block(q0, carry, True)):
            out = acc[:, 0:LANES] / acc[:, LANES:LANES + 1]
            o_ref[0, :, c * LANES:(c + 1) * LANES] = jnp.where(lane // HEAD_DIM == 0, out[0:tq],
                                                               out[tq:2 * tq]).astype(BF16)

    U = FOX_BLOCKS_PER_ITER
    carry = lax.fori_loop(0, i // U, lambda g, c: run_blocks(g * U, U, c), init)
    for rem in range(U):
        @pl.when(i % U == rem)
        def _(rem=rem):
            finish(run_blocks(i - rem, rem, carry))


def _fox_attention(q, k_slabs, v_slabs, B, S, tq):
    n_groups = FOX_WIDTH // LANES // FOX_PAIRS_PER_STEP
    kv_spec = pl.BlockSpec((1, S, FOX_PAIRS_PER_STEP * SLAB), lambda b, g, i: (b, 0, g))
    q_spec = pl.BlockSpec((1, tq, FOX_PAIRS_PER_STEP * LANES), lambda b, g, i: (b, i, g))
    return pl.pallas_call(
        functools.partial(_fox_attn_kernel, tq=tq),
        out_shape=jax.ShapeDtypeStruct((B, S, FOX_WIDTH), BF16),
        grid=(B, n_groups, S // tq),
        in_specs=[q_spec, kv_spec, kv_spec],
        out_specs=q_spec,
        compiler_params=_params(3),
        name="fox_attention",
    )(q, k_slabs, v_slabs)


assert DIL_OUT == SLAB
DIL_BLOCKS_PER_STEP = 16


def _dil_attn_kernel(q_ref, k_ref, v_ref, o_ref, lse_ref, *, n_blocks):
    P = DIL_SPAN
    H = HEADS_PER_SLAB
    masks = _slab_head_masks(P)

    def attend(r, q_rows, k_rows, valid):
        q = q_ref[r, q_rows, :]
        zero = jnp.zeros_like(q)
        qs = jnp.concatenate([jnp.where(masks[h], q, zero) for h in range(H)], axis=0)
        s = jnp.where(valid, _dot_nt(qs, k_ref[r, k_rows, :]), NEG)
        m = jnp.max(s, axis=-1, keepdims=True)
        p = jnp.exp(s - m)
        l = jnp.sum(p, axis=-1, keepdims=True)
        pv = _dot(p.astype(BF16), v_ref[r, k_rows, :]) / l
        lse = jnp.broadcast_to(m + jnp.log(l), (H * P, SLAB))
        o, ls = pv[0:P], lse[0:P]
        for h in range(1, H):
            o = jnp.where(masks[h], pv[h * P:(h + 1) * P], o)
            ls = jnp.where(masks[h], lse[h * P:(h + 1) * P], ls)
        o_ref[r, q_rows, :] = o
        lse_ref[r, q_rows, :] = ls

    qi = lax.broadcasted_iota(jnp.int32, (H * P, P), 0) % P
    ki = lax.broadcasted_iota(jnp.int32, (H * P, P), 1)
    qi2 = lax.broadcasted_iota(jnp.int32, (H * P, 2 * P), 0) % P
    ki2 = lax.broadcasted_iota(jnp.int32, (H * P, 2 * P), 1)
    valid2 = (ki2 >= qi2) & (ki2 <= qi2 + P)

    for r in range(q_ref.shape[0]):
        attend(r, pl.ds(0, P), pl.ds(0, P), ki <= qi)

        def body(n, _, r=r):
            start = pl.multiple_of(n * P, P)
            attend(r, pl.ds(start, P), pl.ds(pl.multiple_of(start - P, P), 2 * P), valid2)
            return 0

        lax.fori_loop(1, n_blocks, body, 0, unroll=min(n_blocks - 1, DIL_BLOCKS_PER_STEP))


def _dil_attention(q, k, v, dilation):
    B, d, L, _ = q.shape
    assert d == dilation
    n_blocks = L // DIL_SPAN
    per_step = max(1, min(d, DIL_BLOCKS_PER_STEP // n_blocks))
    spec = pl.BlockSpec((None, per_step, L, SLAB), lambda b, r: (b, r, 0, 0))
    return pl.pallas_call(
        functools.partial(_dil_attn_kernel, n_blocks=n_blocks),
        out_shape=[jax.ShapeDtypeStruct((B, d, L, SLAB), F32)] * 2,
        grid=(B, d // per_step),
        in_specs=[spec] * 3,
        out_specs=[spec] * 2,
        compiler_params=_params(2),
        name=f"dil_attention_d{dilation}",
    )(q, k, v)


def _dil_merge_kernel(*refs):
    n_groups = len(DIL_PATTERNS)
    o_refs, l_refs = refs[:n_groups], refs[n_groups:2 * n_groups]
    out_ref, stage_ref = refs[2 * n_groups], refs[2 * n_groups + 1]
    tm = out_ref.shape[0]
    slot = 0
    outs, lses = [], []
    for src, dst in ((o_refs, outs), (l_refs, lses)):
        for (_, d), ref in zip(DIL_PATTERNS, src):
            if d == 1:
                dst.append(ref[0, 0])
                continue
            halves = []
            for j in range(SLAB // LANES):
                for r in range(d):
                    stage_ref[slot, pl.ds(r, tm // d, stride=d), :] = ref[0, r, :, j * LANES:(j + 1) * LANES]
                halves.append(stage_ref[slot])
                slot += 1
            dst.append(jnp.concatenate(halves, axis=1))
    m = functools.reduce(jnp.maximum, lses)
    es = [jnp.exp(l - m) for l in lses]
    num = es[0] * outs[0] + es[1] * outs[1] + es[2] * outs[2]
    out_ref[...] = (num / (es[0] + es[1] + es[2])).astype(BF16)


def _dil_merge(outs, lses, B, S, tm):
    T = B * S
    tiles_per_seq = S // tm
    specs = [pl.BlockSpec((1, d, tm // d, SLAB), lambda i: (i // tiles_per_seq, 0, i % tiles_per_seq, 0))
             for _, d in DIL_PATTERNS]
    n_strided = 2 * (SLAB // LANES) * sum(d > 1 for _, d in DIL_PATTERNS)
    return pl.pallas_call(
        _dil_merge_kernel,
        out_shape=jax.ShapeDtypeStruct((T, SLAB), BF16),
        grid=(T // tm,),
        in_specs=specs * 2,
        out_specs=pl.BlockSpec((tm, SLAB), lambda i: (i, 0)),
        scratch_shapes=[pltpu.VMEM((n_strided, tm, LANES), F32)],
        compiler_params=_params(1),
        name="dil_merge",
    )(*outs, *lses)


def _gated_merge_kernel(h_ref, oa_ref, ob_ref, oc_ref, od_ref, g0, g1, g2, g3, wa, wb, wc, wd, out_ref):
    h = h_ref[...]
    acc = None
    for o_ref, g_ref, w_ref in ((oa_ref, g0, wa), (ob_ref, g1, wb), (oc_ref, g2, wc), (od_ref, g3, wd)):
        term = jax.nn.sigmoid(_dot(h, g_ref[...])) * _dot(o_ref[...], w_ref[...])
        acc = term if acc is None else acc + term
    out_ref[...] = acc.astype(BF16)


def _gated_merge(h, branches, w_mix, gate_block, w_branches, layer, tm, tn):
    T, D = h.shape
    n_col = D // tn
    row = lambda n: pl.BlockSpec((tm, n), lambda j, i: (i, 0))
    gate_specs = [pl.BlockSpec((None, D, tn), functools.partial(lambda j, i, br: (layer, 0, gate_block + br * n_col + j), br=br))
                  for br in range(N_BRANCH)]
    w_specs = [pl.BlockSpec((None, w.shape[1], tn), lambda j, i: (layer, 0, j)) for w in w_branches]
    return pl.pallas_call(
        _gated_merge_kernel,
        out_shape=jax.ShapeDtypeStruct((T, D), BF16),
        grid=(n_col, T // tm),
        in_specs=[row(D)] + [row(o.shape[1]) for o in branches] + gate_specs + w_specs,
        out_specs=pl.BlockSpec((tm, tn), lambda j, i: (i, j)),
        compiler_params=_params(2),
        name="gated_merge",
    )(h, *branches, w_mix, w_mix, w_mix, w_mix, *w_branches)


def _residual_proj_kernel(a_ref, w_ref, x_ref, g_ref, xo_ref, h_ref):
    x = x_ref[...] + _dot(a_ref[...], w_ref[...])
    xo_ref[...] = x
    h_ref[...] = _rms(x, g_ref[...]).astype(h_ref.dtype)


def _residual_proj(a, w, x, g_next, layer, tm):
    T, D = x.shape
    K = a.shape[1]
    row = lambda n: pl.BlockSpec((tm, n), lambda i: (i, 0))
    return pl.pallas_call(
        _residual_proj_kernel,
        out_shape=[jax.ShapeDtypeStruct((T, D), F32), jax.ShapeDtypeStruct((T, D), BF16)],
        grid=(T // tm,),
        in_specs=[row(K), _layer_weight(layer, K, D), row(D), _resident((1, D))],
        out_specs=[row(D), row(D)],
        compiler_params=_params(1),
        name=f"residual_proj_k{K}",
    )(a, w, x, g_next)


FFN_CHUNK = 2 * SLAB
SHIFT_PAD = 16


def _shifted_rows(z, s_ref):
    tm = z.shape[0]
    half = tm // 2
    P = SHIFT_PAD
    s_ref[P - 2:P - 1, :] = s_ref[P + tm - 1:P + tm, :]
    s_ref[P - 4:P - 3, :] = s_ref[P + tm - 3:P + tm - 2, :]
    s_ref[pl.ds(P, half, stride=2), :] = z[:half]
    s_ref[pl.ds(P + 1, half, stride=2), :] = z[half:]
    s_ref[P - 1:P, :] = z[half - 1:half]
    s_ref[P - 3:P - 2, :] = z[half - 2:half - 1]
    z1 = jnp.concatenate([s_ref[pl.ds(P - 2, half, stride=2), :], s_ref[pl.ds(P - 1, half, stride=2), :]], axis=0)
    z2 = jnp.concatenate([s_ref[pl.ds(P - 4, half, stride=2), :], s_ref[pl.ds(P - 3, half, stride=2), :]], axis=0)
    return z1, z2


def _ffn_up_kernel(h_ref, wg_ref, wv_ref, cg_ref, cv_ref, act_ref, wg_bf, wv_bf, shift_ref, *, tiles_per_seq):
    i = pl.program_id(1)
    tm = h_ref.shape[0]

    @pl.when(i == 0)
    def _():
        wg_bf[...] = wg_ref[...].astype(BF16)
        wv_bf[...] = wv_ref[...].astype(BF16)

    @pl.when(i % tiles_per_seq == 0)
    def _():
        shift_ref[:, SHIFT_PAD + tm - SUBLANES:, :] = jnp.zeros((shift_ref.shape[0], SUBLANES, LANES), F32)

    h = h_ref[...]

    def conv(z, w_ref, col, buf):
        parts = []
        for j in range(FFN_CHUNK // LANES):
            zj = z[:, j * LANES:(j + 1) * LANES]
            z1, z2 = _shifted_rows(zj, shift_ref.at[buf + j])
            w = w_ref[:, col + j * LANES:col + (j + 1) * LANES]
            parts.append(w[0:1] * z2 + w[1:2] * z1 + w[2:3] * zj)
        return jnp.concatenate(parts, axis=1)

    lane_blocks = act_ref.shape[1] // LANES
    for c in range(act_ref.shape[1] // FFN_CHUNK):
        col = c * FFN_CHUNK
        cols = slice(col, col + FFN_CHUNK)
        up_gate = conv(_dot(h, wg_bf[:, cols]), cg_ref, col, col // LANES)
        up_val = conv(_dot(h, wv_bf[:, cols]), cv_ref, col, lane_blocks + col // LANES)
        act_ref[:, cols] = (jax.nn.silu(up_gate) * up_val).astype(BF16)


def _ffn_up(h, w_up, conv_w, layer, tm, tn, tiles_per_seq):
    T, D = h.shape
    d_ff = w_up.shape[2] // 2
    n_col = d_ff // tn
    return pl.pallas_call(
        functools.partial(_ffn_up_kernel, tiles_per_seq=tiles_per_seq),
        out_shape=jax.ShapeDtypeStruct((T, d_ff), BF16),
        grid=(n_col, T // tm),
        in_specs=[pl.BlockSpec((tm, D), lambda j, i: (i, 0)),
                  pl.BlockSpec((None, D, tn), lambda j, i: (layer, 0, j)),
                  pl.BlockSpec((None, D, tn), lambda j, i: (layer, 0, n_col + j)),
                  pl.BlockSpec((None, CONV_TAPS, tn), lambda j, i: (layer, 0, j)),
                  pl.BlockSpec((None, CONV_TAPS, tn), lambda j, i: (layer, 0, n_col + j))],
        out_specs=pl.BlockSpec((tm, tn), lambda j, i: (i, j)),
        scratch_shapes=[pltpu.VMEM((D, tn), BF16)] * 2 + [pltpu.VMEM((2 * tn // LANES, SHIFT_PAD + tm, LANES), F32)],
        compiler_params=_params(2),
        name="ffn_up",
    )(h, w_up, w_up, conv_w, conv_w)


def _ffn_down_ple_kernel(a_ref, wd_ref, x_ref, gp_ref, wg_ref, p_ref, wp_ref, gn_ref, *out_refs, final):
    down = _dot(a_ref[...], wd_ref[...])
    embed = _dot(p_ref[...].astype(BF16), wp_ref[...])
    x = x_ref[...] + down
    h = _rms(x, gp_ref[...]).astype(BF16)
    gate = jax.nn.sigmoid(_dot(h, wg_ref[...]))
    x = x + gate * embed
    if not final:
        out_refs[0][...] = x
    out_refs[-1][...] = _rms(x, gn_ref[...]).astype(out_refs[-1].dtype)


def _ffn_down_ple(act, w_down, x, g_ple, w_gate, p, w_proj, g_next, layer, tm, final):
    T, D = x.shape
    K = act.shape[1]
    P = p.shape[2]
    row = lambda n: pl.BlockSpec((tm, n), lambda i: (i, 0))
    if final:
        out_shape = [jax.ShapeDtypeStruct((T, D), F32)]
    else:
        out_shape = [jax.ShapeDtypeStruct((T, D), F32), jax.ShapeDtypeStruct((T, D), BF16)]
    return pl.pallas_call(
        functools.partial(_ffn_down_ple_kernel, final=final),
        out_shape=out_shape,
        grid=(T // tm,),
        in_specs=[row(K), _layer_weight(layer, K, D), row(D), _resident((1, D)), _layer_weight(layer, D, D),
                  pl.BlockSpec((None, tm, P), lambda i: (layer, i, 0)), _layer_weight(layer, P, D),
                  _resident((1, D))],
        out_specs=[row(D)] * len(out_shape),
        compiler_params=_params(1),
        name="ffn_down_ple_final" if final else "ffn_down_ple",
    )(act, w_down, x, g_ple, w_gate, p, w_proj, g_next)


def kernel(x, p, positions, norm_mix_g, w_in, fox_forget_b, shortconv_w, sgu_norm_g, sgu_w, sgu_b, w_br_fox, w_br_conv, w_br_sgu, w_br_dil, w_out, norm_ffn_g, w_up, ffn_conv_w, w_down, norm_ple_g, w_ple_gate, w_ple_proj, final_norm_g):
    B, S, D = x.shape
    depth = w_in.shape[0]
    T = B * S
    d_ff = w_down.shape[1]

    tm = _tile(S, 1024)
    tm_attn = _tile(S, 512)
    tm_big = _tile(S, 1024)
    tm_wide = _tile(S, 512)
    tm_down = _tile(S, 256)
    tq = _tile(S, 512)
    tn_merge = _tile(D, 512)
    tn_ffn = _tile(d_ff, 512)

    sizes = (3 * FOX_WIDTH, FOX_HEADS, 3 * CONV_WIDTH, 2 * SGU_WIDTH, 3 * DIL_WIDTH, N_BRANCH * D)
    offs = [0]
    for n in sizes:
        offs.append(offs[-1] + n)

    bf = lambda w: w.astype(BF16)
    b_f = jnp.pad(fox_forget_b.astype(F32), ((0, 0), (0, LANES - FOX_HEADS)))[:, None, :]
    moves, seg_block, pos = [], {}, 0
    for name, seg, block in (("gate", 5, tn_merge), ("sgu", 3, sizes[3]), ("conv", 2, sizes[2]), ("dil", 4, sizes[4])):
        pos += -pos % block
        seg_block[name] = pos // block
        moves.append((offs[seg], sizes[seg], pos))
        pos += sizes[seg]
    w_fox, w_mix = _repack_w_in(w_in, offs[1] + LANES, tuple(moves), pos)
    w_brs = [bf(w_br_fox), bf(w_br_conv), bf(w_br_sgu), bf(w_br_dil)]
    w_out_b, w_down_b = bf(w_out), bf(w_down)
    w_pg_b, w_pp_b = bf(w_ple_gate), bf(w_ple_proj)
    sgu_bias = jnp.repeat(jnp.swapaxes(sgu_b, 1, 2), SGU_GROUP_DIM, axis=2)
    sgu_norm = sgu_norm_g[:, None, :]
    p_tok = p.reshape(depth, T, p.shape[-1])
    row_vec = lambda g: g[None, :]

    tables = _rope_tables(positions, tm)
    xf = x.reshape(T, D)
    h = _rmsnorm(xf, row_vec(norm_mix_g[0]), tm)
    out = None
    for i in range(depth):
        (qa, ka, va), qkv = _attn_proj(h, w_fox, b_f, w_mix, seg_block["dil"], tables, i, B, S, tm_attn)
        o_a = _fox_attention(qa.reshape(B, S, -1), ka.reshape(B, S, -1), va.reshape(B, S, -1),
                             B, S, tq).reshape(T, FOX_WIDTH)
        o_b, o_c = _conv_sgu_branches(h, w_mix, seg_block["conv"], seg_block["sgu"], shortconv_w, sgu_norm, sgu_w,
                                      sgu_bias, i, S, tm)
        outs, lses = [], []
        for g, (window, dil) in enumerate(DIL_PATTERNS):
            assert window // dil == DIL_SPAN
            o_g, l_g = _dil_attention(*qkv[3 * g:3 * g + 3], dil)
            outs.append(o_g)
            lses.append(l_g)
        o_d = _dil_merge(outs, lses, B, S, tm)

        merged = _gated_merge(h, (o_a, o_b, o_c, o_d), w_mix, seg_block["gate"], w_brs, i, tm_big, tn_merge)
        xf, h = _residual_proj(merged, w_out_b, xf, row_vec(norm_ffn_g[i]), i, tm_wide)

        act = _ffn_up(h, w_up, ffn_conv_w, i, tm_big, tn_ffn, S // tm_big)
        final = i == depth - 1
        g_next = final_norm_g if final else norm_mix_g[i + 1]
        res = _ffn_down_ple(act, w_down_b, xf, row_vec(norm_ple_g[i]), w_pg_b, p_tok, w_pp_b, row_vec(g_next), i,
                            tm_down, final)
        if final:
            out = res[0]
        else:
            xf, h = res
    return out.reshape(B, S, D)
```

```python
import functools

import jax
import jax.numpy as jnp
from jax import lax
from jax.experimental import pallas as pl
from jax.experimental.pallas import tpu as pltpu

F32 = jnp.float32
BF16 = jnp.bfloat16

HEAD_DIM = 64
EPS = 1e-6
FOX_HEADS = 8
FOX_WIDTH = FOX_HEADS * HEAD_DIM
CONV_WIDTH = 512
CONV_TAPS = 3
SGU_GROUPS = 4
SGU_GROUP_DIM = 128
SGU_WIDTH = SGU_GROUPS * SGU_GROUP_DIM
SGU_CHUNK = 128
DIL_PATTERNS = ((128, 1), (512, 4), (2048, 16))
DIL_HEADS_PER_GROUP = 4
DIL_HEADS = DIL_HEADS_PER_GROUP * len(DIL_PATTERNS)
DIL_WIDTH = DIL_HEADS * HEAD_DIM
DIL_OUT = DIL_HEADS_PER_GROUP * HEAD_DIM
DIL_SPAN = 128
ROPE_THETA = 500000.0
ROPE_DIM = HEAD_DIM // 4
N_BRANCH = 4

LANES = 128
SUBLANES = 8
BF16_ROWS = 16
HEADS_PER_SLAB = 4
SLAB = HEADS_PER_SLAB * HEAD_DIM
VMEM_LIMIT = 56 * 1024 * 1024
NEG = -1e30
QK_SCALE = HEAD_DIM ** -0.5


def _params(n_axes, fuse_inputs=None):
    return pltpu.CompilerParams(dimension_semantics=("arbitrary",) * n_axes, vmem_limit_bytes=VMEM_LIMIT,
                                allow_input_fusion=fuse_inputs)


def _resident(shape, index=None):
    index = (0,) * len(shape) if index is None else index
    return pl.BlockSpec(shape, lambda *_: index, pipeline_mode=pl.Buffered(1))


def _layer_weight(layer, rows, cols):
    return _resident((None, rows, cols), (layer, 0, 0))


def _tile(n, pref):
    t = min(n, pref)
    assert n % t == 0, (n, t)
    return t


def _rms(x, g):
    var = jnp.mean(x * x, axis=-1, keepdims=True)
    return x * lax.rsqrt(var + EPS) * g


def _dot(a, b):
    return jnp.dot(a, b, preferred_element_type=F32)


def _dot_nt(a, b):
    return lax.dot_general(a, b, (((1,), (1,)), ((), ())), preferred_element_type=F32)


def _causal_conv3(z, w, prev):
    y = w[0:1] * pltpu.roll(z, 2, 0) + w[1:2] * pltpu.roll(z, 1, 0) + w[2:3] * z
    zh = z[0:BF16_ROWS]
    row = lax.broadcasted_iota(jnp.int32, zh.shape, 0)
    p1 = prev[SUBLANES - 1:SUBLANES]
    p2 = prev[SUBLANES - 2:SUBLANES - 1]
    z1 = jnp.where(row == 0, p1, pltpu.roll(zh, 1, 0))
    z2 = jnp.where(row == 0, p2, jnp.where(row == 1, p1, pltpu.roll(zh, 2, 0)))
    yh = w[0:1] * z2 + w[1:2] * z1 + w[2:3] * zh
    return jnp.concatenate([yh, y[BF16_ROWS:]], axis=0)


def _rmsnorm_kernel(x_ref, g_ref, h_ref):
    h_ref[...] = _rms(x_ref[...], g_ref[...]).astype(h_ref.dtype)


def _rmsnorm(x, g, tm):
    T, D = x.shape
    return pl.pallas_call(
        _rmsnorm_kernel,
        out_shape=jax.ShapeDtypeStruct((T, D), BF16),
        grid=(T // tm,),
        in_specs=[pl.BlockSpec((tm, D), lambda i: (i, 0)), _resident((1, D))],
        out_specs=pl.BlockSpec((tm, D), lambda i: (i, 0)),
        compiler_params=_params(1),
        name="rmsnorm",
    )(x, g)


def _rope_table_kernel(pos_ref, inv_ref, c_ref, s1_ref, s2_ref):
    ang = pos_ref[...] * inv_ref[...]
    cos = jnp.cos(ang)
    sin = jnp.sin(ang)
    lane = lax.broadcasted_iota(jnp.int32, ang.shape, 1) % HEAD_DIM
    half = ROPE_DIM // 2
    c_ref[...] = jnp.where(lane < ROPE_DIM, cos, 1.0)
    s1_ref[...] = jnp.where((lane >= half) & (lane < ROPE_DIM), sin, 0.0)
    s2_ref[...] = jnp.where(lane < half, -sin, 0.0)


def _rope_tables(positions, tm):
    T = positions.size
    half = ROPE_DIM // 2
    inv = ROPE_THETA ** (-jnp.arange(half, dtype=F32) * (2.0 / ROPE_DIM))
    lane = jnp.arange(LANES) % HEAD_DIM
    inv_lane = jnp.where(lane < ROPE_DIM, inv[lane % half], 0.0).astype(F32)[None, :]
    posf = jnp.broadcast_to(positions.reshape(T, 1).astype(F32), (T, LANES))
    spec = pl.BlockSpec((tm, LANES), lambda i: (i, 0))
    return pl.pallas_call(
        _rope_table_kernel,
        out_shape=[jax.ShapeDtypeStruct((T, LANES), F32)] * 3,
        grid=(T // tm,),
        in_specs=[spec, _resident((1, LANES))],
        out_specs=[spec] * 3,
        compiler_params=_params(1),
        name="rope_tables",
    )(posf, inv_lane)


def _repack_kernel(wt_ref, fox_ref, mix_ref, *, fox_width, moves, mix_width):
    def rows_of(src, width):
        return wt_ref[src:src + width, :].T.astype(BF16)

    fox_ref[...] = rows_of(0, fox_width)
    end = 0
    for src, width, dst in moves:
        if dst > end:
            mix_ref[:, end:dst] = jnp.zeros((mix_ref.shape[0], dst - end), BF16)
        mix_ref[:, dst:dst + width] = rows_of(src, width)
        end = dst + width
    assert end == mix_width


def _repack_w_in(w_in, fox_width, moves, mix_width):
    depth, D, d_in = w_in.shape
    assert all(src % SUBLANES == 0 for src, _, _ in moves)
    tr = _tile(D, LANES)
    return pl.pallas_call(
        functools.partial(_repack_kernel, fox_width=fox_width, moves=moves, mix_width=mix_width),
        out_shape=[jax.ShapeDtypeStruct((depth, D, fox_width), BF16),
                   jax.ShapeDtypeStruct((depth, D, mix_width), BF16)],
        grid=(depth, D // tr),
        in_specs=[pl.BlockSpec((None, d_in, tr), lambda l, r: (l, 0, r))],
        out_specs=[pl.BlockSpec((None, tr, fox_width), lambda l, r: (l, r, 0)),
                   pl.BlockSpec((None, tr, mix_width), lambda l, r: (l, r, 0))],
        compiler_params=_params(2),
        name="repack_w_in",
    )(jnp.swapaxes(w_in, 1, 2))


def _fox_proj_kernel(h_ref, w_ref, wf_ref, bf_ref, q_ref, k_ref, v_ref, carry_ref, *, tiles_per_seq):
    @pl.when(pl.program_id(0) % tiles_per_seq == 0)
    def _():
        carry_ref[...] = jnp.zeros_like(carry_ref)

    h = h_ref[...]
    W = FOX_WIDTH
    q_ref[...] = (_dot(h, w_ref[:, 0:W]) * QK_SCALE).astype(BF16)
    k = _dot(h, w_ref[:, W:2 * W]).astype(BF16)
    v = _dot(h, w_ref[:, 2 * W:3 * W]).astype(BF16)

    x = jax.nn.log_sigmoid(_dot(h, wf_ref[...]) + bf_ref[...])
    tm = x.shape[0]
    row = lax.broadcasted_iota(jnp.int32, x.shape, 0)
    lane = lax.broadcasted_iota(jnp.int32, x.shape, 1)
    s = 1
    while s < tm:
        x = x + jnp.where(row >= s, pltpu.roll(x, s, 0), 0.0)
        s *= 2
    x = x + carry_ref[0:1, :]
    carry_ref[...] = jnp.broadcast_to(x[tm - 1:tm, :], carry_ref.shape)

    x = jnp.where(lane < FOX_HEADS, x, 0.0)
    hi = x.astype(BF16).astype(F32)
    mid = (x - hi).astype(BF16).astype(F32)
    lo = (x - hi - mid).astype(BF16).astype(F32)
    f_lanes = (hi + pltpu.roll(mid, FOX_HEADS, 1) + pltpu.roll(lo, 2 * FOX_HEADS, 1)).astype(BF16)
    one_lane = jnp.where(lane == 0, 1.0, 0.0).astype(BF16)
    for j in range(FOX_WIDTH // LANES):
        k_ref[:, j * SLAB:j * SLAB + LANES] = k[:, j * LANES:(j + 1) * LANES]
        k_ref[:, j * SLAB + LANES:(j + 1) * SLAB] = f_lanes
        v_ref[:, j * SLAB:j * SLAB + LANES] = v[:, j * LANES:(j + 1) * LANES]
        v_ref[:, j * SLAB + LANES:(j + 1) * SLAB] = one_lane


def _conv_branch_finish(dots, cw_ref, o_ref, prev_ref):
    xb, gate_b, gate_c = dots
    tm = xb.shape[0]
    z = gate_c * xb
    y = _causal_conv3(z, cw_ref[...], prev_ref[...])
    prev_ref[...] = z[tm - SUBLANES:, :]
    o_ref[...] = (gate_b * y).astype(BF16)


def _sgu_branch_finish(dots, ng_ref, ws_ref, bias_ref, o_ref):
    C = SGU_CHUNK
    G = SGU_GROUP_DIM
    u = jax.nn.gelu(dots[0])
    v = _rms(jax.nn.gelu(dots[1]), ng_ref[...]).astype(BF16)
    row = lax.broadcasted_iota(jnp.int32, (C, C), 0)
    col = lax.broadcasted_iota(jnp.int32, (C, C), 1)
    for g in range(SGU_GROUPS):
        w_tril = jnp.where(col <= row, ws_ref[g], 0.0).astype(BF16)
        cols = slice(g * G, (g + 1) * G)
        for c in range(u.shape[0] // C):
            rows = slice(c * C, (c + 1) * C)
            mixed = _dot(w_tril, v[rows, cols]) + bias_ref[:, cols]
            o_ref[rows, cols] = (u[rows, cols] * mixed).astype(BF16)


def _column_dots(h, w_ref, width, n):
    return [_dot(h, w_ref[:, j * width:(j + 1) * width]) for j in range(n)]


def _dil_proj_kernel(h_ref, w_ref, c_ref, s1_ref, s2_ref, *refs):
    n_groups = len(DIL_PATTERNS)
    out_refs = refs[:3 * n_groups]
    stage_ref = refs[3 * n_groups]
    h = h_ref[...]
    tm = h.shape[0]
    cos = c_ref[...]
    s1 = s1_ref[...]
    s2 = s2_ref[...]
    shift = ROPE_DIM // 2
    slot = 0
    kinds = ((0, True, QK_SCALE), (1, True, None), (2, False, None))
    wide = [_dot(h, w_ref[:, which * DIL_WIDTH:(which + 1) * DIL_WIDTH]) for which, _, _ in kinds]
    for (which, rope, scale), x_all in zip(kinds, wide):
        for g, (_, d) in enumerate(DIL_PATTERNS):
            parts = []
            for j in range(SLAB // LANES):
                xj = x_all[:, g * SLAB + j * LANES:g * SLAB + (j + 1) * LANES]
                if rope:
                    xj = xj * cos + pltpu.roll(xj, shift, 1) * s1 + pltpu.roll(xj, LANES - shift, 1) * s2
                if scale is not None:
                    xj = xj * scale
                parts.append(xj)
            out = out_refs[g * 3 + which]
            for j, xj in enumerate(parts):
                lanes = slice(j * LANES, (j + 1) * LANES)
                if d == 1:
                    out[0, 0, :, lanes] = xj.astype(BF16)
                else:
                    stage_ref[slot] = xj
                    for r in range(d):
                        out[0, r, :, lanes] = stage_ref[slot, pl.ds(r, tm // d, stride=d), :].astype(BF16)
                    slot += 1


N_FOX_PROJ_IN, N_FOX_PROJ_OUT = 3, 3


def _attn_proj_kernel(h_ref, *refs, tiles_per_seq):
    n_dil_out = 3 * len(DIL_PATTERNS)
    fox_in, refs = refs[:N_FOX_PROJ_IN], refs[N_FOX_PROJ_IN:]
    dil_in, refs = refs[:4], refs[4:]
    fox_out, refs = refs[:N_FOX_PROJ_OUT], refs[N_FOX_PROJ_OUT:]
    dil_out, (carry_ref, stage_ref) = refs[:n_dil_out], refs[n_dil_out:]
    _fox_proj_kernel(h_ref, *fox_in, *fox_out, carry_ref, tiles_per_seq=tiles_per_seq)
    _dil_proj_kernel(h_ref, *dil_in, *dil_out, stage_ref)


def _attn_proj(h, w_fox, b_f, w_mix, dil_block, tables, layer, B, S, tm):
    T, D = h.shape
    W = FOX_WIDTH
    n_pairs = FOX_WIDTH // LANES
    tiles_per_seq = S // tm
    row = lambda n: pl.BlockSpec((tm, n), lambda i: (i, 0))
    out_shape = [jax.ShapeDtypeStruct((T, W), BF16)] + [jax.ShapeDtypeStruct((T, n_pairs * SLAB), BF16)] * 2
    out_specs = [row(W), row(n_pairs * SLAB), row(n_pairs * SLAB)]
    n_strided = 0
    for _, d in DIL_PATTERNS:
        assert (tm // d) % BF16_ROWS == 0
        n_strided += 3 * (SLAB // LANES) * (d > 1)
        for _ in range(3):
            out_shape.append(jax.ShapeDtypeStruct((B, d, S // d, SLAB), BF16))
            out_specs.append(pl.BlockSpec((1, d, tm // d, SLAB),
                                          lambda i: (i // tiles_per_seq, 0, i % tiles_per_seq, 0)))
    res = pl.pallas_call(
        functools.partial(_attn_proj_kernel, tiles_per_seq=tiles_per_seq),
        out_shape=out_shape,
        grid=(T // tm,),
        in_specs=[row(D), _resident((None, D, 3 * W), (layer, 0, 0)),
                  _resident((None, D, LANES), (layer, 0, 3 * W // LANES)), _layer_weight(layer, 1, LANES),
                  _resident((None, D, 3 * DIL_WIDTH), (layer, 0, dil_block)), row(LANES), row(LANES), row(LANES)],
        out_specs=out_specs,
        scratch_shapes=[pltpu.VMEM((SUBLANES, LANES), F32), pltpu.VMEM((n_strided, tm, LANES), F32)],
        compiler_params=_params(1),
        name="attn_proj",
    )(h, w_fox, w_fox, b_f, w_mix, *tables)
    return res[:3], res[3:]


def _conv_sgu_kernel(h_ref, wc_ref, cw_ref, ws_ref, ng_ref, wsp_ref, bias_ref, ob_ref, oc_ref, prev_ref, *,
                     tiles_per_seq):
    @pl.when(pl.program_id(0) % tiles_per_seq == 0)
    def _():
        prev_ref[...] = jnp.zeros_like(prev_ref)

    h = h_ref[...]
    conv_dots = _column_dots(h, wc_ref, CONV_WIDTH, 3)
    sgu_dots = _column_dots(h, ws_ref, SGU_WIDTH, 2)
    _conv_branch_finish(conv_dots, cw_ref, ob_ref, prev_ref)
    _sgu_branch_finish(sgu_dots, ng_ref, wsp_ref, bias_ref, oc_ref)


def _conv_sgu_branches(h, w_mix, conv_block, sgu_block, conv_w, norm_g, w_s, bias_full, layer, S, tm):
    T, D = h.shape
    row = lambda n: pl.BlockSpec((tm, n), lambda i: (i, 0))
    return pl.pallas_call(
        functools.partial(_conv_sgu_kernel, tiles_per_seq=S // tm),
        out_shape=[jax.ShapeDtypeStruct((T, CONV_WIDTH), BF16), jax.ShapeDtypeStruct((T, SGU_WIDTH), BF16)],
        grid=(T // tm,),
        in_specs=[row(D), _resident((None, D, 3 * CONV_WIDTH), (layer, 0, conv_block)),
                  _layer_weight(layer, CONV_TAPS, CONV_WIDTH),
                  _resident((None, D, 2 * SGU_WIDTH), (layer, 0, sgu_block)), _layer_weight(layer, 1, SGU_WIDTH),
                  _resident((None, SGU_GROUPS, SGU_CHUNK, SGU_CHUNK), (layer, 0, 0, 0)),
                  _layer_weight(layer, SGU_CHUNK, SGU_WIDTH)],
        out_specs=[row(CONV_WIDTH), row(SGU_WIDTH)],
        scratch_shapes=[pltpu.VMEM((SUBLANES, CONV_WIDTH), F32)],
        compiler_params=_params(1),
        name="conv_sgu_branches",
    )(h, w_mix, conv_w, w_mix, norm_g, w_s, bias_full)


def _slab_head_masks(n_rows):
    lane = lax.broadcasted_iota(jnp.int32, (n_rows, SLAB), 1)
    return [(lane >= h * HEAD_DIM) & (lane < (h + 1) * HEAD_DIM) for h in range(HEADS_PER_SLAB)]


FOX_BLOCKS_PER_ITER = 4
FOX_PAIRS_PER_STEP = 2


def _fox_attn_kernel(q_ref, k_ref, v_ref, o_ref, *, tq):
    group = pl.program_id(1)
    i = pl.program_id(2)
    q0 = pl.multiple_of(i * tq, tq)
    lane = lax.broadcasted_iota(jnp.int32, (tq, LANES), 1)
    rowi = lax.broadcasted_iota(jnp.int32, (2 * tq, tq), 0) % tq
    coli = lax.broadcasted_iota(jnp.int32, (2 * tq, tq), 1)

    def stacked_queries(c):
        q = q_ref[0, :, c * LANES:(c + 1) * LANES]
        zero = jnp.zeros_like(q)
        halves = []
        for e in range(2):
            head = 2 * (FOX_PAIRS_PER_STEP * group + c) + e
            f_pick = (lane == head) | (lane == head + FOX_HEADS) | (lane == head + 2 * FOX_HEADS)
            halves.append(jnp.concatenate(
                [jnp.where(lane // HEAD_DIM == e, q, zero), jnp.where(f_pick, -1.0, 0.0).astype(BF16)], axis=1))
        return jnp.concatenate(halves, axis=0)

    qs = [stacked_queries(c) for c in range(FOX_PAIRS_PER_STEP)]

    def block(k0, carry, diagonal):
        slabs = [slice(c * SLAB, (c + 1) * SLAB) for c in range(FOX_PAIRS_PER_STEP)]
        logits = [_dot_nt(qs[c], k_ref[0, pl.ds(k0, tq), slabs[c]]) for c in range(FOX_PAIRS_PER_STEP)]
        stats = []
        for (m, _), s in zip(carry, logits):
            if diagonal:
                s = jnp.where(coli <= rowi, s, NEG)
            m_new = jnp.maximum(m, jnp.max(s, axis=-1, keepdims=True))
            stats.append((m_new, jnp.exp(m - m_new), jnp.exp(s - m_new).astype(BF16)))
        return tuple((m_new, alpha * acc + _dot(p, v_ref[0, pl.ds(k0, tq), slabs[c]]))
                     for c, ((_, acc), (m_new, alpha, p)) in enumerate(zip(carry, stats)))

    init = tuple((jnp.full((2 * tq, 1), NEG, F32), jnp.zeros((2 * tq, SLAB), F32))
                 for _ in range(FOX_PAIRS_PER_STEP))
    def run_blocks(first_block, n, carry):
        for j in range(n):
            carry = block(pl.multiple_of((first_block + j) * tq, tq), carry, False)
        return carry

    def finish(carry):
        for c, (_, acc) in enumerate(block(q0, carry, True)):
            out = acc[:, 0:LANES] / acc[:, LANES:LANES + 1]
            o_ref[0, :, c * LANES:(c + 1) * LANES] = jnp.where(lane // HEAD_DIM == 0, out[0:tq],
                                                               out[tq:2 * tq]).astype(BF16)

    U = FOX_BLOCKS_PER_ITER
    carry = lax.fori_loop(0, i // U, lambda g, c: run_blocks(g * U, U, c), init)
    for rem in range(U):
        @pl.when(i % U == rem)
        def _(rem=rem):
            finish(run_blocks(i - rem, rem, carry))


def _fox_attention(q, k_slabs, v_slabs, B, S, tq):
    n_groups = FOX_WIDTH // LANES // FOX_PAIRS_PER_STEP
    kv_spec = pl.BlockSpec((1, S, FOX_PAIRS_PER_STEP * SLAB), lambda b, g, i: (b, 0, g))
    q_spec = pl.BlockSpec((1, tq, FOX_PAIRS_PER_STEP * LANES), lambda b, g, i: (b, i, g))
    return pl.pallas_call(
        functools.partial(_fox_attn_kernel, tq=tq),
        out_shape=jax.ShapeDtypeStruct((B, S, FOX_WIDTH), BF16),
        grid=(B, n_groups, S // tq),
        in_specs=[q_spec, kv_spec, kv_spec],
        out_specs=q_spec,
        compiler_params=_params(3),
        name="fox_attention",
    )(q, k_slabs, v_slabs)


assert DIL_OUT == SLAB
DIL_BLOCKS_PER_STEP = 16


def _dil_attn_kernel(q_ref, k_ref, v_ref, o_ref, lse_ref, *, n_blocks):
    P = DIL_SPAN
    H = HEADS_PER_SLAB
    masks = _slab_head_masks(P)

    def attend(r, q_rows, k_rows, valid):
        q = q_ref[r, q_rows, :]
        zero = jnp.zeros_like(q)
        qs = jnp.concatenate([jnp.where(masks[h], q, zero) for h in range(H)], axis=0)
        s = jnp.where(valid, _dot_nt(qs, k_ref[r, k_rows, :]), NEG)
        m = jnp.max(s, axis=-1, keepdims=True)
        p = jnp.exp(s - m)
        l = jnp.sum(p, axis=-1, keepdims=True)
        pv = _dot(p.astype(BF16), v_ref[r, k_rows, :]) / l
        lse = jnp.broadcast_to(m + jnp.log(l), (H * P, SLAB))
        o, ls = pv[0:P], lse[0:P]
        for h in range(1, H):
            o = jnp.where(masks[h], pv[h * P:(h + 1) * P], o)
            ls = jnp.where(masks[h], lse[h * P:(h + 1) * P], ls)
        o_ref[r, q_rows, :] = o
        lse_ref[r, q_rows, :] = ls

    qi = lax.broadcasted_iota(jnp.int32, (H * P, P), 0) % P
    ki = lax.broadcasted_iota(jnp.int32, (H * P, P), 1)
    qi2 = lax.broadcasted_iota(jnp.int32, (H * P, 2 * P), 0) % P
    ki2 = lax.broadcasted_iota(jnp.int32, (H * P, 2 * P), 1)
    valid2 = (ki2 >= qi2) & (ki2 <= qi2 + P)

    for r in range(q_ref.shape[0]):
        attend(r, pl.ds(0, P), pl.ds(0, P), ki <= qi)

        def body(n, _, r=r):
            start = pl.multiple_of(n * P, P)
            attend(r, pl.ds(start, P), pl.ds(pl.multiple_of(start - P, P), 2 * P), valid2)
            return 0

        lax.fori_loop(1, n_blocks, body, 0, unroll=min(n_blocks - 1, DIL_BLOCKS_PER_STEP))


def _dil_attention(q, k, v, dilation):
    B, d, L, _ = q.shape
    assert d == dilation
    n_blocks = L // DIL_SPAN
    per_step = max(1, min(d, DIL_BLOCKS_PER_STEP // n_blocks))
    spec = pl.BlockSpec((None, per_step, L, SLAB), lambda b, r: (b, r, 0, 0))
    return pl.pallas_call(
        functools.partial(_dil_attn_kernel, n_blocks=n_blocks),
        out_shape=[jax.ShapeDtypeStruct((B, d, L, SLAB), F32)] * 2,
        grid=(B, d // per_step),
        in_specs=[spec] * 3,
        out_specs=[spec] * 2,
        compiler_params=_params(2),
        name=f"dil_attention_d{dilation}",
    )(q, k, v)


def _dil_merge_kernel(*refs):
    n_groups = len(DIL_PATTERNS)
    o_refs, l_refs = refs[:n_groups], refs[n_groups:2 * n_groups]
    out_ref, stage_ref = refs[2 * n_groups], refs[2 * n_groups + 1]
    tm = out_ref.shape[0]
    slot = 0
    outs, lses = [], []
    for src, dst in ((o_refs, outs), (l_refs, lses)):
        for (_, d), ref in zip(DIL_PATTERNS, src):
            if d == 1:
                dst.append(ref[0, 0])
                continue
            halves = []
            for j in range(SLAB // LANES):
                for r in range(d):
                    stage_ref[slot, pl.ds(r, tm // d, stride=d), :] = ref[0, r, :, j * LANES:(j + 1) * LANES]
                halves.append(stage_ref[slot])
                slot += 1
            dst.append(jnp.concatenate(halves, axis=1))
    m = functools.reduce(jnp.maximum, lses)
    es = [jnp.exp(l - m) for l in lses]
    num = es[0] * outs[0] + es[1] * outs[1] + es[2] * outs[2]
    out_ref[...] = (num / (es[0] + es[1] + es[2])).astype(BF16)


def _dil_merge(outs, lses, B, S, tm):
    T = B * S
    tiles_per_seq = S // tm
    specs = [pl.BlockSpec((1, d, tm // d, SLAB), lambda i: (i // tiles_per_seq, 0, i % tiles_per_seq, 0))
             for _, d in DIL_PATTERNS]
    n_strided = 2 * (SLAB // LANES) * sum(d > 1 for _, d in DIL_PATTERNS)
    return pl.pallas_call(
        _dil_merge_kernel,
        out_shape=jax.ShapeDtypeStruct((T, SLAB), BF16),
        grid=(T // tm,),
        in_specs=specs * 2,
        out_specs=pl.BlockSpec((tm, SLAB), lambda i: (i, 0)),
        scratch_shapes=[pltpu.VMEM((n_strided, tm, LANES), F32)],
        compiler_params=_params(1),
        name="dil_merge",
    )(*outs, *lses)


def _gated_merge_kernel(h_ref, oa_ref, ob_ref, oc_ref, od_ref, g0, g1, g2, g3, wa, wb, wc, wd, out_ref):
    h = h_ref[...]
    acc = None
    for o_ref, g_ref, w_ref in ((oa_ref, g0, wa), (ob_ref, g1, wb), (oc_ref, g2, wc), (od_ref, g3, wd)):
        term = jax.nn.sigmoid(_dot(h, g_ref[...])) * _dot(o_ref[...], w_ref[...])
        acc = term if acc is None else acc + term
    out_ref[...] = acc.astype(BF16)


def _gated_merge(h, branches, w_mix, gate_block, w_branches, layer, tm, tn):
    T, D = h.shape
    n_col = D // tn
    row = lambda n: pl.BlockSpec((tm, n), lambda j, i: (i, 0))
    gate_specs = [pl.BlockSpec((None, D, tn), functools.partial(lambda j, i, br: (layer, 0, gate_block + br * n_col + j), br=br))
                  for br in range(N_BRANCH)]
    w_specs = [pl.BlockSpec((w.shape[0], tn), lambda j, i: (0, j)) for w in w_branches]
    return pl.pallas_call(
        _gated_merge_kernel,
        out_shape=jax.ShapeDtypeStruct((T, D), BF16),
        grid=(n_col, T // tm),
        in_specs=[row(D)] + [row(o.shape[1]) for o in branches] + gate_specs + w_specs,
        out_specs=pl.BlockSpec((tm, tn), lambda j, i: (i, j)),
        compiler_params=_params(2, [False] * 9 + [True] * len(w_branches)),
        name="gated_merge",
    )(h, *branches, w_mix, w_mix, w_mix, w_mix, *w_branches)


def _residual_proj_kernel(a_ref, w_ref, x_ref, g_ref, xo_ref, h_ref):
    x = x_ref[...] + _dot(a_ref[...], w_ref[...])
    xo_ref[...] = x
    h_ref[...] = _rms(x, g_ref[...]).astype(h_ref.dtype)


def _residual_proj(a, w, x, g_next, layer, tm):
    T, D = x.shape
    K = a.shape[1]
    row = lambda n: pl.BlockSpec((tm, n), lambda i: (i, 0))
    return pl.pallas_call(
        _residual_proj_kernel,
        out_shape=[jax.ShapeDtypeStruct((T, D), F32), jax.ShapeDtypeStruct((T, D), BF16)],
        grid=(T // tm,),
        in_specs=[row(K), _layer_weight(layer, K, D), row(D), _resident((1, D))],
        out_specs=[row(D), row(D)],
        compiler_params=_params(1, [False, True, False, False]),
        name=f"residual_proj_k{K}",
    )(a, w, x, g_next)


FFN_CHUNK = 2 * SLAB
SHIFT_PAD = 16


def _shifted_rows(z, s_ref):
    tm = z.shape[0]
    half = tm // 2
    P = SHIFT_PAD
    s_ref[P - 2:P - 1, :] = s_ref[P + tm - 1:P + tm, :]
    s_ref[P - 4:P - 3, :] = s_ref[P + tm - 3:P + tm - 2, :]
    s_ref[pl.ds(P, half, stride=2), :] = z[:half]
    s_ref[pl.ds(P + 1, half, stride=2), :] = z[half:]
    s_ref[P - 1:P, :] = z[half - 1:half]
    s_ref[P - 3:P - 2, :] = z[half - 2:half - 1]
    z1 = jnp.concatenate([s_ref[pl.ds(P - 2, half, stride=2), :], s_ref[pl.ds(P - 1, half, stride=2), :]], axis=0)
    z2 = jnp.concatenate([s_ref[pl.ds(P - 4, half, stride=2), :], s_ref[pl.ds(P - 3, half, stride=2), :]], axis=0)
    return z1, z2


def _ffn_up_kernel(h_ref, wg_ref, wv_ref, cg_ref, cv_ref, act_ref, wg_bf, wv_bf, shift_ref, *, tiles_per_seq):
    i = pl.program_id(1)
    tm = h_ref.shape[0]

    @pl.when(i == 0)
    def _():
        wg_bf[...] = wg_ref[...].astype(BF16)
        wv_bf[...] = wv_ref[...].astype(BF16)

    @pl.when(i % tiles_per_seq == 0)
    def _():
        shift_ref[:, SHIFT_PAD + tm - SUBLANES:, :] = jnp.zeros((shift_ref.shape[0], SUBLANES, LANES), F32)

    h = h_ref[...]

    def conv(z, w_ref, col, buf):
        parts = []
        for j in range(FFN_CHUNK // LANES):
            zj = z[:, j * LANES:(j + 1) * LANES]
            z1, z2 = _shifted_rows(zj, shift_ref.at[buf + j])
            w = w_ref[:, col + j * LANES:col + (j + 1) * LANES]
            parts.append(w[0:1] * z2 + w[1:2] * z1 + w[2:3] * zj)
        return jnp.concatenate(parts, axis=1)

    lane_blocks = act_ref.shape[1] // LANES
    for c in range(act_ref.shape[1] // FFN_CHUNK):
        col = c * FFN_CHUNK
        cols = slice(col, col + FFN_CHUNK)
        up_gate = conv(_dot(h, wg_bf[:, cols]), cg_ref, col, col // LANES)
        up_val = conv(_dot(h, wv_bf[:, cols]), cv_ref, col, lane_blocks + col // LANES)
        act_ref[:, cols] = (jax.nn.silu(up_gate) * up_val).astype(BF16)


def _ffn_up(h, w_up, conv_w, layer, tm, tn, tiles_per_seq):
    T, D = h.shape
    d_ff = w_up.shape[2] // 2
    n_col = d_ff // tn
    return pl.pallas_call(
        functools.partial(_ffn_up_kernel, tiles_per_seq=tiles_per_seq),
        out_shape=jax.ShapeDtypeStruct((T, d_ff), BF16),
        grid=(n_col, T // tm),
        in_specs=[pl.BlockSpec((tm, D), lambda j, i: (i, 0)),
                  pl.BlockSpec((None, D, tn), lambda j, i: (layer, 0, j)),
                  pl.BlockSpec((None, D, tn), lambda j, i: (layer, 0, n_col + j)),
                  pl.BlockSpec((None, CONV_TAPS, tn), lambda j, i: (layer, 0, j)),
                  pl.BlockSpec((None, CONV_TAPS, tn), lambda j, i: (layer, 0, n_col + j))],
        out_specs=pl.BlockSpec((tm, tn), lambda j, i: (i, j)),
        scratch_shapes=[pltpu.VMEM((D, tn), BF16)] * 2 + [pltpu.VMEM((2 * tn // LANES, SHIFT_PAD + tm, LANES), F32)],
        compiler_params=_params(2),
        name="ffn_up",
    )(h, w_up, w_up, conv_w, conv_w)


def _ffn_down_ple_kernel(a_ref, wd_ref, x_ref, gp_ref, wg_ref, p_ref, wp_ref, gn_ref, *out_refs, final):
    down = _dot(a_ref[...], wd_ref[...])
    embed = _dot(p_ref[...].astype(BF16), wp_ref[...])
    x = x_ref[...] + down
    h = _rms(x, gp_ref[...]).astype(BF16)
    gate = jax.nn.sigmoid(_dot(h, wg_ref[...]))
    x = x + gate * embed
    if not final:
        out_refs[0][...] = x
    out_refs[-1][...] = _rms(x, gn_ref[...]).astype(out_refs[-1].dtype)


def _ffn_down_ple(act, w_down, x, g_ple, w_gate, p, w_proj, g_next, layer, tm, final):
    T, D = x.shape
    K = act.shape[1]
    P = p.shape[2]
    row = lambda n: pl.BlockSpec((tm, n), lambda i: (i, 0))
    if final:
        out_shape = [jax.ShapeDtypeStruct((T, D), F32)]
    else:
        out_shape = [jax.ShapeDtypeStruct((T, D), F32), jax.ShapeDtypeStruct((T, D), BF16)]
    return pl.pallas_call(
        functools.partial(_ffn_down_ple_kernel, final=final),
        out_shape=out_shape,
        grid=(T // tm,),
        in_specs=[row(K), _layer_weight(layer, K, D), row(D), _resident((1, D)), _layer_weight(layer, D, D),
                  pl.BlockSpec((None, tm, P), lambda i: (layer, i, 0)), _layer_weight(layer, P, D),
                  _resident((1, D))],
        out_specs=[row(D)] * len(out_shape),
        compiler_params=_params(1, [False, True, False, False, True, False, True, False]),
        name="ffn_down_ple_final" if final else "ffn_down_ple",
    )(act, w_down, x, g_ple, w_gate, p, w_proj, g_next)


def kernel(x, p, positions, norm_mix_g, w_in, fox_forget_b, shortconv_w, sgu_norm_g, sgu_w, sgu_b, w_br_fox, w_br_conv, w_br_sgu, w_br_dil, w_out, norm_ffn_g, w_up, ffn_conv_w, w_down, norm_ple_g, w_ple_gate, w_ple_proj, final_norm_g):
    B, S, D = x.shape
    depth = w_in.shape[0]
    T = B * S
    d_ff = w_down.shape[1]

    tm = _tile(S, 1024)
    tm_attn = _tile(S, 512)
    tm_big = _tile(S, 1024)
    tm_wide = _tile(S, 512)
    tm_down = _tile(S, 256)
    tq = _tile(S, 512)
    tn_merge = _tile(D, 512)
    tn_ffn = _tile(d_ff, 512)

    sizes = (3 * FOX_WIDTH, FOX_HEADS, 3 * CONV_WIDTH, 2 * SGU_WIDTH, 3 * DIL_WIDTH, N_BRANCH * D)
    offs = [0]
    for n in sizes:
        offs.append(offs[-1] + n)

    bf = lambda w: w.astype(BF16)
    b_f = jnp.pad(fox_forget_b.astype(F32), ((0, 0), (0, LANES - FOX_HEADS)))[:, None, :]
    moves, seg_block, pos = [], {}, 0
    for name, seg, block in (("gate", 5, tn_merge), ("sgu", 3, sizes[3]), ("conv", 2, sizes[2]), ("dil", 4, sizes[4])):
        pos += -pos % block
        seg_block[name] = pos // block
        moves.append((offs[seg], sizes[seg], pos))
        pos += sizes[seg]
    w_fox, w_mix = _repack_w_in(w_in, offs[1] + LANES, tuple(moves), pos)
    w_out_b, w_down_b = bf(w_out), bf(w_down)
    w_pg_b, w_pp_b = bf(w_ple_gate), bf(w_ple_proj)
    sgu_bias = jnp.repeat(jnp.swapaxes(sgu_b, 1, 2), SGU_GROUP_DIM, axis=2)
    sgu_norm = sgu_norm_g[:, None, :]
    p_tok = p.reshape(depth, T, p.shape[-1])
    row_vec = lambda g: g[None, :]

    tables = _rope_tables(positions, tm)
    xf = x.reshape(T, D)
    h = _rmsnorm(xf, row_vec(norm_mix_g[0]), tm)
    out = None
    for i in range(depth):
        (qa, ka, va), qkv = _attn_proj(h, w_fox, b_f, w_mix, seg_block["dil"], tables, i, B, S, tm_attn)
        o_a = _fox_attention(qa.reshape(B, S, -1), ka.reshape(B, S, -1), va.reshape(B, S, -1),
                             B, S, tq).reshape(T, FOX_WIDTH)
        o_b, o_c = _conv_sgu_branches(h, w_mix, seg_block["conv"], seg_block["sgu"], shortconv_w, sgu_norm, sgu_w,
                                      sgu_bias, i, S, tm)
        outs, lses = [], []
        for g, (window, dil) in enumerate(DIL_PATTERNS):
            assert window // dil == DIL_SPAN
            o_g, l_g = _dil_attention(*qkv[3 * g:3 * g + 3], dil)
            outs.append(o_g)
            lses.append(l_g)
        o_d = _dil_merge(outs, lses, B, S, tm)

        w_br_layer = [bf(w[i]) for w in (w_br_fox, w_br_conv, w_br_sgu, w_br_dil)]
        merged = _gated_merge(h, (o_a, o_b, o_c, o_d), w_mix, seg_block["gate"], w_br_layer, i, tm_big, tn_merge)
        xf, h = _residual_proj(merged, w_out_b, xf, row_vec(norm_ffn_g[i]), i, tm_wide)

        act = _ffn_up(h, w_up, ffn_conv_w, i, tm_big, tn_ffn, S // tm_big)
        final = i == depth - 1
        g_next = final_norm_g if final else norm_mix_g[i + 1]
        res = _ffn_down_ple(act, w_down_b, xf, row_vec(norm_ple_g[i]), w_pg_b, p_tok, w_pp_b, row_vec(g_next), i,
                            tm_down, final)
        if final:
            out = res[0]
        else:
            xf, h = res
    return out.reshape(B, S, D)
```
